```python
import functools
import jax, jax.numpy as jnp
from jax import lax
import numpy as np

D_MODEL = 2048
BATCH = 32
SEQ = 256
DEPTH = 4
DEC_BATCH = 2
DEC_SEQ = 1024
PAST_LEN = 512

GRID_W = 64
HEAD_DIM = 128
ATT_HEADS = 8
KV_HEADS = 2
Q_PER_KV = ATT_HEADS // KV_HEADS
ATT_DIM = ATT_HEADS * HEAD_DIM
KV_DIM = KV_HEADS * HEAD_DIM
WINDOW = 128
BLOCK = 128
ATT_SCALE = HEAD_DIM ** -0.5
ROPE_THETA = 10000.0
ROPE_FREQS = HEAD_DIM // 4
HG_HEADS = 8
HG_DK = 128
HG_DV = 128
HG_DIM = HG_HEADS * HG_DK
HG_VDIM = HG_HEADS * HG_DV
HG_CHUNK = 32
MIX_DIM = ATT_DIM + HG_VDIM
IN_DIM = ATT_DIM + 2 * KV_DIM + 3 * HG_DIM + 2 * HG_VDIM
D_FF = 4 * D_MODEL
LN_EPS = 1e-5
RMS_EPS = 1e-6
DEEPNORM_ALPHA = (2 * DEPTH) ** 0.25
DEEPNORM_BETA = (8 * DEPTH) ** -0.25
NEG_INF = -1e30
F32 = jnp.float32

kernel_name = 'hybrid_swa_hgrn2_diffusion_step'


def layer_norm(x, g, b):
    xf = x.astype(F32)
    xc = xf - jnp.mean(xf, -1, keepdims=True)
    var = jnp.mean(xc * xc, -1, keepdims=True)
    return (xc * lax.rsqrt(var + LN_EPS) * g.astype(F32) + b.astype(F32)).astype(x.dtype)


def rms_norm(x, g):
    xf = x.astype(F32)
    return (xf * lax.rsqrt(jnp.mean(xf * xf, -1, keepdims=True) + RMS_EPS) * g.astype(F32)).astype(x.dtype)


def modulation(c, w_mod_l, b_mod_l):
    return jnp.split(jax.nn.silu(c) @ w_mod_l + b_mod_l, 6, axis=-1)


def modulate(x, shift, scale):
    return x * (1 + scale) + shift


def split_in(z):
    cuts = np.cumsum([ATT_DIM, KV_DIM, KV_DIM, HG_DIM, HG_DIM, HG_DIM, HG_VDIM]).tolist()
    return jnp.split(z, cuts, axis=-1)


def axial_angles(n_tokens):
    rows = n_tokens // GRID_W
    row = jnp.repeat(jnp.arange(rows, dtype=F32), GRID_W)
    col = jnp.tile(jnp.arange(GRID_W, dtype=F32), rows)
    inv = ROPE_THETA ** (-jnp.arange(ROPE_FREQS, dtype=F32) / ROPE_FREQS)
    return row[:, None] * inv, col[:, None] * inv


def rope_axis(x, ang):
    cos = jnp.cos(ang).astype(x.dtype)[None, :, None, :]
    sin = jnp.sin(ang).astype(x.dtype)[None, :, None, :]
    x1, x2 = x[..., :ROPE_FREQS], x[..., ROPE_FREQS:]
    return jnp.concatenate([x1 * cos - x2 * sin, x1 * sin + x2 * cos], -1)


def axial_rope(x, ang_r, ang_c):
    half = HEAD_DIM // 2
    return jnp.concatenate([rope_axis(x[..., :half], ang_r), rope_axis(x[..., half:], ang_c)], -1)


def softmax_with_sink(s, sink):
    sk = sink.astype(F32).reshape(KV_HEADS, Q_PER_KV, 1, 1)
    m = jnp.maximum(jnp.max(s, -1, keepdims=True), sk)
    e = jnp.exp(s - m)
    return e / (jnp.sum(e, -1, keepdims=True) + jnp.exp(sk - m))


def context_attention(q, k, v, sink):
    B, T = q.shape[0], q.shape[1]
    nb = T // BLOCK
    qb = jnp.swapaxes(q.reshape(B, nb, BLOCK, KV_HEADS, Q_PER_KV, HEAD_DIM), 0, 1)

    def one_block(qblk):
        s = jnp.einsum('bqgrd,bkgd->bgrqk', qblk, k, preferred_element_type=F32) * ATT_SCALE
        p = softmax_with_sink(s, sink).astype(v.dtype)
        return jnp.einsum('bgrqk,bkgd->bqgrd', p, v)

    o = lax.map(one_block, qb)
    return jnp.swapaxes(o, 0, 1).reshape(B, T, ATT_DIM)


def latent_attention(q, k, v, sink, ang_r, ang_c, k_ctx, v_ctx):
    B, T = q.shape[0], q.shape[1]
    nb = T // BLOCK
    q = axial_rope(q, ang_r, ang_c)
    k = axial_rope(k, ang_r, ang_c)
    qb = q.reshape(B, nb, BLOCK, KV_HEADS, Q_PER_KV, HEAD_DIM)

    def neighbours(a):
        ap = jnp.pad(a, ((0, 0), (BLOCK, BLOCK), (0, 0), (0, 0))).reshape(B, nb + 2, BLOCK, KV_HEADS, HEAD_DIM)
        return jnp.concatenate([ap[:, :-2], ap[:, 1:-1], ap[:, 2:]], axis=2)

    kb, vb = neighbours(k), neighbours(v)
    qpos = jnp.arange(nb)[:, None] * BLOCK + jnp.arange(BLOCK)[None, :]
    kpos = jnp.arange(nb)[:, None] * BLOCK - BLOCK + jnp.arange(3 * BLOCK)[None, :]
    kp = kpos[:, None, :]
    valid = (kp >= 0) & (kp < T) & (jnp.abs(kp - qpos[:, :, None]) <= WINDOW)
    s_loc = jnp.einsum('bnqgrd,bnkgd->bngrqk', qb, kb, preferred_element_type=F32) * ATT_SCALE
    s_loc = jnp.where(valid[None, :, None, None], s_loc, NEG_INF)
    s_ctx = jnp.einsum('bnqgrd,bpgd->bngrqp', qb, k_ctx, preferred_element_type=F32) * ATT_SCALE
    p = softmax_with_sink(jnp.concatenate([s_loc, s_ctx], -1), sink).astype(v.dtype)
    o = (jnp.einsum('bngrqk,bnkgd->bnqgrd', p[..., :3 * BLOCK], vb)
         + jnp.einsum('bngrqp,bpgd->bnqgrd', p[..., 3 * BLOCK:], v_ctx))
    return o.reshape(B, T, ATT_DIM)


def hgrn_chunk_scan(q, k, v, log_f, s0):
    B, T, H, _ = q.shape
    n, C = T // HG_CHUNK, HG_CHUNK

    def chunks(a):
        return a.reshape(B, n, C, H, a.shape[-1]).astype(F32)

    qc, kc, vc = chunks(q), chunks(k), chunks(v)
    b = jnp.cumsum(chunks(log_f), axis=2)
    b_mid = b[:, :, C // 2 - 1:C // 2]
    b_last = b[:, :, C - 1:C]
    a = jnp.einsum('bnthk,bnshk->bnhts', qc * jnp.exp(b - b_mid), kc * jnp.exp(b_mid - b))
    a = jnp.where(jnp.tril(jnp.ones((C, C), bool)), a, 0.0)
    o_intra = jnp.einsum('bnhts,bnshv->bnthv', a, vc)
    decay = jnp.exp(b_last[:, :, 0])
    u = jnp.einsum('bnshk,bnshv->bnhkv', kc * jnp.exp(b_last - b), vc)

    def step(s, inp):
        d, uu = inp
        return d[..., None] * s + uu, s

    s_final, s_starts = lax.scan(step, s0.astype(F32), (jnp.moveaxis(decay, 1, 0), jnp.moveaxis(u, 1, 0)))
    o_inter = jnp.einsum('bnthk,bnhkv->bnthv', qc * jnp.exp(b), jnp.moveaxis(s_starts, 0, 1))
    o = (o_intra + o_inter).reshape(B, T, H, v.shape[-1])
    return o.astype(v.dtype), s_final.astype(v.dtype)


def hgrn_mixer(hq, hff, hfb, hi, hgt, lb_f, lb_b, norm_g, s_f0, s_b0):
    B, T, _ = hq.shape
    q = jax.nn.silu(hq).reshape(B, T, HG_HEADS, HG_DK)
    v = hi.reshape(B, T, HG_HEADS, HG_DV)

    def gates(z, lb):
        zf = z.reshape(B, T, HG_HEADS, HG_DK).astype(F32)
        lbh = lb.reshape(HG_HEADS, HG_DK)
        log_f = jnp.log(lbh + (1 - lbh) * jax.nn.sigmoid(zf))
        k = ((1 - lbh) * jax.nn.sigmoid(-zf)).astype(v.dtype)
        return log_f, k

    logf_f, k_f = gates(hff, lb_f)
    logf_b, k_b = gates(hfb, lb_b)
    o_f, s_f = hgrn_chunk_scan(q, k_f, v, logf_f, s_f0)
    flip = lambda a: jnp.flip(a, axis=1)
    o_b, s_b = hgrn_chunk_scan(flip(q), flip(k_b), flip(v), flip(logf_b), s_b0)
    o = rms_norm(o_f + flip(o_b), norm_g) * jax.nn.silu(hgt.reshape(B, T, HG_HEADS, HG_DV))
    return o.reshape(B, T, HG_VDIM), s_f, s_b


def lower_bounds(lb_logits):
    p = jax.nn.softmax(lb_logits.astype(F32), axis=0)
    cs = jnp.cumsum(p, axis=0)
    return cs - cs[0:1]


def sq_relu_mlp(h, w_up, w_down):
    return jnp.square(jax.nn.relu(h @ w_up)) @ w_down


def hybrid_layer(x, mods, w_in_l, sink_l, attn_g_l, lb_f_l, lb_b_l, hg_g_l, w_o_l, ln_g_l, ln_b_l,
                 w_up_l, w_down_l, s_f0, s_b0, attend):
    shift1, scale1, gate1, shift2, scale2, gate2 = mods
    B, T, _ = x.shape
    h = modulate(x, shift1, scale1)
    q, k, v, hq, hff, hfb, hi, hgt = split_in(h @ w_in_l)
    q = q.reshape(B, T, ATT_HEADS, HEAD_DIM)
    k = k.reshape(B, T, KV_HEADS, HEAD_DIM)
    v = v.reshape(B, T, KV_HEADS, HEAD_DIM)
    att = rms_norm(attend(q, k, v, sink_l), attn_g_l)
    hg, s_f, s_b = hgrn_mixer(hq, hff, hfb, hi, hgt, lb_f_l, lb_b_l, hg_g_l, s_f0, s_b0)
    mix = jnp.concatenate([att, hg], -1) @ w_o_l
    x = layer_norm(DEEPNORM_ALPHA * x + gate1 * mix, ln_g_l[0], ln_b_l[0])
    ffn = sq_relu_mlp(modulate(x, shift2, scale2), w_up_l, w_down_l)
    x = layer_norm(DEEPNORM_ALPHA * x + gate2 * ffn, ln_g_l[1], ln_b_l[1])
    return x, k, v, s_f, s_b


def setup_inputs(seed: int = 0) -> dict:
    key = jax.random.key(seed)
    ks = jax.random.split(key, 20)

    def nrm(k, shape, s):
        return jax.random.normal(k, shape, F32) * s

    return {
        'x_prompt': nrm(ks[0], (BATCH, SEQ, D_MODEL), 1.0),
        'x_sample': nrm(ks[1], (DEC_BATCH, DEC_SEQ, D_MODEL), 1.0),
        'cache_k': nrm(ks[2], (DEC_BATCH, DEPTH, PAST_LEN, KV_HEADS, HEAD_DIM), 1.0),
        'cache_v': nrm(ks[3], (DEC_BATCH, DEPTH, PAST_LEN, KV_HEADS, HEAD_DIM), 1.0),
        'state_hgrn_fwd': nrm(ks[4], (DEC_BATCH, DEPTH, HG_HEADS, HG_DK, HG_DV), 0.5),
        'state_hgrn_bwd': nrm(ks[5], (DEC_BATCH, DEPTH, HG_HEADS, HG_DK, HG_DV), 0.5),
        'c': nrm(ks[6], (DEC_BATCH, D_MODEL), 1.0),
        'c_ctx': nrm(ks[7], (D_MODEL,), 1.0),
        'w_mod': nrm(ks[8], (DEPTH, D_MODEL, 6 * D_MODEL), 0.5 * D_MODEL ** -0.5),
        'b_mod': nrm(ks[9], (DEPTH, 6 * D_MODEL), 0.02),
        'w_in': nrm(ks[10], (DEPTH, D_MODEL, IN_DIM), D_MODEL ** -0.5),
        'attn_sink': nrm(ks[11], (DEPTH, ATT_HEADS), 0.5),
        'attn_norm_g': 1.0 + nrm(ks[12], (DEPTH, ATT_DIM), 0.02),
        'hg_lb_logits': nrm(ks[13], (2, DEPTH, HG_DIM), 0.5),
        'hg_norm_g': 1.0 + nrm(ks[14], (DEPTH, HG_DV), 0.02),
        'w_o': nrm(ks[15], (DEPTH, MIX_DIM, D_MODEL), MIX_DIM ** -0.5 * DEEPNORM_BETA),
        'ln_g': 1.0 + nrm(ks[16], (DEPTH, 2, D_MODEL), 0.02),
        'ln_b': nrm(ks[17], (DEPTH, 2, D_MODEL), 0.02),
        'w_up': nrm(ks[18], (DEPTH, D_MODEL, D_FF), D_MODEL ** -0.5),
        'w_down': nrm(ks[19], (DEPTH, D_FF, D_MODEL), D_FF ** -0.5 * DEEPNORM_BETA),
    }


def reference(x_prompt, x_sample, cache_k, cache_v, state_hgrn_fwd, state_hgrn_bwd, c, c_ctx,
              w_mod, b_mod, w_in, attn_sink, attn_norm_g, hg_lb_logits, hg_norm_g, w_o, ln_g, ln_b,
              w_up, w_down):
    lb_fwd = lower_bounds(hg_lb_logits[0]).astype(x_prompt.dtype)
    lb_bwd = lower_bounds(hg_lb_logits[1]).astype(x_prompt.dtype)
    Bp = x_prompt.shape[0]
    Ts = x_sample.shape[1]
    ang_r, ang_c = axial_angles(Ts)
    zero_state = jnp.zeros((Bp, HG_HEADS, HG_DK, HG_DV), x_prompt.dtype)
    xp, xs = x_prompt, x_sample
    new_k, new_v, new_sf, new_sb = [], [], [], []
    for l in range(DEPTH):
        mods_ctx = modulation(c_ctx, w_mod[l], b_mod[l])
        xp, k_c, v_c, s_f, s_b = hybrid_layer(
            xp, mods_ctx, w_in[l], attn_sink[l], attn_norm_g[l], lb_fwd[l], lb_bwd[l], hg_norm_g[l],
            w_o[l], ln_g[l], ln_b[l], w_up[l], w_down[l], zero_state, zero_state, context_attention)
        new_k.append(k_c)
        new_v.append(v_c)
        new_sf.append(s_f)
        new_sb.append(s_b)
        mods_lat = [m[:, None, :] for m in modulation(c, w_mod[l], b_mod[l])]
        attend_lat = functools.partial(latent_attention, ang_r=ang_r, ang_c=ang_c,
                                       k_ctx=cache_k[:, l], v_ctx=cache_v[:, l])
        xs, _, _, _, _ = hybrid_layer(
            xs, mods_lat, w_in[l], attn_sink[l], attn_norm_g[l], lb_fwd[l], lb_bwd[l], hg_norm_g[l],
            w_o[l], ln_g[l], ln_b[l], w_up[l], w_down[l], state_hgrn_fwd[:, l], state_hgrn_bwd[:, l], attend_lat)
    new_cache_k = jnp.stack(new_k, axis=1)
    new_cache_v = jnp.stack(new_v, axis=1)
    new_state_hgrn_fwd = jnp.stack(new_sf, axis=1)
    new_state_hgrn_bwd = jnp.stack(new_sb, axis=1)
    return (xp, xs, new_cache_k, new_cache_v, new_state_hgrn_fwd, new_state_hgrn_bwd)
```

```python
import functools

import jax
import jax.numpy as jnp
import numpy as np
from jax import lax
from jax.experimental import pallas as pl
from jax.experimental.pallas import tpu as pltpu

F32 = jnp.float32
BF16 = jnp.bfloat16

D_MODEL = 2048
DEPTH = 4
GRID_W = 64
HEAD_DIM = 128
ATT_HEADS = 8
KV_HEADS = 2
Q_PER_KV = ATT_HEADS // KV_HEADS
ATT_DIM = ATT_HEADS * HEAD_DIM
KV_DIM = KV_HEADS * HEAD_DIM
WINDOW = 128
BLOCK = 128
ATT_SCALE = HEAD_DIM ** -0.5
ROPE_THETA = 10000.0
ROPE_FREQS = HEAD_DIM // 4
HG_HEADS = 8
HG_DK = 128
HG_DV = 128
HG_DIM = HG_HEADS * HG_DK
HG_VDIM = HG_HEADS * HG_DV
HG_CHUNK = 32
MIX_DIM = ATT_DIM + HG_VDIM
IN_DIM = ATT_DIM + 2 * KV_DIM + 3 * HG_DIM + 2 * HG_VDIM
D_FF = 4 * D_MODEL
LN_EPS = 1e-5
RMS_EPS = 1e-6
DEEPNORM_ALPHA = (2 * DEPTH) ** 0.25
NEG_INF = -1e30

Q_OFF = 0
K_OFF = ATT_DIM
V_OFF = K_OFF + KV_DIM
HQ_OFF = V_OFF + KV_DIM
HFF_OFF = HQ_OFF + HG_DIM
HFB_OFF = HFF_OFF + HG_DIM
HI_OFF = HFB_OFF + HG_DIM
HGT_OFF = HI_OFF + HG_VDIM

V7X_VMEM_LIMIT_BYTES = 56 * 1024 * 1024

MOD_ROWS = 8
MOD_TN = 768
TOKEN_TILE = 1024
INPROJ_TN = 512
OPROJ_TM = 512
MLP_TM = 512
MLP_TF = 1024
HG_GROUP = 4
HG_GW = HG_GROUP * HG_DK

NT_DIMS = (((1,), (1,)), ((), ()))
TN_DIMS = (((0,), (0,)), ((), ()))


def _params(n_axes):
    return pltpu.CompilerParams(dimension_semantics=("arbitrary",) * n_axes,
                                vmem_limit_bytes=V7X_VMEM_LIMIT_BYTES)


def _sigmoid(x):
    return 1.0 / (1.0 + jnp.exp(-x))


def _layer_norm(y, g, b):
    mu = jnp.mean(y, -1, keepdims=True)
    yc = y - mu
    var = jnp.mean(yc * yc, -1, keepdims=True)
    return yc * lax.rsqrt(var + LN_EPS) * g + b


def _mod_kernel(c_ref, w_ref, b_ref, o_ref):
    c = c_ref[...]
    a = (c * _sigmoid(c)).astype(BF16)
    o_ref[0] = jnp.dot(a, w_ref[0].astype(BF16), preferred_element_type=F32) + b_ref[0]


def _modulation(c_rows, w_mod, b_mod):
    depth, d, n = w_mod.shape
    return pl.pallas_call(
        _mod_kernel,
        grid=(depth, n // MOD_TN),
        in_specs=[pl.BlockSpec((MOD_ROWS, d), lambda l, j: (0, 0)),
                  pl.BlockSpec((1, d, MOD_TN), lambda l, j: (l, 0, j)),
                  pl.BlockSpec((1, 1, MOD_TN), lambda l, j: (l, 0, j))],
        out_specs=pl.BlockSpec((1, MOD_ROWS, MOD_TN), lambda l, j: (l, 0, j)),
        out_shape=jax.ShapeDtypeStruct((depth, MOD_ROWS, n), F32),
        compiler_params=_params(2),
        name="modulation",
    )(c_rows, w_mod, b_mod.reshape(depth, 1, n))


def _mod_index(tile_rows, n_ctx_rows, lat_rows):
    n_ctx_tiles = n_ctx_rows // tile_rows
    per_lat = lat_rows // tile_rows

    def idx(i):
        return jnp.where(i < n_ctx_tiles, 0, 1 + (i - n_ctx_tiles) // per_lat)
    return idx


def _inproj_kernel(x_ref, mod_ref, w_ref, z_ref, h_ref):
    @pl.when(pl.program_id(1) == 0)
    def _():
        shift, scale = mod_ref[0:1, :], mod_ref[1:2, :]
        h_ref[...] = (x_ref[...] * (1.0 + scale) + shift).astype(BF16)

    z_ref[...] = jnp.dot(h_ref[...], w_ref[...], preferred_element_type=F32)


def _inproj(x, mods, w_in, layer, mod_idx):
    n_tok = x.shape[0]
    return pl.pallas_call(
        _inproj_kernel,
        grid=(n_tok // TOKEN_TILE, IN_DIM // INPROJ_TN),
        in_specs=[pl.BlockSpec((TOKEN_TILE, D_MODEL), lambda i, j: (i, 0)),
                  pl.BlockSpec((None, None, 6, D_MODEL), lambda i, j: (layer, mod_idx(i), 0, 0)),
                  pl.BlockSpec((None, D_MODEL, INPROJ_TN), lambda i, j: (layer, 0, j))],
        out_specs=pl.BlockSpec((TOKEN_TILE, INPROJ_TN), lambda i, j: (i, j)),
        out_shape=jax.ShapeDtypeStruct((n_tok, IN_DIM), F32),
        scratch_shapes=[pltpu.VMEM((TOKEN_TILE, D_MODEL), BF16)],
        compiler_params=_params(2),
        name="inproj",
    )(x, mods, w_in)


def _softmax_rows(parts, sink):
    m = sink
    for s in parts:
        m = jnp.maximum(m, jnp.max(s, -1, keepdims=True))
    es = [jnp.exp(s - m) for s in parts]
    den = jnp.exp(sink - m)
    for e in es:
        den = den + jnp.sum(e, -1, keepdims=True)
    inv = 1.0 / den
    return [(e * inv).astype(BF16) for e in es]


def _rms_norm_store(o_scr, ss, g_ref, o_ref):
    inv = lax.rsqrt(ss * (1.0 / ATT_DIM) + RMS_EPS)
    o_ref[...] = (o_scr[...] * inv * g_ref[...]).astype(o_ref.dtype)


def _ctx_attn_kernel(sink_ref, q_ref, k_ref, v_ref, g_ref, o_ref, o_scr):
    ss = jnp.zeros((q_ref.shape[0], 1), F32)
    for g in range(KV_HEADS):
        kv_cols = slice(g * HEAD_DIM, (g + 1) * HEAD_DIM)
        kg = k_ref[:, kv_cols].astype(BF16)
        vg = v_ref[:, kv_cols].astype(BF16)
        for r in range(Q_PER_KV):
            h = g * Q_PER_KV + r
            cols = slice(h * HEAD_DIM, (h + 1) * HEAD_DIM)
            qh = q_ref[:, cols].astype(BF16)
            s = lax.dot_general(qh, kg, NT_DIMS, preferred_element_type=F32) * ATT_SCALE
            (p,) = _softmax_rows([s], sink_ref[h])
            oh = jnp.dot(p, vg, preferred_element_type=F32)
            ss = ss + jnp.sum(oh * oh, -1, keepdims=True)
            o_scr[:, cols] = oh
    _rms_norm_store(o_scr, ss, g_ref, o_ref)


def _ctx_attention(z, sink, norm_g, n_batch, seq):
    n_tok = z.shape[0]
    kb, vb = K_OFF // KV_DIM, V_OFF // KV_DIM
    return pl.pallas_call(
        _ctx_attn_kernel,
        grid=(n_batch,),
        in_specs=[pl.BlockSpec(memory_space=pltpu.SMEM),
                  pl.BlockSpec((seq, ATT_DIM), lambda b: (b, 0)),
                  pl.BlockSpec((seq, KV_DIM), lambda b: (b, kb)),
                  pl.BlockSpec((seq, KV_DIM), lambda b: (b, vb)),
                  pl.BlockSpec((1, ATT_DIM), lambda b: (0, 0))],
        out_specs=pl.BlockSpec((seq, ATT_DIM), lambda b: (b, 0)),
        out_shape=jax.ShapeDtypeStruct((n_tok, ATT_DIM), BF16),
        scratch_shapes=[pltpu.VMEM((seq, ATT_DIM), F32)],
        compiler_params=_params(1),
        name="ctx_attention",
    )(sink, z, z, z, norm_g.reshape(1, ATT_DIM))


def _rope(x, cos, sin_lo, sin_hi):
    return (x * cos + pltpu.roll(x, HEAD_DIM - ROPE_FREQS, 1) * sin_lo
            + pltpu.roll(x, ROPE_FREQS, 1) * sin_hi)


def _lat_attn_kernel(sink_ref, q_ref, k_ref, v_ref, ck_ref, cv_ref, cq_ref, slq_ref, shq_ref,
                     ca_ref, sla_ref, sha_ref, g_ref, buf_ref, o_ref, kr_scr, v_scr, ckb_scr, cvb_scr,
                     o_scr):
    del buf_ref
    n = pl.program_id(1)
    seq = k_ref.shape[0]
    win = 3 * BLOCK

    @pl.when(n == 0)
    def _():
        for g in range(KV_HEADS):
            cols = slice(g * HEAD_DIM, (g + 1) * HEAD_DIM)
            kr_scr[:, cols] = _rope(k_ref[:, cols], ca_ref[...], sla_ref[...], sha_ref[...]).astype(BF16)
        v_scr[...] = v_ref[...].astype(BF16)
        ckb_scr[...] = ck_ref[...].astype(BF16)
        cvb_scr[...] = cv_ref[...].astype(BF16)

    k0 = pl.multiple_of(jnp.clip((n - 1) * BLOCK, 0, seq - win), BLOCK)
    qpos = n * BLOCK + lax.broadcasted_iota(jnp.int32, (BLOCK, win), 0)
    kpos = k0 + lax.broadcasted_iota(jnp.int32, (BLOCK, win), 1)
    valid = jnp.abs(kpos - qpos) <= WINDOW
    ss = jnp.zeros((BLOCK, 1), F32)
    for g in range(KV_HEADS):
        kv_cols = slice(g * HEAD_DIM, (g + 1) * HEAD_DIM)
        k_loc = kr_scr[pl.ds(k0, win), kv_cols]
        v_loc = v_scr[pl.ds(k0, win), kv_cols]
        k_ctx = ckb_scr[:, kv_cols]
        v_ctx = cvb_scr[:, kv_cols]
        for r in range(Q_PER_KV):
            h = g * Q_PER_KV + r
            cols = slice(h * HEAD_DIM, (h + 1) * HEAD_DIM)
            qh = _rope(q_ref[:, cols], cq_ref[...], slq_ref[...], shq_ref[...]).astype(BF16)
            s_loc = lax.dot_general(qh, k_loc, NT_DIMS, preferred_element_type=F32) * ATT_SCALE
            s_loc = jnp.where(valid, s_loc, NEG_INF)
            s_ctx = lax.dot_general(qh, k_ctx, NT_DIMS, preferred_element_type=F32) * ATT_SCALE
            p_loc, p_ctx = _softmax_rows([s_loc, s_ctx], sink_ref[h])
            oh = (jnp.dot(p_loc, v_loc, preferred_element_type=F32)
                  + jnp.dot(p_ctx, v_ctx, preferred_element_type=F32))
            ss = ss + jnp.sum(oh * oh, -1, keepdims=True)
            o_scr[:, cols] = oh
    _rms_norm_store(o_scr, ss, g_ref, o_ref)


def _lat_attention(z, att_buf, cache_k, cache_v, layer, sink, norm_g, rope_tabs, row0, n_batch, seq):
    n_tok = z.shape[0]
    kb, vb = K_OFF // KV_DIM, V_OFF // KV_DIM
    nq = seq // BLOCK
    qrow0, srow0 = row0 // BLOCK, row0 // seq
    past = cache_k.shape[2]
    cos, sin_lo, sin_hi = rope_tabs
    qtab = pl.BlockSpec((BLOCK, HEAD_DIM), lambda b, n: (n, 0))
    atab = pl.BlockSpec((seq, HEAD_DIM), lambda b, n: (0, 0))
    cache_spec = pl.BlockSpec((None, None, past, KV_DIM), lambda b, n: (b, layer, 0, 0))
    return pl.pallas_call(
        _lat_attn_kernel,
        grid=(n_batch, nq),
        in_specs=[pl.BlockSpec(memory_space=pltpu.SMEM),
                  pl.BlockSpec((BLOCK, ATT_DIM), lambda b, n: (qrow0 + b * nq + n, 0)),
                  pl.BlockSpec((seq, KV_DIM), lambda b, n: (srow0 + b, kb)),
                  pl.BlockSpec((seq, KV_DIM), lambda b, n: (srow0 + b, vb)),
                  cache_spec, cache_spec, qtab, qtab, qtab, atab, atab, atab,
                  pl.BlockSpec((1, ATT_DIM), lambda b, n: (0, 0)),
                  pl.BlockSpec(memory_space=pl.ANY)],
        out_specs=pl.BlockSpec((BLOCK, ATT_DIM), lambda b, n: (qrow0 + b * nq + n, 0)),
        out_shape=jax.ShapeDtypeStruct((n_tok, ATT_DIM), BF16),
        scratch_shapes=[pltpu.VMEM((seq, KV_DIM), BF16), pltpu.VMEM((seq, KV_DIM), BF16),
                        pltpu.VMEM((past, KV_DIM), BF16), pltpu.VMEM((past, KV_DIM), BF16),
                        pltpu.VMEM((BLOCK, ATT_DIM), F32)],
        input_output_aliases={13: 0},
        compiler_params=_params(2),
        name="lat_attention",
    )(sink, z, z, z, cache_k, cache_v, cos, sin_lo, sin_hi, cos, sin_lo, sin_hi,
      norm_g.reshape(1, ATT_DIM), att_buf)


def _rope_tables(seq):
    rows = seq // GRID_W
    row = jnp.repeat(jnp.arange(rows, dtype=F32), GRID_W)
    col = jnp.tile(jnp.arange(GRID_W, dtype=F32), rows)
    inv = ROPE_THETA ** (-jnp.arange(ROPE_FREQS, dtype=F32) / ROPE_FREQS)
    ang_r, ang_c = row[:, None] * inv, col[:, None] * inv
    cr, sr, cc, sc = jnp.cos(ang_r), jnp.sin(ang_r), jnp.cos(ang_c), jnp.sin(ang_c)
    zero = jnp.zeros_like(sr)
    cos = jnp.concatenate([cr, cr, cc, cc], -1)
    sin_lo = jnp.concatenate([-sr, zero, -sc, zero], -1)
    sin_hi = jnp.concatenate([zero, sr, zero, sc], -1)
    return cos, sin_lo, sin_hi


def _chunk_cumsum(x, rows, reverse):
    s = 1
    while s < HG_CHUNK:
        if reverse:
            x = x + jnp.where(rows < HG_CHUNK - s, pltpu.roll(x, HG_CHUNK - s, 0), 0.0)
        else:
            x = x + jnp.where(rows >= s, pltpu.roll(x, s, 0), 0.0)
        s *= 2
    return x


def _hgrn_kernel(*refs, seq, has_init, has_buf, want_state):
    hq_ref, hff_ref, hfb_ref, hi_ref, hgt_ref, lbf_ref, lbb_ref, ng_ref = refs[:8]
    pos = 8
    if has_init:
        sf0_ref, sb0_ref = refs[pos:pos + 2]
        pos += 2
    if has_buf:
        pos += 1
    o_ref = refs[pos]
    pos += 1
    if want_state:
        sf_ref, sb_ref = refs[pos:pos + 2]
        pos += 2
    q_scr, of_scr, ob_scr, st_scr = refs[pos:pos + 4]

    C = HG_CHUNK
    n_chunks = seq // C
    q_in = hq_ref[...]
    q_scr[...] = q_in * _sigmoid(q_in)
    for h in range(HG_GROUP):
        if has_init:
            st_scr[h] = sf0_ref[h].T
            st_scr[HG_GROUP + h] = sb0_ref[h].T
        else:
            st_scr[h] = jnp.zeros((HG_DV, HG_DK), F32)
            st_scr[HG_GROUP + h] = jnp.zeros((HG_DV, HG_DK), F32)

    rows = lax.broadcasted_iota(jnp.int32, (C, HG_DK), 0)
    ti = lax.broadcasted_iota(jnp.int32, (C, C), 0)
    si = lax.broadcasted_iota(jnp.int32, (C, C), 1)
    dirs = ((hff_ref, lbf_ref, of_scr, False, si <= ti, C // 2 - 1, C - 1),
            (hfb_ref, lbb_ref, ob_scr, True, si >= ti, C // 2, 0))

    def chunk_step(c, carry):
        starts = (pl.multiple_of(c * C, C), pl.multiple_of((n_chunks - 1 - c) * C, C))
        for h in range(HG_GROUP):
            cols = slice(h * HG_DK, (h + 1) * HG_DK)
            for d, (z_ref, lb_ref, out_scr, reverse, causal, mid, last) in enumerate(dirs):
                rs = pl.ds(starts[d], C)
                q = q_scr[rs, cols]
                v = hi_ref[rs, cols].astype(BF16)
                lb = lb_ref[:, cols]
                sig = _sigmoid(z_ref[rs, cols])
                log_f = jnp.log(lb + (1.0 - lb) * sig)
                k = (1.0 - lb) * (1.0 - sig)
                b = _chunk_cumsum(log_f, rows, reverse)
                b_mid = b[mid:mid + 1, :]
                b_last = b[last:last + 1, :]
                qe = (q * jnp.exp(b - b_mid)).astype(BF16)
                ke = (k * jnp.exp(b_mid - b)).astype(BF16)
                a = lax.dot_general(qe, ke, NT_DIMS, preferred_element_type=F32)
                a = jnp.where(causal, a, 0.0).astype(BF16)
                st = st_scr[d * HG_GROUP + h]
                o = (jnp.dot(a, v, preferred_element_type=F32)
                     + lax.dot_general((q * jnp.exp(b)).astype(BF16), st.astype(BF16), NT_DIMS,
                                       preferred_element_type=F32))
                out_scr[rs, cols] = o
                u_t = lax.dot_general(v, (k * jnp.exp(b_last - b)).astype(BF16), TN_DIMS,
                                      preferred_element_type=F32)
                st_scr[d * HG_GROUP + h] = st * jnp.exp(b_last) + u_t
        return carry

    lax.fori_loop(0, n_chunks, chunk_step, 0)

    for h in range(HG_GROUP):
        cols = slice(h * HG_DK, (h + 1) * HG_DK)
        o = of_scr[:, cols] + ob_scr[:, cols]
        o = o * lax.rsqrt(jnp.mean(o * o, -1, keepdims=True) + RMS_EPS) * ng_ref[...]
        gt = hgt_ref[:, cols]
        o_ref[:, cols] = (o * (gt * _sigmoid(gt))).astype(o_ref.dtype)
        if want_state:
            sf_ref[h] = st_scr[h].T
            sb_ref[h] = st_scr[HG_GROUP + h].T


def _hgrn(z, hg_buf, lb_f, lb_b, norm_g, row0, n_batch, seq, init_states=None, layer=None,
          want_state=False):
    n_tok = z.shape[0]
    n_groups = HG_HEADS // HG_GROUP
    r0 = row0 // seq
    has_init = init_states is not None

    def zspec(off):
        return pl.BlockSpec((seq, HG_GW), lambda b, g, o=off // HG_GW: (r0 + b, o + g))

    lbspec = pl.BlockSpec((1, HG_GW), lambda b, g: (0, g))
    in_specs = [zspec(HQ_OFF), zspec(HFF_OFF), zspec(HFB_OFF), zspec(HI_OFF), zspec(HGT_OFF),
                lbspec, lbspec, pl.BlockSpec((1, HG_DV), lambda b, g: (0, 0))]
    args = [z, z, z, z, z, lb_f.reshape(1, HG_DIM), lb_b.reshape(1, HG_DIM), norm_g.reshape(1, HG_DV)]
    if has_init:
        st_spec = pl.BlockSpec((None, None, HG_GROUP, HG_DK, HG_DV), lambda b, g: (b, layer, g, 0, 0))
        in_specs += [st_spec, st_spec]
        args += list(init_states)
    has_buf = hg_buf is not None
    if has_buf:
        in_specs.append(pl.BlockSpec(memory_space=pl.ANY))
        args.append(hg_buf)
    out_specs = [pl.BlockSpec((seq, HG_GW), lambda b, g: (r0 + b, g))]
    out_shape = [jax.ShapeDtypeStruct((n_tok, HG_VDIM), BF16)]
    if want_state:
        so = pl.BlockSpec((None, HG_GROUP, HG_DK, HG_DV), lambda b, g: (b, g, 0, 0))
        out_specs += [so, so]
        out_shape += [jax.ShapeDtypeStruct((n_batch, HG_HEADS, HG_DK, HG_DV), F32)] * 2
    return pl.pallas_call(
        functools.partial(_hgrn_kernel, seq=seq, has_init=has_init, has_buf=has_buf,
                          want_state=want_state),
        grid=(n_batch, n_groups),
        in_specs=in_specs,
        out_specs=out_specs,
        out_shape=out_shape,
        scratch_shapes=[pltpu.VMEM((seq, HG_GW), F32), pltpu.VMEM((seq, HG_GW), F32),
                        pltpu.VMEM((seq, HG_GW), F32),
                        pltpu.VMEM((2 * HG_GROUP, HG_DV, HG_DK), F32)],
        input_output_aliases={len(args) - 1: 0} if has_buf else {},
        compiler_params=_params(2),
        name=f"hgrn_t{seq}",
    )(*args)


def _oproj_kernel(att_ref, hg_ref, w_ref, x_ref, mod_ref, g_ref, b_ref, o_ref):
    mix = (jnp.dot(att_ref[...], w_ref[0:ATT_DIM, :], preferred_element_type=F32)
           + jnp.dot(hg_ref[...], w_ref[ATT_DIM:MIX_DIM, :], preferred_element_type=F32))
    y = DEEPNORM_ALPHA * x_ref[...] + mod_ref[2:3, :] * mix
    o_ref[...] = _layer_norm(y, g_ref[0:1, :], b_ref[0:1, :])


def _oproj(att, hg, w_o, x, mods, ln_g, ln_b, layer, mod_idx):
    n_tok = x.shape[0]
    row = lambda i: (i, 0)
    return pl.pallas_call(
        _oproj_kernel,
        grid=(n_tok // OPROJ_TM,),
        in_specs=[pl.BlockSpec((OPROJ_TM, ATT_DIM), row),
                  pl.BlockSpec((OPROJ_TM, HG_VDIM), row),
                  pl.BlockSpec((None, MIX_DIM, D_MODEL), lambda i: (layer, 0, 0)),
                  pl.BlockSpec((OPROJ_TM, D_MODEL), row),
                  pl.BlockSpec((None, None, 6, D_MODEL), lambda i: (layer, mod_idx(i), 0, 0)),
                  pl.BlockSpec((None, 2, D_MODEL), lambda i: (layer, 0, 0)),
                  pl.BlockSpec((None, 2, D_MODEL), lambda i: (layer, 0, 0))],
        out_specs=pl.BlockSpec((OPROJ_TM, D_MODEL), row),
        out_shape=jax.ShapeDtypeStruct((n_tok, D_MODEL), F32),
        compiler_params=_params(1),
        name="oproj_ln",
    )(att, hg, w_o, x, mods, ln_g, ln_b)


def _mlp_kernel(x_ref, mod_ref, wu_ref, wd_ref, g_ref, b_ref, o_ref, h_ref):
    f = pl.program_id(1)

    @pl.when(f == 0)
    def _():
        shift, scale = mod_ref[3:4, :], mod_ref[4:5, :]
        h_ref[...] = (x_ref[...] * (1.0 + scale) + shift).astype(BF16)

    u = jnp.dot(h_ref[...], wu_ref[...], preferred_element_type=F32)
    u = jnp.square(jnp.maximum(u, 0.0)).astype(BF16)
    part = jnp.dot(u, wd_ref[...], preferred_element_type=F32)

    @pl.when(f == 0)
    def _():
        o_ref[...] = part

    @pl.when(f > 0)
    def _():
        o_ref[...] += part

    @pl.when(f == pl.num_programs(1) - 1)
    def _():
        y = DEEPNORM_ALPHA * x_ref[...] + mod_ref[5:6, :] * o_ref[...]
        o_ref[...] = _layer_norm(y, g_ref[1:2, :], b_ref[1:2, :])


def _mlp(x, mods, w_up, w_down, ln_g, ln_b, layer, mod_idx):
    n_tok = x.shape[0]
    return pl.pallas_call(
        _mlp_kernel,
        grid=(n_tok // MLP_TM, D_FF // MLP_TF),
        in_specs=[pl.BlockSpec((MLP_TM, D_MODEL), lambda i, f: (i, 0)),
                  pl.BlockSpec((None, None, 6, D_MODEL), lambda i, f: (layer, mod_idx(i), 0, 0)),
                  pl.BlockSpec((None, D_MODEL, MLP_TF), lambda i, f: (layer, 0, f)),
                  pl.BlockSpec((None, MLP_TF, D_MODEL), lambda i, f: (layer, f, 0)),
                  pl.BlockSpec((None, 2, D_MODEL), lambda i, f: (layer, 0, 0)),
                  pl.BlockSpec((None, 2, D_MODEL), lambda i, f: (layer, 0, 0))],
        out_specs=pl.BlockSpec((MLP_TM, D_MODEL), lambda i, f: (i, 0)),
        out_shape=jax.ShapeDtypeStruct((n_tok, D_MODEL), F32),
        scratch_shapes=[pltpu.VMEM((MLP_TM, D_MODEL), BF16)],
        compiler_params=_params(2),
        name="mlp_ln",
    )(x, mods, w_up, w_down, ln_g, ln_b)


def _lower_bounds(lb_logits):
    p = jax.nn.softmax(lb_logits.astype(F32), axis=0)
    cs = jnp.cumsum(p, axis=0)
    return cs - cs[0:1]


def kernel(x_prompt, x_sample, cache_k, cache_v, state_hgrn_fwd, state_hgrn_bwd, c, c_ctx, w_mod, b_mod,
           w_in, attn_sink, attn_norm_g, hg_lb_logits, hg_norm_g, w_o, ln_g, ln_b, w_up, w_down):
    bp, seq_p, d = x_prompt.shape
    bs, seq_s, _ = x_sample.shape
    depth = w_in.shape[0]
    past = cache_k.shape[2]
    n_ctx, n_lat = bp * seq_p, bs * seq_s
    n_tok = n_ctx + n_lat

    x = jnp.concatenate([x_prompt.reshape(n_ctx, d), x_sample.reshape(n_lat, d)], 0)
    c_rows = jnp.concatenate([c_ctx[None, :], c, jnp.zeros((MOD_ROWS - 1 - bs, d), F32)], 0)
    mods = _modulation(c_rows, w_mod, b_mod)[:, :1 + bs].reshape(depth, 1 + bs, 6, d)

    w_in_b, w_o_b, w_up_b, w_down_b = (w.astype(BF16) for w in (w_in, w_o, w_up, w_down))
    lb_f, lb_b = _lower_bounds(hg_lb_logits[0]), _lower_bounds(hg_lb_logits[1])
    rope_tabs = _rope_tables(seq_s)
    ck = cache_k.reshape(bs, depth, past, KV_DIM)
    cv = cache_v.reshape(bs, depth, past, KV_DIM)

    new_k, new_v, new_sf, new_sb = [], [], [], []
    for l in range(depth):
        z = _inproj(x, mods, w_in_b, l, _mod_index(TOKEN_TILE, n_ctx, seq_s))
        new_k.append(z[:n_ctx, K_OFF:K_OFF + KV_DIM].reshape(bp, seq_p, KV_HEADS, HEAD_DIM))
        new_v.append(z[:n_ctx, V_OFF:V_OFF + KV_DIM].reshape(bp, seq_p, KV_HEADS, HEAD_DIM))

        att = _ctx_attention(z, attn_sink[l], attn_norm_g[l], bp, seq_p)
        att = _lat_attention(z, att, ck, cv, l, attn_sink[l], attn_norm_g[l], rope_tabs, n_ctx, bs, seq_s)

        hg, s_f, s_b = _hgrn(z, None, lb_f[l], lb_b[l], hg_norm_g[l], 0, bp, seq_p, want_state=True)
        (hg,) = _hgrn(z, hg, lb_f[l], lb_b[l], hg_norm_g[l], n_ctx, bs, seq_s,
                      init_states=(state_hgrn_fwd, state_hgrn_bwd), layer=l)
        new_sf.append(s_f)
        new_sb.append(s_b)

        x = _oproj(att, hg, w_o_b, x, mods, ln_g, ln_b, l, _mod_index(OPROJ_TM, n_ctx, seq_s))
        x = _mlp(x, mods, w_up_b, w_down_b, ln_g, ln_b, l, _mod_index(MLP_TM, n_ctx, seq_s))

    return (x[:n_ctx].reshape(bp, seq_p, d), x[n_ctx:].reshape(bs, seq_s, d),
            jnp.stack(new_k, 1), jnp.stack(new_v, 1), jnp.stack(new_sf, 1), jnp.stack(new_sb, 1))
```

```python
import functools

import jax
import jax.numpy as jnp
import numpy as np
from jax import lax
from jax.experimental import pallas as pl
from jax.experimental.pallas import tpu as pltpu

F32 = jnp.float32
BF16 = jnp.bfloat16

D_MODEL = 2048
DEPTH = 4
GRID_W = 64
HEAD_DIM = 128
ATT_HEADS = 8
KV_HEADS = 2
Q_PER_KV = ATT_HEADS // KV_HEADS
ATT_DIM = ATT_HEADS * HEAD_DIM
KV_DIM = KV_HEADS * HEAD_DIM
WINDOW = 128
BLOCK = 128
ATT_SCALE = HEAD_DIM ** -0.5
ROPE_THETA = 10000.0
ROPE_FREQS = HEAD_DIM // 4
HG_HEADS = 8
HG_DK = 128
HG_DV = 128
HG_DIM = HG_HEADS * HG_DK
HG_VDIM = HG_HEADS * HG_DV
HG_CHUNK = 32
MIX_DIM = ATT_DIM + HG_VDIM
IN_DIM = ATT_DIM + 2 * KV_DIM + 3 * HG_DIM + 2 * HG_VDIM
D_FF = 4 * D_MODEL
LN_EPS = 1e-5
RMS_EPS = 1e-6
DEEPNORM_ALPHA = (2 * DEPTH) ** 0.25
NEG_INF = -1e30

Q_OFF = 0
K_OFF = ATT_DIM
V_OFF = K_OFF + KV_DIM
HQ_OFF = V_OFF + KV_DIM
HFF_OFF = HQ_OFF + HG_DIM
HFB_OFF = HFF_OFF + HG_DIM
HI_OFF = HFB_OFF + HG_DIM
HGT_OFF = HI_OFF + HG_VDIM

V7X_VMEM_LIMIT_BYTES = 56 * 1024 * 1024

MOD_ROWS = 8
MOD_TN = 768
TOKEN_TILE = 1024
INPROJ_TN = 1664
OPROJ_TM = 512
MLP_TM = 512
MLP_TF = 1024
MLP_TN = 512
HG_GROUP = 4
HG_GW = HG_GROUP * HG_DK

NT_DIMS = (((1,), (1,)), ((), ()))
TN_DIMS = (((0,), (0,)), ((), ()))


def _params(n_axes):
    return pltpu.CompilerParams(dimension_semantics=("arbitrary",) * n_axes,
                                vmem_limit_bytes=V7X_VMEM_LIMIT_BYTES)


def _sigmoid(x):
    return 1.0 / (1.0 + jnp.exp(-x))


def _layer_norm(y, g, b):
    mu = jnp.mean(y, -1, keepdims=True)
    yc = y - mu
    var = jnp.mean(yc * yc, -1, keepdims=True)
    return yc * lax.rsqrt(var + LN_EPS) * g + b


def _mod_kernel(c_ref, w_ref, b_ref, o_ref):
    c = c_ref[...]
    a = (c * _sigmoid(c)).astype(BF16)
    o_ref[0] = jnp.dot(a, w_ref[0].astype(BF16), preferred_element_type=F32) + b_ref[0]


def _modulation(c_rows, w_mod, b_mod):
    depth, d, n = w_mod.shape
    return pl.pallas_call(
        _mod_kernel,
        grid=(depth, n // MOD_TN),
        in_specs=[pl.BlockSpec((MOD_ROWS, d), lambda l, j: (0, 0)),
                  pl.BlockSpec((1, d, MOD_TN), lambda l, j: (l, 0, j)),
                  pl.BlockSpec((1, 1, MOD_TN), lambda l, j: (l, 0, j))],
        out_specs=pl.BlockSpec((1, MOD_ROWS, MOD_TN), lambda l, j: (l, 0, j)),
        out_shape=jax.ShapeDtypeStruct((depth, MOD_ROWS, n), F32),
        compiler_params=_params(2),
        name="modulation",
    )(c_rows, w_mod, b_mod.reshape(depth, 1, n))


def _mod_index(tile_rows, n_ctx_rows, lat_rows):
    n_ctx_tiles = n_ctx_rows // tile_rows
    per_lat = lat_rows // tile_rows

    def idx(i):
        return jnp.where(i < n_ctx_tiles, 0, 1 + (i - n_ctx_tiles) // per_lat)
    return idx


def _inproj_kernel(x_ref, mod_ref, w_ref, z_ref, h_ref):
    @pl.when(pl.program_id(1) == 0)
    def _():
        shift, scale = mod_ref[0:1, :], mod_ref[1:2, :]
        h_ref[...] = (x_ref[...] * (1.0 + scale) + shift).astype(BF16)

    z_ref[...] = jnp.dot(h_ref[...], w_ref[...], preferred_element_type=F32)


def _col_tiles(w, tn):
    depth, k, n = w.shape
    return w.reshape(depth, k, n // tn, tn).transpose(0, 2, 1, 3)


def _inproj(x, mods, w_in_tiles, layer, mod_idx):
    n_tok = x.shape[0]
    return pl.pallas_call(
        _inproj_kernel,
        grid=(n_tok // TOKEN_TILE, IN_DIM // INPROJ_TN),
        in_specs=[pl.BlockSpec((TOKEN_TILE, D_MODEL), lambda i, j: (i, 0)),
                  pl.BlockSpec((None, None, 6, D_MODEL), lambda i, j: (layer, mod_idx(i), 0, 0)),
                  pl.BlockSpec((None, None, D_MODEL, INPROJ_TN), lambda i, j: (layer, j, 0, 0))],
        out_specs=pl.BlockSpec((TOKEN_TILE, INPROJ_TN), lambda i, j: (i, j)),
        out_shape=jax.ShapeDtypeStruct((n_tok, IN_DIM), F32),
        scratch_shapes=[pltpu.VMEM((TOKEN_TILE, D_MODEL), BF16)],
        compiler_params=_params(2),
        name="inproj",
    )(x, mods, w_in_tiles)


def _softmax_rows(parts, sink):
    m = sink
    for s in parts:
        m = jnp.maximum(m, jnp.max(s, -1, keepdims=True))
    es = [jnp.exp(s - m) for s in parts]
    den = jnp.exp(sink - m)
    for e in es:
        den = den + jnp.sum(e, -1, keepdims=True)
    inv = 1.0 / den
    return [(e * inv).astype(BF16) for e in es]


def _rms_norm_store(o_scr, ss, g_ref, o_ref):
    inv = lax.rsqrt(ss * (1.0 / ATT_DIM) + RMS_EPS)
    o_ref[...] = (o_scr[...] * inv * g_ref[...]).astype(o_ref.dtype)


def _ctx_attn_kernel(sink_ref, q_ref, k_ref, v_ref, g_ref, o_ref, o_scr):
    ss = jnp.zeros((q_ref.shape[0], 1), F32)
    for g in range(KV_HEADS):
        kv_cols = slice(g * HEAD_DIM, (g + 1) * HEAD_DIM)
        kg = k_ref[:, kv_cols].astype(BF16)
        vg = v_ref[:, kv_cols].astype(BF16)
        for r in range(Q_PER_KV):
            h = g * Q_PER_KV + r
            cols = slice(h * HEAD_DIM, (h + 1) * HEAD_DIM)
            qh = q_ref[:, cols].astype(BF16)
            s = lax.dot_general(qh, kg, NT_DIMS, preferred_element_type=F32) * ATT_SCALE
            (p,) = _softmax_rows([s], sink_ref[h])
            oh = jnp.dot(p, vg, preferred_element_type=F32)
            ss = ss + jnp.sum(oh * oh, -1, keepdims=True)
            o_scr[:, cols] = oh
    _rms_norm_store(o_scr, ss, g_ref, o_ref)


def _ctx_attention(z, sink, norm_g, n_batch, seq):
    n_tok = z.shape[0]
    kb, vb = K_OFF // KV_DIM, V_OFF // KV_DIM
    return pl.pallas_call(
        _ctx_attn_kernel,
        grid=(n_batch,),
        in_specs=[pl.BlockSpec(memory_space=pltpu.SMEM),
                  pl.BlockSpec((seq, ATT_DIM), lambda b: (b, 0)),
                  pl.BlockSpec((seq, KV_DIM), lambda b: (b, kb)),
                  pl.BlockSpec((seq, KV_DIM), lambda b: (b, vb)),
                  pl.BlockSpec((1, ATT_DIM), lambda b: (0, 0))],
        out_specs=pl.BlockSpec((seq, ATT_DIM), lambda b: (b, 0)),
        out_shape=jax.ShapeDtypeStruct((n_tok, ATT_DIM), BF16),
        scratch_shapes=[pltpu.VMEM((seq, ATT_DIM), F32)],
        compiler_params=_params(1),
        name="ctx_attention",
    )(sink, z, z, z, norm_g.reshape(1, ATT_DIM))


def _rope(x, cos, sin_lo, sin_hi):
    return (x * cos + pltpu.roll(x, HEAD_DIM - ROPE_FREQS, 1) * sin_lo
            + pltpu.roll(x, ROPE_FREQS, 1) * sin_hi)


def _lat_attn_kernel(sink_ref, q_ref, k_ref, v_ref, ck_ref, cv_ref, cq_ref, slq_ref, shq_ref,
                     ca_ref, sla_ref, sha_ref, g_ref, buf_ref, o_ref, kr_scr, v_scr, ckb_scr, cvb_scr,
                     o_scr):
    del buf_ref
    n = pl.program_id(1)
    seq = k_ref.shape[0]
    win = 3 * BLOCK

    @pl.when(n == 0)
    def _():
        for g in range(KV_HEADS):
            cols = slice(g * HEAD_DIM, (g + 1) * HEAD_DIM)
            kr_scr[:, cols] = _rope(k_ref[:, cols], ca_ref[...], sla_ref[...], sha_ref[...]).astype(BF16)
        v_scr[...] = v_ref[...].astype(BF16)
        ckb_scr[...] = ck_ref[...].astype(BF16)
        cvb_scr[...] = cv_ref[...].astype(BF16)

    k0 = pl.multiple_of(jnp.clip((n - 1) * BLOCK, 0, seq - win), BLOCK)
    qpos = n * BLOCK + lax.broadcasted_iota(jnp.int32, (BLOCK, win), 0)
    kpos = k0 + lax.broadcasted_iota(jnp.int32, (BLOCK, win), 1)
    valid = jnp.abs(kpos - qpos) <= WINDOW
    ss = jnp.zeros((BLOCK, 1), F32)
    for g in range(KV_HEADS):
        kv_cols = slice(g * HEAD_DIM, (g + 1) * HEAD_DIM)
        k_loc = kr_scr[pl.ds(k0, win), kv_cols]
        v_loc = v_scr[pl.ds(k0, win), kv_cols]
        k_ctx = ckb_scr[:, kv_cols]
        v_ctx = cvb_scr[:, kv_cols]
        for r in range(Q_PER_KV):
            h = g * Q_PER_KV + r
            cols = slice(h * HEAD_DIM, (h + 1) * HEAD_DIM)
            qh = _rope(q_ref[:, cols], cq_ref[...], slq_ref[...], shq_ref[...]).astype(BF16)
            s_loc = lax.dot_general(qh, k_loc, NT_DIMS, preferred_element_type=F32) * ATT_SCALE
            s_loc = jnp.where(valid, s_loc, NEG_INF)
            s_ctx = lax.dot_general(qh, k_ctx, NT_DIMS, preferred_element_type=F32) * ATT_SCALE
            p_loc, p_ctx = _softmax_rows([s_loc, s_ctx], sink_ref[h])
            oh = (jnp.dot(p_loc, v_loc, preferred_element_type=F32)
                  + jnp.dot(p_ctx, v_ctx, preferred_element_type=F32))
            ss = ss + jnp.sum(oh * oh, -1, keepdims=True)
            o_scr[:, cols] = oh
    _rms_norm_store(o_scr, ss, g_ref, o_ref)


def _lat_attention(z, att_buf, cache_k, cache_v, layer, sink, norm_g, rope_tabs, row0, n_batch, seq):
    n_tok = z.shape[0]
    kb, vb = K_OFF // KV_DIM, V_OFF // KV_DIM
    nq = seq // BLOCK
    qrow0, srow0 = row0 // BLOCK, row0 // seq
    past = cache_k.shape[2]
    cos, sin_lo, sin_hi = rope_tabs
    qtab = pl.BlockSpec((BLOCK, HEAD_DIM), lambda b, n: (n, 0))
    atab = pl.BlockSpec((seq, HEAD_DIM), lambda b, n: (0, 0))
    cache_spec = pl.BlockSpec((None, None, past, KV_DIM), lambda b, n: (b, layer, 0, 0))
    return pl.pallas_call(
        _lat_attn_kernel,
        grid=(n_batch, nq),
        in_specs=[pl.BlockSpec(memory_space=pltpu.SMEM),
                  pl.BlockSpec((BLOCK, ATT_DIM), lambda b, n: (qrow0 + b * nq + n, 0)),
                  pl.BlockSpec((seq, KV_DIM), lambda b, n: (srow0 + b, kb)),
                  pl.BlockSpec((seq, KV_DIM), lambda b, n: (srow0 + b, vb)),
                  cache_spec, cache_spec, qtab, qtab, qtab, atab, atab, atab,
                  pl.BlockSpec((1, ATT_DIM), lambda b, n: (0, 0)),
                  pl.BlockSpec(memory_space=pl.ANY)],
        out_specs=pl.BlockSpec((BLOCK, ATT_DIM), lambda b, n: (qrow0 + b * nq + n, 0)),
        out_shape=jax.ShapeDtypeStruct((n_tok, ATT_DIM), BF16),
        scratch_shapes=[pltpu.VMEM((seq, KV_DIM), BF16), pltpu.VMEM((seq, KV_DIM), BF16),
                        pltpu.VMEM((past, KV_DIM), BF16), pltpu.VMEM((past, KV_DIM), BF16),
                        pltpu.VMEM((BLOCK, ATT_DIM), F32)],
        input_output_aliases={13: 0},
        compiler_params=_params(2),
        name="lat_attention",
    )(sink, z, z, z, cache_k, cache_v, cos, sin_lo, sin_hi, cos, sin_lo, sin_hi,
      norm_g.reshape(1, ATT_DIM), att_buf)


def _rope_tables(seq):
    rows = seq // GRID_W
    row = jnp.repeat(jnp.arange(rows, dtype=F32), GRID_W)
    col = jnp.tile(jnp.arange(GRID_W, dtype=F32), rows)
    inv = ROPE_THETA ** (-jnp.arange(ROPE_FREQS, dtype=F32) / ROPE_FREQS)
    ang_r, ang_c = row[:, None] * inv, col[:, None] * inv
    cr, sr, cc, sc = jnp.cos(ang_r), jnp.sin(ang_r), jnp.cos(ang_c), jnp.sin(ang_c)
    zero = jnp.zeros_like(sr)
    cos = jnp.concatenate([cr, cr, cc, cc], -1)
    sin_lo = jnp.concatenate([-sr, zero, -sc, zero], -1)
    sin_hi = jnp.concatenate([zero, sr, zero, sc], -1)
    return cos, sin_lo, sin_hi


def _chunk_cumsum(x, rows, reverse):
    s = 1
    while s < HG_CHUNK:
        if reverse:
            x = x + jnp.where(rows < HG_CHUNK - s, pltpu.roll(x, HG_CHUNK - s, 0), 0.0)
        else:
            x = x + jnp.where(rows >= s, pltpu.roll(x, s, 0), 0.0)
        s *= 2
    return x


def _hgrn_kernel(*refs, seq, has_init, has_buf, want_state):
    hq_ref, hff_ref, hfb_ref, hi_ref, hgt_ref, lbf_ref, lbb_ref, ng_ref = refs[:8]
    pos = 8
    if has_init:
        sf0_ref, sb0_ref = refs[pos:pos + 2]
        pos += 2
    if has_buf:
        pos += 1
    o_ref = refs[pos]
    pos += 1
    if want_state:
        sf_ref, sb_ref = refs[pos:pos + 2]
        pos += 2
    q_scr, of_scr, ob_scr, st_scr = refs[pos:pos + 4]

    C = HG_CHUNK
    n_chunks = seq // C
    q_in = hq_ref[...]
    q_scr[...] = q_in * _sigmoid(q_in)
    for h in range(HG_GROUP):
        if has_init:
            st_scr[h] = sf0_ref[h].T
            st_scr[HG_GROUP + h] = sb0_ref[h].T
        else:
            st_scr[h] = jnp.zeros((HG_DV, HG_DK), F32)
            st_scr[HG_GROUP + h] = jnp.zeros((HG_DV, HG_DK), F32)

    rows = lax.broadcasted_iota(jnp.int32, (C, HG_DK), 0)
    ti = lax.broadcasted_iota(jnp.int32, (C, C), 0)
    si = lax.broadcasted_iota(jnp.int32, (C, C), 1)
    dirs = ((hff_ref, lbf_ref, of_scr, False, si <= ti, C // 2 - 1, C - 1),
            (hfb_ref, lbb_ref, ob_scr, True, si >= ti, C // 2, 0))

    def chunk_step(c, carry):
        starts = (pl.multiple_of(c * C, C), pl.multiple_of((n_chunks - 1 - c) * C, C))
        streams = [(h, d) for h in range(HG_GROUP) for d in range(2)]
        prep = []
        for h, d in streams:
            z_ref, lb_ref, _, reverse, _, mid, last = dirs[d]
            cols = slice(h * HG_DK, (h + 1) * HG_DK)
            rs = pl.ds(starts[d], C)
            q = q_scr[rs, cols]
            v = hi_ref[rs, cols].astype(BF16)
            lb = lb_ref[:, cols]
            sig = _sigmoid(z_ref[rs, cols])
            log_f = jnp.log(lb + (1.0 - lb) * sig)
            k = (1.0 - lb) * (1.0 - sig)
            b = _chunk_cumsum(log_f, rows, reverse)
            b_mid = b[mid:mid + 1, :]
            b_last = b[last:last + 1, :]
            prep.append(dict(
                v=v,
                qe=(q * jnp.exp(b - b_mid)).astype(BF16),
                ke=(k * jnp.exp(b_mid - b)).astype(BF16),
                qb=(q * jnp.exp(b)).astype(BF16),
                kl=(k * jnp.exp(b_last - b)).astype(BF16),
                decay=jnp.exp(b_last)))
        a_raw = [lax.dot_general(p["qe"], p["ke"], NT_DIMS, preferred_element_type=F32) for p in prep]
        states = [st_scr[d * HG_GROUP + h] for h, d in streams]
        o_inter = [lax.dot_general(p["qb"], st.astype(BF16), NT_DIMS, preferred_element_type=F32)
                   for p, st in zip(prep, states)]
        u_t = [lax.dot_general(p["v"], p["kl"], TN_DIMS, preferred_element_type=F32) for p in prep]
        o_intra = [jnp.dot(jnp.where(dirs[d][4], a, 0.0).astype(BF16), p["v"], preferred_element_type=F32)
                   for (h, d), a, p in zip(streams, a_raw, prep)]
        for i, (h, d) in enumerate(streams):
            cols = slice(h * HG_DK, (h + 1) * HG_DK)
            dirs[d][2][pl.ds(starts[d], C), cols] = o_intra[i] + o_inter[i]
            st_scr[d * HG_GROUP + h] = states[i] * prep[i]["decay"] + u_t[i]
        return carry

    lax.fori_loop(0, n_chunks, chunk_step, 0, unroll=4)

    for h in range(HG_GROUP):
        cols = slice(h * HG_DK, (h + 1) * HG_DK)
        o = of_scr[:, cols] + ob_scr[:, cols]
        o = o * lax.rsqrt(jnp.mean(o * o, -1, keepdims=True) + RMS_EPS) * ng_ref[...]
        gt = hgt_ref[:, cols]
        o_ref[:, cols] = (o * (gt * _sigmoid(gt))).astype(o_ref.dtype)
        if want_state:
            sf_ref[h] = st_scr[h].T
            sb_ref[h] = st_scr[HG_GROUP + h].T


def _hgrn(z, hg_buf, lb_f, lb_b, norm_g, row0, n_batch, seq, init_states=None, layer=None,
          want_state=False):
    n_tok = z.shape[0]
    n_groups = HG_HEADS // HG_GROUP
    r0 = row0 // seq
    has_init = init_states is not None

    def zspec(off):
        return pl.BlockSpec((seq, HG_GW), lambda b, g, o=off // HG_GW: (r0 + b, o + g))

    lbspec = pl.BlockSpec((1, HG_GW), lambda b, g: (0, g))
    in_specs = [zspec(HQ_OFF), zspec(HFF_OFF), zspec(HFB_OFF), zspec(HI_OFF), zspec(HGT_OFF),
                lbspec, lbspec, pl.BlockSpec((1, HG_DV), lambda b, g: (0, 0))]
    args = [z, z, z, z, z, lb_f.reshape(1, HG_DIM), lb_b.reshape(1, HG_DIM), norm_g.reshape(1, HG_DV)]
    if has_init:
        st_spec = pl.BlockSpec((None, None, HG_GROUP, HG_DK, HG_DV), lambda b, g: (b, layer, g, 0, 0))
        in_specs += [st_spec, st_spec]
        args += list(init_states)
    has_buf = hg_buf is not None
    if has_buf:
        in_specs.append(pl.BlockSpec(memory_space=pl.ANY))
        args.append(hg_buf)
    out_specs = [pl.BlockSpec((seq, HG_GW), lambda b, g: (r0 + b, g))]
    out_shape = [jax.ShapeDtypeStruct((n_tok, HG_VDIM), BF16)]
    if want_state:
        so = pl.BlockSpec((None, HG_GROUP, HG_DK, HG_DV), lambda b, g: (b, g, 0, 0))
        out_specs += [so, so]
        out_shape += [jax.ShapeDtypeStruct((n_batch, HG_HEADS, HG_DK, HG_DV), F32)] * 2
    return pl.pallas_call(
        functools.partial(_hgrn_kernel, seq=seq, has_init=has_init, has_buf=has_buf,
                          want_state=want_state),
        grid=(n_batch, n_groups),
        in_specs=in_specs,
        out_specs=out_specs,
        out_shape=out_shape,
        scratch_shapes=[pltpu.VMEM((seq, HG_GW), F32), pltpu.VMEM((seq, HG_GW), F32),
                        pltpu.VMEM((seq, HG_GW), F32),
                        pltpu.VMEM((2 * HG_GROUP, HG_DV, HG_DK), F32)],
        input_output_aliases={len(args) - 1: 0} if has_buf else {},
        compiler_params=_params(2),
        name=f"hgrn_t{seq}",
    )(*args)


def _oproj_kernel(att_ref, hg_ref, w_ref, x_ref, mod_ref, g_ref, b_ref, o_ref):
    mix = (jnp.dot(att_ref[...], w_ref[0:ATT_DIM, :], preferred_element_type=F32)
           + jnp.dot(hg_ref[...], w_ref[ATT_DIM:MIX_DIM, :], preferred_element_type=F32))
    y = DEEPNORM_ALPHA * x_ref[...] + mod_ref[2:3, :] * mix
    o_ref[...] = _layer_norm(y, g_ref[0:1, :], b_ref[0:1, :])


def _oproj(att, hg, w_o, x, mods, ln_g, ln_b, layer, mod_idx):
    n_tok = x.shape[0]
    row = lambda i: (i, 0)
    return pl.pallas_call(
        _oproj_kernel,
        grid=(n_tok // OPROJ_TM,),
        in_specs=[pl.BlockSpec((OPROJ_TM, ATT_DIM), row),
                  pl.BlockSpec((OPROJ_TM, HG_VDIM), row),
                  pl.BlockSpec((None, MIX_DIM, D_MODEL), lambda i: (layer, 0, 0)),
                  pl.BlockSpec((OPROJ_TM, D_MODEL), row),
                  pl.BlockSpec((None, None, 6, D_MODEL), lambda i: (layer, mod_idx(i), 0, 0)),
                  pl.BlockSpec((None, 2, D_MODEL), lambda i: (layer, 0, 0)),
                  pl.BlockSpec((None, 2, D_MODEL), lambda i: (layer, 0, 0))],
        out_specs=pl.BlockSpec((OPROJ_TM, D_MODEL), row),
        out_shape=jax.ShapeDtypeStruct((n_tok, D_MODEL), F32),
        compiler_params=_params(1),
        name="oproj_ln",
    )(att, hg, w_o, x, mods, ln_g, ln_b)


def _mlp_kernel(x_ref, mod_ref, wu_ref, wd_ref, g_ref, b_ref, o_ref, h_ref):
    f = pl.program_id(1)

    @pl.when(f == 0)
    def _():
        shift, scale = mod_ref[3:4, :], mod_ref[4:5, :]
        h_ref[...] = (x_ref[...] * (1.0 + scale) + shift).astype(BF16)
        o_ref[...] = jnp.zeros_like(o_ref)

    u = jnp.dot(h_ref[...], wu_ref[...], preferred_element_type=F32)
    u = jnp.square(jnp.maximum(u, 0.0)).astype(BF16)
    for n in range(0, D_MODEL, MLP_TN):
        o_ref[:, n:n + MLP_TN] += jnp.dot(u, wd_ref[:, n:n + MLP_TN], preferred_element_type=F32)

    @pl.when(f == pl.num_programs(1) - 1)
    def _():
        y = DEEPNORM_ALPHA * x_ref[...] + mod_ref[5:6, :] * o_ref[...]
        o_ref[...] = _layer_norm(y, g_ref[1:2, :], b_ref[1:2, :])


def _mlp(x, mods, w_up, w_down, ln_g, ln_b, layer, mod_idx):
    n_tok = x.shape[0]
    return pl.pallas_call(
        _mlp_kernel,
        grid=(n_tok // MLP_TM, D_FF // MLP_TF),
        in_specs=[pl.BlockSpec((MLP_TM, D_MODEL), lambda i, f: (i, 0)),
                  pl.BlockSpec((None, None, 6, D_MODEL), lambda i, f: (layer, mod_idx(i), 0, 0)),
                  pl.BlockSpec((None, None, D_MODEL, MLP_TF), lambda i, f: (layer, f, 0, 0)),
                  pl.BlockSpec((None, MLP_TF, D_MODEL), lambda i, f: (layer, f, 0)),
                  pl.BlockSpec((None, 2, D_MODEL), lambda i, f: (layer, 0, 0)),
                  pl.BlockSpec((None, 2, D_MODEL), lambda i, f: (layer, 0, 0))],
        out_specs=pl.BlockSpec((MLP_TM, D_MODEL), lambda i, f: (i, 0)),
        out_shape=jax.ShapeDtypeStruct((n_tok, D_MODEL), F32),
        scratch_shapes=[pltpu.VMEM((MLP_TM, D_MODEL), BF16)],
        compiler_params=_params(2),
        name="mlp_ln",
    )(x, mods, w_up, w_down, ln_g, ln_b)


def _lower_bounds(lb_logits):
    p = jax.nn.softmax(lb_logits.astype(F32), axis=0)
    cs = jnp.cumsum(p, axis=0)
    return cs - cs[0:1]


def kernel(x_prompt, x_sample, cache_k, cache_v, state_hgrn_fwd, state_hgrn_bwd, c, c_ctx, w_mod, b_mod,
           w_in, attn_sink, attn_norm_g, hg_lb_logits, hg_norm_g, w_o, ln_g, ln_b, w_up, w_down):
    bp, seq_p, d = x_prompt.shape
    bs, seq_s, _ = x_sample.shape
    depth = w_in.shape[0]
    past = cache_k.shape[2]
    n_ctx, n_lat = bp * seq_p, bs * seq_s
    n_tok = n_ctx + n_lat

    x = jnp.concatenate([x_prompt.reshape(n_ctx, d), x_sample.reshape(n_lat, d)], 0)
    c_rows = jnp.concatenate([c_ctx[None, :], c, jnp.zeros((MOD_ROWS - 1 - bs, d), F32)], 0)
    mods = _modulation(c_rows, w_mod, b_mod)[:, :1 + bs].reshape(depth, 1 + bs, 6, d)

    w_in_b = _col_tiles(w_in.astype(BF16), INPROJ_TN)
    w_up_b = _col_tiles(w_up.astype(BF16), MLP_TF)
    w_o_b, w_down_b = w_o.astype(BF16), w_down.astype(BF16)
    lb_f, lb_b = _lower_bounds(hg_lb_logits[0]), _lower_bounds(hg_lb_logits[1])
    rope_tabs = _rope_tables(seq_s)
    ck = cache_k.reshape(bs, depth, past, KV_DIM)
    cv = cache_v.reshape(bs, depth, past, KV_DIM)

    new_k, new_v, new_sf, new_sb = [], [], [], []
    for l in range(depth):
        z = _inproj(x, mods, w_in_b, l, _mod_index(TOKEN_TILE, n_ctx, seq_s))
        new_k.append(z[:n_ctx, K_OFF:K_OFF + KV_DIM].reshape(bp, seq_p, KV_HEADS, HEAD_DIM))
        new_v.append(z[:n_ctx, V_OFF:V_OFF + KV_DIM].reshape(bp, seq_p, KV_HEADS, HEAD_DIM))

        att = _ctx_attention(z, attn_sink[l], attn_norm_g[l], bp, seq_p)
        att = _lat_attention(z, att, ck, cv, l, attn_sink[l], attn_norm_g[l], rope_tabs, n_ctx, bs, seq_s)

        hg, s_f, s_b = _hgrn(z, None, lb_f[l], lb_b[l], hg_norm_g[l], 0, bp, seq_p, want_state=True)
        (hg,) = _hgrn(z, hg, lb_f[l], lb_b[l], hg_norm_g[l], n_ctx, bs, seq_s,
                      init_states=(state_hgrn_fwd, state_hgrn_bwd), layer=l)
        new_sf.append(s_f)
        new_sb.append(s_b)

        x = _oproj(att, hg, w_o_b, x, mods, ln_g, ln_b, l, _mod_index(OPROJ_TM, n_ctx, seq_s))
        x = _mlp(x, mods, w_up_b, w_down_b, ln_g, ln_b, l, _mod_index(MLP_TM, n_ctx, seq_s))

    return (x[:n_ctx].reshape(bp, seq_p, d), x[n_ctx:].reshape(bs, seq_s, d),
            jnp.stack(new_k, 1), jnp.stack(new_v, 1), jnp.stack(new_sf, 1), jnp.stack(new_sb, 1))
```

```python
import functools

import jax
import jax.numpy as jnp
from jax import lax
from jax.experimental import pallas as pl
from jax.experimental.pallas import tpu as pltpu

F32 = jnp.float32
BF16 = jnp.bfloat16

D_MODEL = 2048
DEPTH = 4
GRID_W = 64
HEAD_DIM = 128
ATT_HEADS = 8
KV_HEADS = 2
Q_PER_KV = ATT_HEADS // KV_HEADS
ATT_DIM = ATT_HEADS * HEAD_DIM
KV_DIM = KV_HEADS * HEAD_DIM
WINDOW = 128
BLOCK = 128
ATT_SCALE = HEAD_DIM ** -0.5
ROPE_THETA = 10000.0
ROPE_FREQS = HEAD_DIM // 4
HG_HEADS = 8
HG_DK = 128
HG_DV = 128
HG_DIM = HG_HEADS * HG_DK
HG_VDIM = HG_HEADS * HG_DV
HG_CHUNK = 32
MIX_DIM = ATT_DIM + HG_VDIM
IN_DIM = ATT_DIM + 2 * KV_DIM + 3 * HG_DIM + 2 * HG_VDIM
D_FF = 4 * D_MODEL
LN_EPS = 1e-5
RMS_EPS = 1e-6
DEEPNORM_ALPHA = (2 * DEPTH) ** 0.25
NEG_INF = -1e30

Q_OFF = 0
K_OFF = ATT_DIM
V_OFF = K_OFF + KV_DIM
HQ_OFF = V_OFF + KV_DIM
HFF_OFF = HQ_OFF + HG_DIM
HFB_OFF = HFF_OFF + HG_DIM
HI_OFF = HFB_OFF + HG_DIM
HGT_OFF = HI_OFF + HG_VDIM

V7X_VMEM_LIMIT_BYTES = 56 * 1024 * 1024

MOD_ROWS = 8
MOD_TN = 768
TOKEN_TILE = 1024
INPROJ_TN = 1664
OPROJ_TM = 512
MLP_TM = 1024
MLP_TF = 512
MLP_TN = 512
HG_GROUP = 4
HG_GW = HG_GROUP * HG_DK

NT_DIMS = (((1,), (1,)), ((), ()))
TN_DIMS = (((0,), (0,)), ((), ()))


def _params(n_axes):
    return pltpu.CompilerParams(dimension_semantics=("arbitrary",) * n_axes,
                                vmem_limit_bytes=V7X_VMEM_LIMIT_BYTES)


def _sigmoid(x):
    return 1.0 / (1.0 + jnp.exp(-x))


def _layer_norm(y, g, b):
    mu = jnp.mean(y, -1, keepdims=True)
    yc = y - mu
    var = jnp.mean(yc * yc, -1, keepdims=True)
    return yc * lax.rsqrt(var + LN_EPS) * g + b


def _mod_kernel(c_ref, w_ref, b_ref, o_ref):
    c = c_ref[...]
    a = (c * _sigmoid(c)).astype(BF16)
    o_ref[0] = jnp.dot(a, w_ref[0].astype(BF16), preferred_element_type=F32) + b_ref[0]


def _modulation(c_rows, w_mod, b_mod):
    depth, d, n = w_mod.shape
    return pl.pallas_call(
        _mod_kernel,
        grid=(depth, n // MOD_TN),
        in_specs=[pl.BlockSpec((MOD_ROWS, d), lambda l, j: (0, 0)),
                  pl.BlockSpec((1, d, MOD_TN), lambda l, j: (l, 0, j)),
                  pl.BlockSpec((1, 1, MOD_TN), lambda l, j: (l, 0, j))],
        out_specs=pl.BlockSpec((1, MOD_ROWS, MOD_TN), lambda l, j: (l, 0, j)),
        out_shape=jax.ShapeDtypeStruct((depth, MOD_ROWS, n), F32),
        compiler_params=_params(2),
        name="modulation",
    )(c_rows, w_mod, b_mod.reshape(depth, 1, n))


def _mod_index(tile_rows, n_ctx_rows, lat_rows):
    n_ctx_tiles = n_ctx_rows // tile_rows
    per_lat = lat_rows // tile_rows

    def idx(i):
        return jnp.where(i < n_ctx_tiles, 0, 1 + (i - n_ctx_tiles) // per_lat)
    return idx


def _inproj_kernel(x_ref, mod_ref, w_ref, z_ref, h_ref):
    @pl.when(pl.program_id(1) == 0)
    def _():
        shift, scale = mod_ref[0:1, :], mod_ref[1:2, :]
        h_ref[...] = (x_ref[...] * (1.0 + scale) + shift).astype(BF16)

    z_ref[...] = jnp.dot(h_ref[...], w_ref[...], preferred_element_type=F32)


def _inproj(x, mods, w_in, layer, mod_idx):
    n_tok = x.shape[0]
    return pl.pallas_call(
        _inproj_kernel,
        grid=(n_tok // TOKEN_TILE, IN_DIM // INPROJ_TN),
        in_specs=[pl.BlockSpec((TOKEN_TILE, D_MODEL), lambda i, j: (i, 0)),
                  pl.BlockSpec((None, None, 6, D_MODEL), lambda i, j: (layer, mod_idx(i), 0, 0)),
                  pl.BlockSpec((None, D_MODEL, INPROJ_TN), lambda i, j: (layer, 0, j))],
        out_specs=pl.BlockSpec((TOKEN_TILE, INPROJ_TN), lambda i, j: (i, j)),
        out_shape=jax.ShapeDtypeStruct((n_tok, IN_DIM), F32),
        scratch_shapes=[pltpu.VMEM((TOKEN_TILE, D_MODEL), BF16)],
        compiler_params=_params(2),
        name="inproj",
    )(x, mods, w_in)


def _softmax_rows(parts, sink):
    m = sink
    for s in parts:
        m = jnp.maximum(m, jnp.max(s, -1, keepdims=True))
    es = [jnp.exp(s - m) for s in parts]
    den = jnp.exp(sink - m)
    for e in es:
        den = den + jnp.sum(e, -1, keepdims=True)
    inv = 1.0 / den
    return [(e * inv).astype(BF16) for e in es]


def _rms_norm_store(o_scr, ss, g_ref, o_ref):
    inv = lax.rsqrt(ss * (1.0 / ATT_DIM) + RMS_EPS)
    o_ref[...] = (o_scr[...] * inv * g_ref[...]).astype(o_ref.dtype)


def _ctx_attn_kernel(sink_ref, q_ref, k_ref, v_ref, g_ref, o_ref, o_scr):
    ss = jnp.zeros((q_ref.shape[0], 1), F32)
    for g in range(KV_HEADS):
        kv_cols = slice(g * HEAD_DIM, (g + 1) * HEAD_DIM)
        kg = k_ref[:, kv_cols].astype(BF16)
        vg = v_ref[:, kv_cols].astype(BF16)
        for r in range(Q_PER_KV):
            h = g * Q_PER_KV + r
            cols = slice(h * HEAD_DIM, (h + 1) * HEAD_DIM)
            qh = q_ref[:, cols].astype(BF16)
            s = lax.dot_general(qh, kg, NT_DIMS, preferred_element_type=F32) * ATT_SCALE
            (p,) = _softmax_rows([s], sink_ref[h])
            oh = jnp.dot(p, vg, preferred_element_type=F32)
            ss = ss + jnp.sum(oh * oh, -1, keepdims=True)
            o_scr[:, cols] = oh
    _rms_norm_store(o_scr, ss, g_ref, o_ref)


def _ctx_attention(z, sink, norm_g, n_batch, seq):
    kb, vb = K_OFF // KV_DIM, V_OFF // KV_DIM
    return pl.pallas_call(
        _ctx_attn_kernel,
        grid=(n_batch,),
        in_specs=[pl.BlockSpec(memory_space=pltpu.SMEM),
                  pl.BlockSpec((seq, ATT_DIM), lambda b: (b, 0)),
                  pl.BlockSpec((seq, KV_DIM), lambda b: (b, kb)),
                  pl.BlockSpec((seq, KV_DIM), lambda b: (b, vb)),
                  pl.BlockSpec((1, ATT_DIM), lambda b: (0, 0))],
        out_specs=pl.BlockSpec((seq, ATT_DIM), lambda b: (b, 0)),
        out_shape=jax.ShapeDtypeStruct((n_batch * seq, ATT_DIM), BF16),
        scratch_shapes=[pltpu.VMEM((seq, ATT_DIM), F32)],
        compiler_params=_params(1),
        name="ctx_attention",
    )(sink, z, z, z, norm_g.reshape(1, ATT_DIM))


def _rope(x, cos, sin_lo, sin_hi):
    return (x * cos + pltpu.roll(x, HEAD_DIM - ROPE_FREQS, 1) * sin_lo
            + pltpu.roll(x, ROPE_FREQS, 1) * sin_hi)


def _lat_attn_kernel(sink_ref, q_ref, k_ref, v_ref, ck_ref, cv_ref, cq_ref, slq_ref, shq_ref,
                     ca_ref, sla_ref, sha_ref, g_ref, o_ref, kr_scr, v_scr, ckb_scr, cvb_scr, o_scr):
    n = pl.program_id(1)
    seq = k_ref.shape[0]
    win = 3 * BLOCK

    @pl.when(n == 0)
    def _():
        for g in range(KV_HEADS):
            cols = slice(g * HEAD_DIM, (g + 1) * HEAD_DIM)
            kr_scr[:, cols] = _rope(k_ref[:, cols], ca_ref[...], sla_ref[...], sha_ref[...]).astype(BF16)
        v_scr[...] = v_ref[...].astype(BF16)
        ckb_scr[...] = ck_ref[...].astype(BF16)
        cvb_scr[...] = cv_ref[...].astype(BF16)

    k0 = pl.multiple_of(jnp.clip((n - 1) * BLOCK, 0, seq - win), BLOCK)
    qpos = n * BLOCK + lax.broadcasted_iota(jnp.int32, (BLOCK, win), 0)
    kpos = k0 + lax.broadcasted_iota(jnp.int32, (BLOCK, win), 1)
    valid = jnp.abs(kpos - qpos) <= WINDOW
    ss = jnp.zeros((BLOCK, 1), F32)
    for g in range(KV_HEADS):
        kv_cols = slice(g * HEAD_DIM, (g + 1) * HEAD_DIM)
        k_loc = kr_scr[pl.ds(k0, win), kv_cols]
        v_loc = v_scr[pl.ds(k0, win), kv_cols]
        k_ctx = ckb_scr[:, kv_cols]
        v_ctx = cvb_scr[:, kv_cols]
        for r in range(Q_PER_KV):
            h = g * Q_PER_KV + r
            cols = slice(h * HEAD_DIM, (h + 1) * HEAD_DIM)
            qh = _rope(q_ref[:, cols], cq_ref[...], slq_ref[...], shq_ref[...]).astype(BF16)
            s_loc = lax.dot_general(qh, k_loc, NT_DIMS, preferred_element_type=F32) * ATT_SCALE
            s_loc = jnp.where(valid, s_loc, NEG_INF)
            s_ctx = lax.dot_general(qh, k_ctx, NT_DIMS, preferred_element_type=F32) * ATT_SCALE
            p_loc, p_ctx = _softmax_rows([s_loc, s_ctx], sink_ref[h])
            oh = (jnp.dot(p_loc, v_loc, preferred_element_type=F32)
                  + jnp.dot(p_ctx, v_ctx, preferred_element_type=F32))
            ss = ss + jnp.sum(oh * oh, -1, keepdims=True)
            o_scr[:, cols] = oh
    _rms_norm_store(o_scr, ss, g_ref, o_ref)


def _lat_attention(z, cache_k, cache_v, layer, sink, norm_g, rope_tabs, row0, n_batch, seq):
    kb, vb = K_OFF // KV_DIM, V_OFF // KV_DIM
    nq = seq // BLOCK
    qrow0, srow0 = row0 // BLOCK, row0 // seq
    past = cache_k.shape[2]
    cos, sin_lo, sin_hi = rope_tabs
    qtab = pl.BlockSpec((BLOCK, HEAD_DIM), lambda b, n: (n, 0))
    atab = pl.BlockSpec((seq, HEAD_DIM), lambda b, n: (0, 0))
    cache_spec = pl.BlockSpec((None, None, past, KV_DIM), lambda b, n: (b, layer, 0, 0))
    return pl.pallas_call(
        _lat_attn_kernel,
        grid=(n_batch, nq),
        in_specs=[pl.BlockSpec(memory_space=pltpu.SMEM),
                  pl.BlockSpec((BLOCK, ATT_DIM), lambda b, n: (qrow0 + b * nq + n, 0)),
                  pl.BlockSpec((seq, KV_DIM), lambda b, n: (srow0 + b, kb)),
                  pl.BlockSpec((seq, KV_DIM), lambda b, n: (srow0 + b, vb)),
                  cache_spec, cache_spec, qtab, qtab, qtab, atab, atab, atab,
                  pl.BlockSpec((1, ATT_DIM), lambda b, n: (0, 0))],
        out_specs=pl.BlockSpec((BLOCK, ATT_DIM), lambda b, n: (b * nq + n, 0)),
        out_shape=jax.ShapeDtypeStruct((n_batch * seq, ATT_DIM), BF16),
        scratch_shapes=[pltpu.VMEM((seq, KV_DIM), BF16), pltpu.VMEM((seq, KV_DIM), BF16),
                        pltpu.VMEM((past, KV_DIM), BF16), pltpu.VMEM((past, KV_DIM), BF16),
                        pltpu.VMEM((BLOCK, ATT_DIM), F32)],
        compiler_params=_params(2),
        name="lat_attention",
    )(sink, z, z, z, cache_k, cache_v, cos, sin_lo, sin_hi, cos, sin_lo, sin_hi,
      norm_g.reshape(1, ATT_DIM))


def _rope_tables(seq):
    rows = seq // GRID_W
    row = jnp.repeat(jnp.arange(rows, dtype=F32), GRID_W)
    col = jnp.tile(jnp.arange(GRID_W, dtype=F32), rows)
    inv = ROPE_THETA ** (-jnp.arange(ROPE_FREQS, dtype=F32) / ROPE_FREQS)
    ang_r, ang_c = row[:, None] * inv, col[:, None] * inv
    cr, sr, cc, sc = jnp.cos(ang_r), jnp.sin(ang_r), jnp.cos(ang_c), jnp.sin(ang_c)
    zero = jnp.zeros_like(sr)
    cos = jnp.concatenate([cr, cr, cc, cc], -1)
    sin_lo = jnp.concatenate([-sr, zero, -sc, zero], -1)
    sin_hi = jnp.concatenate([zero, sr, zero, sc], -1)
    return cos, sin_lo, sin_hi


def _chunk_cumsum(x, rows, reverse):
    s = 1
    while s < HG_CHUNK:
        if reverse:
            x = x + jnp.where(rows < HG_CHUNK - s, pltpu.roll(x, HG_CHUNK - s, 0), 0.0)
        else:
            x = x + jnp.where(rows >= s, pltpu.roll(x, s, 0), 0.0)
        s *= 2
    return x


def _hgrn_kernel(*refs, seq, has_init, want_state):
    hq_ref, hff_ref, hfb_ref, hi_ref, hgt_ref, lbf_ref, lbb_ref, ng_ref = refs[:8]
    pos = 8
    if has_init:
        sf0_ref, sb0_ref = refs[pos:pos + 2]
        pos += 2
    o_ref = refs[pos]
    pos += 1
    if want_state:
        sf_ref, sb_ref = refs[pos:pos + 2]
        pos += 2
    q_scr, of_scr, ob_scr, st_scr = refs[pos:pos + 4]

    C = HG_CHUNK
    n_chunks = seq // C
    q_in = hq_ref[...]
    q_scr[...] = q_in * _sigmoid(q_in)
    for h in range(HG_GROUP):
        if has_init:
            st_scr[h] = sf0_ref[h].T
            st_scr[HG_GROUP + h] = sb0_ref[h].T
        else:
            st_scr[h] = jnp.zeros((HG_DV, HG_DK), F32)
            st_scr[HG_GROUP + h] = jnp.zeros((HG_DV, HG_DK), F32)

    rows = lax.broadcasted_iota(jnp.int32, (C, HG_DK), 0)
    ti = lax.broadcasted_iota(jnp.int32, (C, C), 0)
    si = lax.broadcasted_iota(jnp.int32, (C, C), 1)
    dirs = ((hff_ref, lbf_ref, of_scr, False, si <= ti, C // 2 - 1, C - 1),
            (hfb_ref, lbb_ref, ob_scr, True, si >= ti, C // 2, 0))

    def chunk_step(c, carry):
        starts = (pl.multiple_of(c * C, C), pl.multiple_of((n_chunks - 1 - c) * C, C))
        streams = [(h, d) for h in range(HG_GROUP) for d in range(2)]
        prep = []
        for h, d in streams:
            z_ref, lb_ref, _, reverse, _, mid, last = dirs[d]
            cols = slice(h * HG_DK, (h + 1) * HG_DK)
            rs = pl.ds(starts[d], C)
            q = q_scr[rs, cols]
            v = hi_ref[rs, cols].astype(BF16)
            lb = lb_ref[:, cols]
            sig = _sigmoid(z_ref[rs, cols])
            log_f = jnp.log(lb + (1.0 - lb) * sig)
            k = (1.0 - lb) * (1.0 - sig)
            b = _chunk_cumsum(log_f, rows, reverse)
            b_mid = b[mid:mid + 1, :]
            b_last = b[last:last + 1, :]
            prep.append(dict(
                v=v,
                qe=(q * jnp.exp(b - b_mid)).astype(BF16),
                ke=(k * jnp.exp(b_mid - b)).astype(BF16),
                qb=(q * jnp.exp(b)).astype(BF16),
                kl=(k * jnp.exp(b_last - b)).astype(BF16),
                decay=jnp.exp(b_last)))
        a_raw = [lax.dot_general(p["qe"], p["ke"], NT_DIMS, preferred_element_type=F32) for p in prep]
        states = [st_scr[d * HG_GROUP + h] for h, d in streams]
        o_inter = [lax.dot_general(p["qb"], st.astype(BF16), NT_DIMS, preferred_element_type=F32)
                   for p, st in zip(prep, states)]
        u_t = [lax.dot_general(p["v"], p["kl"], TN_DIMS, preferred_element_type=F32) for p in prep]
        o_intra = [jnp.dot(jnp.where(dirs[d][4], a, 0.0).astype(BF16), p["v"], preferred_element_type=F32)
                   for (h, d), a, p in zip(streams, a_raw, prep)]
        for i, (h, d) in enumerate(streams):
            cols = slice(h * HG_DK, (h + 1) * HG_DK)
            dirs[d][2][pl.ds(starts[d], C), cols] = o_intra[i] + o_inter[i]
            st_scr[d * HG_GROUP + h] = states[i] * prep[i]["decay"] + u_t[i]
        return carry

    lax.fori_loop(0, n_chunks, chunk_step, 0, unroll=4)

    for h in range(HG_GROUP):
        cols = slice(h * HG_DK, (h + 1) * HG_DK)
        o = of_scr[:, cols] + ob_scr[:, cols]
        o = o * lax.rsqrt(jnp.mean(o * o, -1, keepdims=True) + RMS_EPS) * ng_ref[...]
        gt = hgt_ref[:, cols]
        o_ref[:, cols] = (o * (gt * _sigmoid(gt))).astype(o_ref.dtype)
        if want_state:
            sf_ref[h] = st_scr[h].T
            sb_ref[h] = st_scr[HG_GROUP + h].T


def _hgrn(z, lb_f, lb_b, norm_g, row0, n_batch, seq, init_states=None, layer=None, want_state=False):
    n_groups = HG_HEADS // HG_GROUP
    r0 = row0 // seq
    has_init = init_states is not None

    def zspec(off):
        return pl.BlockSpec((seq, HG_GW), lambda b, g, o=off // HG_GW: (r0 + b, o + g))

    lbspec = pl.BlockSpec((1, HG_GW), lambda b, g: (0, g))
    in_specs = [zspec(HQ_OFF), zspec(HFF_OFF), zspec(HFB_OFF), zspec(HI_OFF), zspec(HGT_OFF),
                lbspec, lbspec, pl.BlockSpec((1, HG_DV), lambda b, g: (0, 0))]
    args = [z, z, z, z, z, lb_f.reshape(1, HG_DIM), lb_b.reshape(1, HG_DIM), norm_g.reshape(1, HG_DV)]
    if has_init:
        st_spec = pl.BlockSpec((None, None, HG_GROUP, HG_DK, HG_DV), lambda b, g: (b, layer, g, 0, 0))
        in_specs += [st_spec, st_spec]
        args += list(init_states)
    out_specs = [pl.BlockSpec((seq, HG_GW), lambda b, g: (b, g))]
    out_shape = [jax.ShapeDtypeStruct((n_batch * seq, HG_VDIM), BF16)]
    if want_state:
        so = pl.BlockSpec((None, HG_GROUP, HG_DK, HG_DV), lambda b, g: (b, g, 0, 0))
        out_specs += [so, so]
        out_shape += [jax.ShapeDtypeStruct((n_batch, HG_HEADS, HG_DK, HG_DV), F32)] * 2
    return pl.pallas_call(
        functools.partial(_hgrn_kernel, seq=seq, has_init=has_init, want_state=want_state),
        grid=(n_batch, n_groups),
        in_specs=in_specs,
        out_specs=out_specs,
        out_shape=out_shape,
        scratch_shapes=[pltpu.VMEM((seq, HG_GW), F32), pltpu.VMEM((seq, HG_GW), F32),
                        pltpu.VMEM((seq, HG_GW), F32),
                        pltpu.VMEM((2 * HG_GROUP, HG_DV, HG_DK), F32)],
        compiler_params=_params(2),
        name=f"hgrn_t{seq}",
    )(*args)


def _oproj_kernel(attc_ref, attl_ref, hgc_ref, hgl_ref, w_ref, x_ref, mod_ref, g_ref, b_ref, o_ref, *,
                  n_ctx_tiles):
    is_ctx = pl.program_id(0) < n_ctx_tiles
    att = jnp.where(is_ctx, attc_ref[...], attl_ref[...])
    hg = jnp.where(is_ctx, hgc_ref[...], hgl_ref[...])
    mix = (jnp.dot(att, w_ref[0:ATT_DIM, :], preferred_element_type=F32)
           + jnp.dot(hg, w_ref[ATT_DIM:MIX_DIM, :], preferred_element_type=F32))
    y = DEEPNORM_ALPHA * x_ref[...] + mod_ref[2:3, :] * mix
    o_ref[...] = _layer_norm(y, g_ref[0:1, :], b_ref[0:1, :])


def _oproj(att_c, att_l, hg_c, hg_l, w_o, x, mods, ln_g, ln_b, layer, mod_idx):
    n_tok = x.shape[0]
    n_ctx_tiles = att_c.shape[0] // OPROJ_TM
    row = lambda i: (i, 0)
    ctx_row = lambda i: (jnp.minimum(i, n_ctx_tiles - 1), 0)
    lat_row = lambda i: (jnp.maximum(i - n_ctx_tiles, 0), 0)
    return pl.pallas_call(
        functools.partial(_oproj_kernel, n_ctx_tiles=n_ctx_tiles),
        grid=(n_tok // OPROJ_TM,),
        in_specs=[pl.BlockSpec((OPROJ_TM, ATT_DIM), ctx_row),
                  pl.BlockSpec((OPROJ_TM, ATT_DIM), lat_row),
                  pl.BlockSpec((OPROJ_TM, HG_VDIM), ctx_row),
                  pl.BlockSpec((OPROJ_TM, HG_VDIM), lat_row),
                  pl.BlockSpec((None, MIX_DIM, D_MODEL), lambda i: (layer, 0, 0)),
                  pl.BlockSpec((OPROJ_TM, D_MODEL), row),
                  pl.BlockSpec((None, None, 6, D_MODEL), lambda i: (layer, mod_idx(i), 0, 0)),
                  pl.BlockSpec((None, 2, D_MODEL), lambda i: (layer, 0, 0)),
                  pl.BlockSpec((None, 2, D_MODEL), lambda i: (layer, 0, 0))],
        out_specs=pl.BlockSpec((OPROJ_TM, D_MODEL), row),
        out_shape=jax.ShapeDtypeStruct((n_tok, D_MODEL), F32),
        compiler_params=_params(1),
        name="oproj_ln",
    )(att_c, att_l, hg_c, hg_l, w_o, x, mods, ln_g, ln_b)


def _mlp_kernel(x_ref, mod_ref, wu_ref, wd_ref, g_ref, b_ref, o_ref, h_ref):
    f = pl.program_id(1)

    @pl.when(f == 0)
    def _():
        shift, scale = mod_ref[3:4, :], mod_ref[4:5, :]
        h_ref[...] = (x_ref[...] * (1.0 + scale) + shift).astype(BF16)
        o_ref[...] = jnp.zeros_like(o_ref)

    u = jnp.dot(h_ref[...], wu_ref[...], preferred_element_type=F32)
    u = jnp.square(jnp.maximum(u, 0.0)).astype(BF16)
    for n in range(0, D_MODEL, MLP_TN):
        o_ref[:, n:n + MLP_TN] += jnp.dot(u, wd_ref[:, n:n + MLP_TN], preferred_element_type=F32)

    @pl.when(f == pl.num_programs(1) - 1)
    def _():
        y = DEEPNORM_ALPHA * x_ref[...] + mod_ref[5:6, :] * o_ref[...]
        o_ref[...] = _layer_norm(y, g_ref[1:2, :], b_ref[1:2, :])


def _mlp(x, mods, w_up, w_down, ln_g, ln_b, layer, mod_idx):
    n_tok = x.shape[0]
    return pl.pallas_call(
        _mlp_kernel,
        grid=(n_tok // MLP_TM, D_FF // MLP_TF),
        in_specs=[pl.BlockSpec((MLP_TM, D_MODEL), lambda i, f: (i, 0)),
                  pl.BlockSpec((None, None, 6, D_MODEL), lambda i, f: (layer, mod_idx(i), 0, 0)),
                  pl.BlockSpec((None, D_MODEL, MLP_TF), lambda i, f: (layer, 0, f)),
                  pl.BlockSpec((None, MLP_TF, D_MODEL), lambda i, f: (layer, f, 0)),
                  pl.BlockSpec((None, 2, D_MODEL), lambda i, f: (layer, 0, 0)),
                  pl.BlockSpec((None, 2, D_MODEL), lambda i, f: (layer, 0, 0))],
        out_specs=pl.BlockSpec((MLP_TM, D_MODEL), lambda i, f: (i, 0)),
        out_shape=jax.ShapeDtypeStruct((n_tok, D_MODEL), F32),
        scratch_shapes=[pltpu.VMEM((MLP_TM, D_MODEL), BF16)],
        compiler_params=_params(2),
        name="mlp_ln",
    )(x, mods, w_up, w_down, ln_g, ln_b)


def _lower_bounds(lb_logits):
    p = jax.nn.softmax(lb_logits.astype(F32), axis=0)
    cs = jnp.cumsum(p, axis=0)
    return cs - cs[0:1]


def kernel(x_prompt, x_sample, cache_k, cache_v, state_hgrn_fwd, state_hgrn_bwd, c, c_ctx, w_mod, b_mod,
           w_in, attn_sink, attn_norm_g, hg_lb_logits, hg_norm_g, w_o, ln_g, ln_b, w_up, w_down):
    bp, seq_p, d = x_prompt.shape
    bs, seq_s, _ = x_sample.shape
    depth = w_in.shape[0]
    past = cache_k.shape[2]
    n_ctx, n_lat = bp * seq_p, bs * seq_s

    x = jnp.concatenate([x_prompt.reshape(n_ctx, d), x_sample.reshape(n_lat, d)], 0)
    c_rows = jnp.concatenate([c_ctx[None, :], c, jnp.zeros((MOD_ROWS - 1 - bs, d), F32)], 0)
    mods = _modulation(c_rows, w_mod, b_mod)[:, :1 + bs].reshape(depth, 1 + bs, 6, d)

    w_in_b, w_o_b, w_up_b, w_down_b = (w.astype(BF16) for w in (w_in, w_o, w_up, w_down))
    lb_f, lb_b = _lower_bounds(hg_lb_logits[0]), _lower_bounds(hg_lb_logits[1])
    rope_tabs = _rope_tables(seq_s)
    ck = cache_k.reshape(bs, depth, past, KV_DIM)
    cv = cache_v.reshape(bs, depth, past, KV_DIM)

    new_k, new_v, new_sf, new_sb = [], [], [], []
    for l in range(depth):
        z = _inproj(x, mods, w_in_b, l, _mod_index(TOKEN_TILE, n_ctx, seq_s))
        new_k.append(z[:n_ctx, K_OFF:K_OFF + KV_DIM].reshape(bp, seq_p, KV_HEADS, HEAD_DIM))
        new_v.append(z[:n_ctx, V_OFF:V_OFF + KV_DIM].reshape(bp, seq_p, KV_HEADS, HEAD_DIM))

        att_c = _ctx_attention(z, attn_sink[l], attn_norm_g[l], bp, seq_p)
        att_l = _lat_attention(z, ck, cv, l, attn_sink[l], attn_norm_g[l], rope_tabs, n_ctx, bs, seq_s)

        hg_c, s_f, s_b = _hgrn(z, lb_f[l], lb_b[l], hg_norm_g[l], 0, bp, seq_p, want_state=True)
        (hg_l,) = _hgrn(z, lb_f[l], lb_b[l], hg_norm_g[l], n_ctx, bs, seq_s,
                        init_states=(state_hgrn_fwd, state_hgrn_bwd), layer=l)
        new_sf.append(s_f)
        new_sb.append(s_b)

        x = _oproj(att_c, att_l, hg_c, hg_l, w_o_b, x, mods, ln_g, ln_b, l,
                   _mod_index(OPROJ_TM, n_ctx, seq_s))
        x = _mlp(x, mods, w_up_b, w_down_b, ln_g, ln_b, l, _mod_index(MLP_TM, n_ctx, seq_s))

    return (x[:n_ctx].reshape(bp, seq_p, d), x[n_ctx:].reshape(bs, seq_s, d),
            jnp.stack(new_k, 1), jnp.stack(new_v, 1), jnp.stack(new_sf, 1), jnp.stack(new_sb, 1))
```

```python
import functools

import jax
import jax.numpy as jnp
from jax import lax
from jax.experimental import pallas as pl
from jax.experimental.pallas import tpu as pltpu

F32 = jnp.float32
BF16 = jnp.bfloat16

D_MODEL = 2048
DEPTH = 4
GRID_W = 64
HEAD_DIM = 128
ATT_HEADS = 8
KV_HEADS = 2
Q_PER_KV = ATT_HEADS // KV_HEADS
ATT_DIM = ATT_HEADS * HEAD_DIM
KV_DIM = KV_HEADS * HEAD_DIM
WINDOW = 128
BLOCK = 128
ATT_SCALE = HEAD_DIM ** -0.5
ROPE_THETA = 10000.0
ROPE_FREQS = HEAD_DIM // 4
HG_HEADS = 8
HG_DK = 128
HG_DV = 128
HG_DIM = HG_HEADS * HG_DK
HG_VDIM = HG_HEADS * HG_DV
HG_CHUNK = 32
MIX_DIM = ATT_DIM + HG_VDIM
IN_DIM = ATT_DIM + 2 * KV_DIM + 3 * HG_DIM + 2 * HG_VDIM
D_FF = 4 * D_MODEL
LN_EPS = 1e-5
RMS_EPS = 1e-6
DEEPNORM_ALPHA = (2 * DEPTH) ** 0.25
NEG_INF = -1e30
LOG2E = 1.4426950408889634

Q_OFF = 0
K_OFF = ATT_DIM
V_OFF = K_OFF + KV_DIM
HQ_OFF = V_OFF + KV_DIM
HFF_OFF = HQ_OFF + HG_DIM
HFB_OFF = HFF_OFF + HG_DIM
HI_OFF = HFB_OFF + HG_DIM
HGT_OFF = HI_OFF + HG_VDIM

V7X_VMEM_LIMIT_BYTES = 56 * 1024 * 1024

MOD_ROWS = 8
MOD_TN = 768
TOKEN_TILE = 1024
INPROJ_TN = 1664
OPROJ_TM = 512
MLP_TM = 1024
MLP_TF = 512
MLP_TN = 512
HG_GROUP = 4
HG_GW = HG_GROUP * HG_DK

NT_DIMS = (((1,), (1,)), ((), ()))
TN_DIMS = (((0,), (0,)), ((), ()))


def _params(n_axes):
    return pltpu.CompilerParams(dimension_semantics=("arbitrary",) * n_axes,
                                vmem_limit_bytes=V7X_VMEM_LIMIT_BYTES)


def _sigmoid(x):
    return 1.0 / (1.0 + jnp.exp(-x))


def _layer_norm(y, g, b):
    mu = jnp.mean(y, -1, keepdims=True)
    yc = y - mu
    var = jnp.mean(yc * yc, -1, keepdims=True)
    return yc * lax.rsqrt(var + LN_EPS) * g + b


def _mod_kernel(c_ref, w_ref, b_ref, o_ref):
    c = c_ref[...]
    a = (c * _sigmoid(c)).astype(BF16)
    o_ref[0] = jnp.dot(a, w_ref[0].astype(BF16), preferred_element_type=F32) + b_ref[0]


def _modulation(c_rows, w_mod, b_mod):
    depth, d, n = w_mod.shape
    return pl.pallas_call(
        _mod_kernel,
        grid=(depth, n // MOD_TN),
        in_specs=[pl.BlockSpec((MOD_ROWS, d), lambda l, j: (0, 0)),
                  pl.BlockSpec((1, d, MOD_TN), lambda l, j: (l, 0, j)),
                  pl.BlockSpec((1, 1, MOD_TN), lambda l, j: (l, 0, j))],
        out_specs=pl.BlockSpec((1, MOD_ROWS, MOD_TN), lambda l, j: (l, 0, j)),
        out_shape=jax.ShapeDtypeStruct((depth, MOD_ROWS, n), F32),
        compiler_params=_params(2),
        name="modulation",
    )(c_rows, w_mod, b_mod.reshape(depth, 1, n))


def _mod_index(tile_rows, n_ctx_rows, lat_rows):
    n_ctx_tiles = n_ctx_rows // tile_rows
    per_lat = lat_rows // tile_rows

    def idx(i):
        return jnp.where(i < n_ctx_tiles, 0, 1 + (i - n_ctx_tiles) // per_lat)
    return idx


def _inproj_kernel(x_ref, mod_ref, w_ref, z_ref, k_ref, v_ref, h_ref, *, n_ctx_tiles):
    i, j = pl.program_id(0), pl.program_id(1)

    @pl.when(j == 0)
    def _():
        shift, scale = mod_ref[0:1, :], mod_ref[1:2, :]
        h_ref[...] = (x_ref[...] * (1.0 + scale) + shift).astype(BF16)

    z_ref[...] = jnp.dot(h_ref[...], w_ref[...], preferred_element_type=F32)

    @pl.when(jnp.logical_and(j == 0, i < n_ctx_tiles))
    def _():
        k_ref[...] = z_ref[:, K_OFF:K_OFF + KV_DIM]
        v_ref[...] = z_ref[:, V_OFF:V_OFF + KV_DIM]


def _inproj(x, mods, w_in, layer, mod_idx, n_ctx):
    assert V_OFF + KV_DIM <= INPROJ_TN
    n_tok = x.shape[0]
    n_ctx_tiles = n_ctx // TOKEN_TILE
    kv_spec = pl.BlockSpec((TOKEN_TILE, KV_DIM), lambda i, j: (jnp.minimum(i, n_ctx_tiles - 1), 0))
    return pl.pallas_call(
        functools.partial(_inproj_kernel, n_ctx_tiles=n_ctx_tiles),
        grid=(n_tok // TOKEN_TILE, IN_DIM // INPROJ_TN),
        in_specs=[pl.BlockSpec((TOKEN_TILE, D_MODEL), lambda i, j: (i, 0)),
                  pl.BlockSpec((None, None, 6, D_MODEL), lambda i, j: (layer, mod_idx(i), 0, 0)),
                  pl.BlockSpec((None, D_MODEL, INPROJ_TN), lambda i, j: (layer, 0, j))],
        out_specs=[pl.BlockSpec((TOKEN_TILE, INPROJ_TN), lambda i, j: (i, j)), kv_spec, kv_spec],
        out_shape=[jax.ShapeDtypeStruct((n_tok, IN_DIM), F32),
                   jax.ShapeDtypeStruct((n_ctx, KV_DIM), F32),
                   jax.ShapeDtypeStruct((n_ctx, KV_DIM), F32)],
        scratch_shapes=[pltpu.VMEM((TOKEN_TILE, D_MODEL), BF16)],
        compiler_params=_params(2),
        name="inproj",
    )(x, mods, w_in)


def _with_ones(v):
    return jnp.concatenate([v, jnp.ones_like(v)], axis=1)


def _sink_attend(scores, values, sink2):
    c = ATT_SCALE * LOG2E
    m_raw = jnp.max(scores[0], -1, keepdims=True)
    for s in scores[1:]:
        m_raw = jnp.maximum(m_raw, jnp.max(s, -1, keepdims=True))
    m2 = jnp.maximum(m_raw * c, sink2)
    acc = None
    for s, v1 in zip(scores, values):
        e = jnp.exp2(s * c - m2).astype(BF16)
        part = jnp.dot(e, v1, preferred_element_type=F32)
        acc = part if acc is None else acc + part
    den = acc[:, HEAD_DIM:HEAD_DIM + 1] + jnp.exp2(sink2 - m2)
    return acc[:, :HEAD_DIM] * (1.0 / den)


def _rms_norm_store(o_scr, sq, g_ref, o_ref):
    ss = jnp.sum(sq, -1, keepdims=True)
    inv = lax.rsqrt(ss * (1.0 / ATT_DIM) + RMS_EPS)
    o_ref[...] = (o_scr[...] * inv * g_ref[...]).astype(o_ref.dtype)


def _head_cols(g, r):
    h = g * Q_PER_KV + r
    return slice(h * HEAD_DIM, (h + 1) * HEAD_DIM)


def _group_sinks(sink_ref, g, rows):
    return jnp.concatenate([jnp.full((rows, 1), sink_ref[g * Q_PER_KV + r] * LOG2E, F32)
                            for r in range(Q_PER_KV)], axis=0)


def _scatter_heads(og, g, rows, sq, o_scr):
    for r in range(Q_PER_KV):
        oh = og[r * rows:(r + 1) * rows]
        sq = sq + oh * oh
        o_scr[:, _head_cols(g, r)] = oh
    return sq


def _ctx_attn_kernel(sink_ref, q_ref, k_ref, v_ref, g_ref, o_ref, o_scr):
    c = ATT_SCALE * LOG2E
    sq = jnp.zeros((q_ref.shape[0], HEAD_DIM), F32)
    for g in range(KV_HEADS):
        kv_cols = slice(g * HEAD_DIM, (g + 1) * HEAD_DIM)
        kg = k_ref[:, kv_cols].astype(BF16)
        vg = v_ref[:, kv_cols].astype(BF16)
        for r in range(Q_PER_KV):
            cols = _head_cols(g, r)
            sink2 = sink_ref[g * Q_PER_KV + r] * LOG2E
            s = lax.dot_general(q_ref[:, cols].astype(BF16), kg, NT_DIMS, preferred_element_type=F32)
            m2 = jnp.maximum(jnp.max(s, -1, keepdims=True) * c, sink2)
            e = jnp.exp2(s * c - m2)
            den = jnp.sum(e, -1, keepdims=True) + jnp.exp2(sink2 - m2)
            p = (e * (1.0 / den)).astype(BF16)
            oh = jnp.dot(p, vg, preferred_element_type=F32)
            sq = sq + oh * oh
            o_scr[:, cols] = oh
    _rms_norm_store(o_scr, sq, g_ref, o_ref)


def _ctx_attention(z, sink, norm_g, n_batch, seq):
    kb, vb = K_OFF // KV_DIM, V_OFF // KV_DIM
    return pl.pallas_call(
        _ctx_attn_kernel,
        grid=(n_batch,),
        in_specs=[pl.BlockSpec(memory_space=pltpu.SMEM),
                  pl.BlockSpec((seq, ATT_DIM), lambda b: (b, 0)),
                  pl.BlockSpec((seq, KV_DIM), lambda b: (b, kb)),
                  pl.BlockSpec((seq, KV_DIM), lambda b: (b, vb)),
                  pl.BlockSpec((1, ATT_DIM), lambda b: (0, 0))],
        out_specs=pl.BlockSpec((seq, ATT_DIM), lambda b: (b, 0)),
        out_shape=jax.ShapeDtypeStruct((n_batch * seq, ATT_DIM), BF16),
        scratch_shapes=[pltpu.VMEM((seq, ATT_DIM), F32)],
        compiler_params=_params(1),
        name="ctx_attention",
    )(sink, z, z, z, norm_g.reshape(1, ATT_DIM))


def _rope(x, cos, sin_lo, sin_hi):
    return (x * cos + pltpu.roll(x, HEAD_DIM - ROPE_FREQS, 1) * sin_lo
            + pltpu.roll(x, ROPE_FREQS, 1) * sin_hi)


def _lat_attn_kernel(sink_ref, q_ref, k_ref, v_ref, ck_ref, cv_ref, cq_ref, slq_ref, shq_ref,
                     ca_ref, sla_ref, sha_ref, g_ref, o_ref, kr_scr, v_scr, ckb_scr, cvb_scr, o_scr):
    n = pl.program_id(1)
    seq = k_ref.shape[0]
    win = 3 * BLOCK

    @pl.when(n == 0)
    def _():
        for g in range(KV_HEADS):
            cols = slice(g * HEAD_DIM, (g + 1) * HEAD_DIM)
            kr_scr[:, cols] = _rope(k_ref[:, cols], ca_ref[...], sla_ref[...], sha_ref[...]).astype(BF16)
            v_scr[g] = _with_ones(v_ref[:, cols].astype(BF16))
            cvb_scr[g] = _with_ones(cv_ref[:, cols].astype(BF16))
        ckb_scr[...] = ck_ref[...].astype(BF16)

    k0 = pl.multiple_of(jnp.clip((n - 1) * BLOCK, 0, seq - win), BLOCK)
    rows = Q_PER_KV * BLOCK
    qpos = n * BLOCK + lax.broadcasted_iota(jnp.int32, (rows, win), 0) % BLOCK
    kpos = k0 + lax.broadcasted_iota(jnp.int32, (rows, win), 1)
    valid = jnp.abs(kpos - qpos) <= WINDOW
    sq = jnp.zeros((BLOCK, HEAD_DIM), F32)
    for g in range(KV_HEADS):
        kv_cols = slice(g * HEAD_DIM, (g + 1) * HEAD_DIM)
        k_loc = kr_scr[pl.ds(k0, win), kv_cols]
        v_loc = v_scr[g, pl.ds(k0, win), :]
        k_ctx = ckb_scr[:, kv_cols]
        v_ctx = cvb_scr[g]
        qg = jnp.concatenate(
            [_rope(q_ref[:, _head_cols(g, r)], cq_ref[...], slq_ref[...], shq_ref[...]).astype(BF16)
             for r in range(Q_PER_KV)], axis=0)
        s_loc = lax.dot_general(qg, k_loc, NT_DIMS, preferred_element_type=F32)
        s_loc = jnp.where(valid, s_loc, NEG_INF / ATT_SCALE)
        s_ctx = lax.dot_general(qg, k_ctx, NT_DIMS, preferred_element_type=F32)
        og = _sink_attend([s_loc, s_ctx], [v_loc, v_ctx], _group_sinks(sink_ref, g, BLOCK))
        sq = _scatter_heads(og, g, BLOCK, sq, o_scr)
    _rms_norm_store(o_scr, sq, g_ref, o_ref)


def _lat_attention(z, cache_k, cache_v, layer, sink, norm_g, rope_tabs, row0, n_batch, seq):
    kb, vb = K_OFF // KV_DIM, V_OFF // KV_DIM
    nq = seq // BLOCK
    qrow0, srow0 = row0 // BLOCK, row0 // seq
    past = cache_k.shape[2]
    cos, sin_lo, sin_hi = rope_tabs
    qtab = pl.BlockSpec((BLOCK, HEAD_DIM), lambda b, n: (n, 0))
    atab = pl.BlockSpec((seq, HEAD_DIM), lambda b, n: (0, 0))
    cache_spec = pl.BlockSpec((None, None, past, KV_DIM), lambda b, n: (b, layer, 0, 0))
    return pl.pallas_call(
        _lat_attn_kernel,
        grid=(n_batch, nq),
        in_specs=[pl.BlockSpec(memory_space=pltpu.SMEM),
                  pl.BlockSpec((BLOCK, ATT_DIM), lambda b, n: (qrow0 + b * nq + n, 0)),
                  pl.BlockSpec((seq, KV_DIM), lambda b, n: (srow0 + b, kb)),
                  pl.BlockSpec((seq, KV_DIM), lambda b, n: (srow0 + b, vb)),
                  cache_spec, cache_spec, qtab, qtab, qtab, atab, atab, atab,
                  pl.BlockSpec((1, ATT_DIM), lambda b, n: (0, 0))],
        out_specs=pl.BlockSpec((BLOCK, ATT_DIM), lambda b, n: (b * nq + n, 0)),
        out_shape=jax.ShapeDtypeStruct((n_batch * seq, ATT_DIM), BF16),
        scratch_shapes=[pltpu.VMEM((seq, KV_DIM), BF16), pltpu.VMEM((KV_HEADS, seq, 2 * HEAD_DIM), BF16),
                        pltpu.VMEM((past, KV_DIM), BF16), pltpu.VMEM((KV_HEADS, past, 2 * HEAD_DIM), BF16),
                        pltpu.VMEM((BLOCK, ATT_DIM), F32)],
        compiler_params=_params(2),
        name="lat_attention",
    )(sink, z, z, z, cache_k, cache_v, cos, sin_lo, sin_hi, cos, sin_lo, sin_hi,
      norm_g.reshape(1, ATT_DIM))


def _rope_tables(seq):
    rows = seq // GRID_W
    row = jnp.repeat(jnp.arange(rows, dtype=F32), GRID_W)
    col = jnp.tile(jnp.arange(GRID_W, dtype=F32), rows)
    inv = ROPE_THETA ** (-jnp.arange(ROPE_FREQS, dtype=F32) / ROPE_FREQS)
    ang_r, ang_c = row[:, None] * inv, col[:, None] * inv
    cr, sr, cc, sc = jnp.cos(ang_r), jnp.sin(ang_r), jnp.cos(ang_c), jnp.sin(ang_c)
    zero = jnp.zeros_like(sr)
    cos = jnp.concatenate([cr, cr, cc, cc], -1)
    sin_lo = jnp.concatenate([-sr, zero, -sc, zero], -1)
    sin_hi = jnp.concatenate([zero, sr, zero, sc], -1)
    return cos, sin_lo, sin_hi


def _chunk_cumsum(x, rows, reverse):
    s = 1
    while s < HG_CHUNK:
        if reverse:
            x = x + jnp.where(rows < HG_CHUNK - s, pltpu.roll(x, HG_CHUNK - s, 0), 0.0)
        else:
            x = x + jnp.where(rows >= s, pltpu.roll(x, s, 0), 0.0)
        s *= 2
    return x


def _hgrn_kernel(*refs, seq, has_init, want_state):
    hq_ref, hff_ref, hfb_ref, hi_ref, hgt_ref, lbf_ref, lbb_ref, ng_ref = refs[:8]
    pos = 8
    if has_init:
        sf0_ref, sb0_ref = refs[pos:pos + 2]
        pos += 2
    o_ref = refs[pos]
    pos += 1
    if want_state:
        sf_ref, sb_ref = refs[pos:pos + 2]
        pos += 2
    q_scr, of_scr, ob_scr, st_scr = refs[pos:pos + 4]

    C = HG_CHUNK
    n_chunks = seq // C
    q_in = hq_ref[...]
    q_scr[...] = q_in * _sigmoid(q_in)
    for h in range(HG_GROUP):
        if has_init:
            st_scr[h] = sf0_ref[h].T
            st_scr[HG_GROUP + h] = sb0_ref[h].T
        else:
            st_scr[h] = jnp.zeros((HG_DV, HG_DK), F32)
            st_scr[HG_GROUP + h] = jnp.zeros((HG_DV, HG_DK), F32)

    rows = lax.broadcasted_iota(jnp.int32, (C, HG_DK), 0)
    ti = lax.broadcasted_iota(jnp.int32, (C, C), 0)
    si = lax.broadcasted_iota(jnp.int32, (C, C), 1)
    dirs = ((hff_ref, lbf_ref, of_scr, False, si <= ti, C // 2 - 1, C - 1),
            (hfb_ref, lbb_ref, ob_scr, True, si >= ti, C // 2, 0))

    def chunk_step(c, carry):
        starts = (pl.multiple_of(c * C, C), pl.multiple_of((n_chunks - 1 - c) * C, C))
        streams = [(h, d) for h in range(HG_GROUP) for d in range(2)]
        prep = []
        for h, d in streams:
            z_ref, lb_ref, _, reverse, _, mid, last = dirs[d]
            cols = slice(h * HG_DK, (h + 1) * HG_DK)
            rs = pl.ds(starts[d], C)
            q = q_scr[rs, cols]
            v = hi_ref[rs, cols].astype(BF16)
            lb = lb_ref[:, cols]
            sig = _sigmoid(z_ref[rs, cols])
            log_f = jnp.log(lb + (1.0 - lb) * sig)
            k = (1.0 - lb) * (1.0 - sig)
            b = _chunk_cumsum(log_f, rows, reverse)
            b_mid = b[mid:mid + 1, :]
            b_last = b[last:last + 1, :]
            e_mid = jnp.exp(b - b_mid)
            qe = q * e_mid
            ke = k * (1.0 / e_mid)
            prep.append(dict(
                v=v,
                qe=qe.astype(BF16),
                ke=ke.astype(BF16),
                qb=(qe * jnp.exp(b_mid)).astype(BF16),
                kl=(ke * jnp.exp(b_last - b_mid)).astype(BF16),
                decay=jnp.exp(b_last)))
        a_raw = [lax.dot_general(p["qe"], p["ke"], NT_DIMS, preferred_element_type=F32) for p in prep]
        states = [st_scr[d * HG_GROUP + h] for h, d in streams]
        o_inter = [lax.dot_general(p["qb"], st.astype(BF16), NT_DIMS, preferred_element_type=F32)
                   for p, st in zip(prep, states)]
        u_t = [lax.dot_general(p["v"], p["kl"], TN_DIMS, preferred_element_type=F32) for p in prep]
        o_intra = [jnp.dot(jnp.where(dirs[d][4], a, 0.0).astype(BF16), p["v"], preferred_element_type=F32)
                   for (h, d), a, p in zip(streams, a_raw, prep)]
        for i, (h, d) in enumerate(streams):
            cols = slice(h * HG_DK, (h + 1) * HG_DK)
            dirs[d][2][pl.ds(starts[d], C), cols] = o_intra[i] + o_inter[i]
            st_scr[d * HG_GROUP + h] = states[i] * prep[i]["decay"] + u_t[i]
        return carry

    lax.fori_loop(0, n_chunks, chunk_step, 0, unroll=4)

    for h in range(HG_GROUP):
        cols = slice(h * HG_DK, (h + 1) * HG_DK)
        o = of_scr[:, cols] + ob_scr[:, cols]
        o = o * lax.rsqrt(jnp.mean(o * o, -1, keepdims=True) + RMS_EPS) * ng_ref[...]
        gt = hgt_ref[:, cols]
        o_ref[:, cols] = (o * (gt * _sigmoid(gt))).astype(o_ref.dtype)
        if want_state:
            sf_ref[h] = st_scr[h].T
            sb_ref[h] = st_scr[HG_GROUP + h].T


def _hgrn(z, lb_f, lb_b, norm_g, row0, n_batch, seq, init_states=None, layer=None, want_state=False):
    n_groups = HG_HEADS // HG_GROUP
    r0 = row0 // seq
    has_init = init_states is not None

    def zspec(off):
        return pl.BlockSpec((seq, HG_GW), lambda b, g, o=off // HG_GW: (r0 + b, o + g))

    lbspec = pl.BlockSpec((1, HG_GW), lambda b, g: (0, g))
    in_specs = [zspec(HQ_OFF), zspec(HFF_OFF), zspec(HFB_OFF), zspec(HI_OFF), zspec(HGT_OFF),
                lbspec, lbspec, pl.BlockSpec((1, HG_DV), lambda b, g: (0, 0))]
    args = [z, z, z, z, z, lb_f.reshape(1, HG_DIM), lb_b.reshape(1, HG_DIM), norm_g.reshape(1, HG_DV)]
    if has_init:
        st_spec = pl.BlockSpec((None, None, HG_GROUP, HG_DK, HG_DV), lambda b, g: (b, layer, g, 0, 0))
        in_specs += [st_spec, st_spec]
        args += list(init_states)
    out_specs = [pl.BlockSpec((seq, HG_GW), lambda b, g: (b, g))]
    out_shape = [jax.ShapeDtypeStruct((n_batch * seq, HG_VDIM), BF16)]
    if want_state:
        so = pl.BlockSpec((None, HG_GROUP, HG_DK, HG_DV), lambda b, g: (b, g, 0, 0))
        out_specs += [so, so]
        out_shape += [jax.ShapeDtypeStruct((n_batch, HG_HEADS, HG_DK, HG_DV), F32)] * 2
    return pl.pallas_call(
        functools.partial(_hgrn_kernel, seq=seq, has_init=has_init, want_state=want_state),
        grid=(n_batch, n_groups),
        in_specs=in_specs,
        out_specs=out_specs,
        out_shape=out_shape,
        scratch_shapes=[pltpu.VMEM((seq, HG_GW), F32), pltpu.VMEM((seq, HG_GW), F32),
                        pltpu.VMEM((seq, HG_GW), F32),
                        pltpu.VMEM((2 * HG_GROUP, HG_DV, HG_DK), F32)],
        compiler_params=_params(2),
        name=f"hgrn_t{seq}",
    )(*args)


def _oproj_kernel(attc_ref, attl_ref, hgc_ref, hgl_ref, w_ref, x_ref, mod_ref, g_ref, b_ref, o_ref, *,
                  n_ctx_tiles):
    is_ctx = pl.program_id(0) < n_ctx_tiles
    att = jnp.where(is_ctx, attc_ref[...], attl_ref[...])
    hg = jnp.where(is_ctx, hgc_ref[...], hgl_ref[...])
    mix = (jnp.dot(att, w_ref[0:ATT_DIM, :], preferred_element_type=F32)
           + jnp.dot(hg, w_ref[ATT_DIM:MIX_DIM, :], preferred_element_type=F32))
    y = DEEPNORM_ALPHA * x_ref[...] + mod_ref[2:3, :] * mix
    o_ref[...] = _layer_norm(y, g_ref[0:1, :], b_ref[0:1, :])


def _oproj(att_c, att_l, hg_c, hg_l, w_o, x, mods, ln_g, ln_b, layer, mod_idx):
    n_tok = x.shape[0]
    n_ctx_tiles = att_c.shape[0] // OPROJ_TM
    row = lambda i: (i, 0)
    ctx_row = lambda i: (jnp.minimum(i, n_ctx_tiles - 1), 0)
    lat_row = lambda i: (jnp.maximum(i - n_ctx_tiles, 0), 0)
    return pl.pallas_call(
        functools.partial(_oproj_kernel, n_ctx_tiles=n_ctx_tiles),
        grid=(n_tok // OPROJ_TM,),
        in_specs=[pl.BlockSpec((OPROJ_TM, ATT_DIM), ctx_row),
                  pl.BlockSpec((OPROJ_TM, ATT_DIM), lat_row),
                  pl.BlockSpec((OPROJ_TM, HG_VDIM), ctx_row),
                  pl.BlockSpec((OPROJ_TM, HG_VDIM), lat_row),
                  pl.BlockSpec((None, MIX_DIM, D_MODEL), lambda i: (layer, 0, 0)),
                  pl.BlockSpec((OPROJ_TM, D_MODEL), row),
                  pl.BlockSpec((None, None, 6, D_MODEL), lambda i: (layer, mod_idx(i), 0, 0)),
                  pl.BlockSpec((None, 2, D_MODEL), lambda i: (layer, 0, 0)),
                  pl.BlockSpec((None, 2, D_MODEL), lambda i: (layer, 0, 0))],
        out_specs=pl.BlockSpec((OPROJ_TM, D_MODEL), row),
        out_shape=jax.ShapeDtypeStruct((n_tok, D_MODEL), F32),
        compiler_params=_params(1),
        name="oproj_ln",
    )(att_c, att_l, hg_c, hg_l, w_o, x, mods, ln_g, ln_b)


def _mlp_kernel(x_ref, mod_ref, wu_ref, wd_ref, g_ref, b_ref, o_ref, h_ref):
    f = pl.program_id(1)

    @pl.when(f == 0)
    def _():
        shift, scale = mod_ref[3:4, :], mod_ref[4:5, :]
        h_ref[...] = (x_ref[...] * (1.0 + scale) + shift).astype(BF16)
        o_ref[...] = jnp.zeros_like(o_ref)

    u = jnp.dot(h_ref[...], wu_ref[...], preferred_element_type=F32)
    u = jnp.square(jnp.maximum(u, 0.0)).astype(BF16)
    for n in range(0, D_MODEL, MLP_TN):
        o_ref[:, n:n + MLP_TN] += jnp.dot(u, wd_ref[:, n:n + MLP_TN], preferred_element_type=F32)

    @pl.when(f == pl.num_programs(1) - 1)
    def _():
        y = DEEPNORM_ALPHA * x_ref[...] + mod_ref[5:6, :] * o_ref[...]
        o_ref[...] = _layer_norm(y, g_ref[1:2, :], b_ref[1:2, :])


def _mlp(x, mods, w_up, w_down, ln_g, ln_b, layer, mod_idx):
    n_tok = x.shape[0]
    return pl.pallas_call(
        _mlp_kernel,
        grid=(n_tok // MLP_TM, D_FF // MLP_TF),
        in_specs=[pl.BlockSpec((MLP_TM, D_MODEL), lambda i, f: (i, 0)),
                  pl.BlockSpec((None, None, 6, D_MODEL), lambda i, f: (layer, mod_idx(i), 0, 0)),
                  pl.BlockSpec((None, D_MODEL, MLP_TF), lambda i, f: (layer, 0, f)),
                  pl.BlockSpec((None, MLP_TF, D_MODEL), lambda i, f: (layer, f, 0)),
                  pl.BlockSpec((None, 2, D_MODEL), lambda i, f: (layer, 0, 0)),
                  pl.BlockSpec((None, 2, D_MODEL), lambda i, f: (layer, 0, 0))],
        out_specs=pl.BlockSpec((MLP_TM, D_MODEL), lambda i, f: (i, 0)),
        out_shape=jax.ShapeDtypeStruct((n_tok, D_MODEL), F32),
        scratch_shapes=[pltpu.VMEM((MLP_TM, D_MODEL), BF16)],
        compiler_params=_params(2),
        name="mlp_ln",
    )(x, mods, w_up, w_down, ln_g, ln_b)


def _lower_bounds(lb_logits):
    p = jax.nn.softmax(lb_logits.astype(F32), axis=0)
    cs = jnp.cumsum(p, axis=0)
    return cs - cs[0:1]


def kernel(x_prompt, x_sample, cache_k, cache_v, state_hgrn_fwd, state_hgrn_bwd, c, c_ctx, w_mod, b_mod,
           w_in, attn_sink, attn_norm_g, hg_lb_logits, hg_norm_g, w_o, ln_g, ln_b, w_up, w_down):
    bp, seq_p, d = x_prompt.shape
    bs, seq_s, _ = x_sample.shape
    depth = w_in.shape[0]
    past = cache_k.shape[2]
    n_ctx, n_lat = bp * seq_p, bs * seq_s

    x = jnp.concatenate([x_prompt.reshape(n_ctx, d), x_sample.reshape(n_lat, d)], 0)
    c_rows = jnp.concatenate([c_ctx[None, :], c, jnp.zeros((MOD_ROWS - 1 - bs, d), F32)], 0)
    mods = _modulation(c_rows, w_mod, b_mod)[:, :1 + bs].reshape(depth, 1 + bs, 6, d)

    w_in_b, w_o_b, w_up_b, w_down_b = (w.astype(BF16) for w in (w_in, w_o, w_up, w_down))
    lb_f, lb_b = _lower_bounds(hg_lb_logits[0]), _lower_bounds(hg_lb_logits[1])
    rope_tabs = _rope_tables(seq_s)
    ck = cache_k.reshape(bs, depth, past, KV_DIM)
    cv = cache_v.reshape(bs, depth, past, KV_DIM)

    new_k, new_v, new_sf, new_sb = [], [], [], []
    for l in range(depth):
        z, k_c, v_c = _inproj(x, mods, w_in_b, l, _mod_index(TOKEN_TILE, n_ctx, seq_s), n_ctx)
        new_k.append(k_c.reshape(bp, seq_p, KV_HEADS, HEAD_DIM))
        new_v.append(v_c.reshape(bp, seq_p, KV_HEADS, HEAD_DIM))

        att_c = _ctx_attention(z, attn_sink[l], attn_norm_g[l], bp, seq_p)
        att_l = _lat_attention(z, ck, cv, l, attn_sink[l], attn_norm_g[l], rope_tabs, n_ctx, bs, seq_s)

        hg_c, s_f, s_b = _hgrn(z, lb_f[l], lb_b[l], hg_norm_g[l], 0, bp, seq_p, want_state=True)
        (hg_l,) = _hgrn(z, lb_f[l], lb_b[l], hg_norm_g[l], n_ctx, bs, seq_s,
                        init_states=(state_hgrn_fwd, state_hgrn_bwd), layer=l)
        new_sf.append(s_f)
        new_sb.append(s_b)

        x = _oproj(att_c, att_l, hg_c, hg_l, w_o_b, x, mods, ln_g, ln_b, l,
                   _mod_index(OPROJ_TM, n_ctx, seq_s))
        x = _mlp(x, mods, w_up_b, w_down_b, ln_g, ln_b, l, _mod_index(MLP_TM, n_ctx, seq_s))

    return (x[:n_ctx].reshape(bp, seq_p, d), x[n_ctx:].reshape(bs, seq_s, d),
            jnp.stack(new_k, 1), jnp.stack(new_v, 1), jnp.stack(new_sf, 1), jnp.stack(new_sb, 1))
```

```python
import functools

import jax
import jax.numpy as jnp
from jax import lax
from jax.experimental import pallas as pl
from jax.experimental.pallas import tpu as pltpu

F32 = jnp.float32
BF16 = jnp.bfloat16

D_MODEL = 2048
DEPTH = 4
GRID_W = 64
HEAD_DIM = 128
ATT_HEADS = 8
KV_HEADS = 2
Q_PER_KV = ATT_HEADS // KV_HEADS
ATT_DIM = ATT_HEADS * HEAD_DIM
KV_DIM = KV_HEADS * HEAD_DIM
WINDOW = 128
BLOCK = 128
ATT_SCALE = HEAD_DIM ** -0.5
ROPE_THETA = 10000.0
ROPE_FREQS = HEAD_DIM // 4
HG_HEADS = 8
HG_DK = 128
HG_DV = 128
HG_DIM = HG_HEADS * HG_DK
HG_VDIM = HG_HEADS * HG_DV
HG_CHUNK = 32
MIX_DIM = ATT_DIM + HG_VDIM
IN_DIM = ATT_DIM + 2 * KV_DIM + 3 * HG_DIM + 2 * HG_VDIM
D_FF = 4 * D_MODEL
LN_EPS = 1e-5
RMS_EPS = 1e-6
DEEPNORM_ALPHA = (2 * DEPTH) ** 0.25
NEG_INF = -1e30
LOG2E = 1.4426950408889634

Q_OFF = 0
K_OFF = ATT_DIM
V_OFF = K_OFF + KV_DIM
HQ_OFF = V_OFF + KV_DIM
HFF_OFF = HQ_OFF + HG_DIM
HFB_OFF = HFF_OFF + HG_DIM
HI_OFF = HFB_OFF + HG_DIM
HGT_OFF = HI_OFF + HG_VDIM

V7X_VMEM_LIMIT_BYTES = 56 * 1024 * 1024

MOD_ROWS = 8
MOD_TN = 768
TOKEN_TILE = 1024
INPROJ_TN = 1664
OPROJ_TM = 512
MLP_TM = 512
MLP_TF = 1024
MLP_TN = 512
CAST_STEPS = 128
HG_GROUP = 4
HG_GW = HG_GROUP * HG_DK

NT_DIMS = (((1,), (1,)), ((), ()))
TN_DIMS = (((0,), (0,)), ((), ()))


def _params(n_axes):
    return pltpu.CompilerParams(dimension_semantics=("arbitrary",) * n_axes,
                                vmem_limit_bytes=V7X_VMEM_LIMIT_BYTES)


def _sigmoid(x):
    return 1.0 / (1.0 + jnp.exp(-x))


def _layer_norm(y, g, b):
    mu = jnp.mean(y, -1, keepdims=True)
    yc = y - mu
    var = jnp.mean(yc * yc, -1, keepdims=True)
    return yc * lax.rsqrt(var + LN_EPS) * g + b


def _mod_kernel(c_ref, w_ref, b_ref, o_ref):
    c = c_ref[...]
    a = (c * _sigmoid(c)).astype(BF16)
    o_ref[0] = jnp.dot(a, w_ref[0].astype(BF16), preferred_element_type=F32) + b_ref[0]


def _modulation(c_rows, w_mod, b_mod):
    depth, d, n = w_mod.shape
    return pl.pallas_call(
        _mod_kernel,
        grid=(depth, n // MOD_TN),
        in_specs=[pl.BlockSpec((MOD_ROWS, d), lambda l, j: (0, 0)),
                  pl.BlockSpec((1, d, MOD_TN), lambda l, j: (l, 0, j)),
                  pl.BlockSpec((1, 1, MOD_TN), lambda l, j: (l, 0, j))],
        out_specs=pl.BlockSpec((1, MOD_ROWS, MOD_TN), lambda l, j: (l, 0, j)),
        out_shape=jax.ShapeDtypeStruct((depth, MOD_ROWS, n), F32),
        compiler_params=_params(2),
        name="modulation",
    )(c_rows, w_mod, b_mod.reshape(depth, 1, n))


def _mod_index(tile_rows, n_ctx_rows, lat_rows):
    n_ctx_tiles = n_ctx_rows // tile_rows
    per_lat = lat_rows // tile_rows

    def idx(i):
        return jnp.where(i < n_ctx_tiles, 0, 1 + (i - n_ctx_tiles) // per_lat)
    return idx


def _inproj_kernel(x_ref, mod_ref, w_ref, z_ref, k_ref, v_ref, h_ref, *, n_ctx_tiles):
    i, j = pl.program_id(0), pl.program_id(1)

    @pl.when(j == 0)
    def _():
        shift, scale = mod_ref[0:1, :], mod_ref[1:2, :]
        h_ref[...] = (x_ref[...] * (1.0 + scale) + shift).astype(BF16)

    z_ref[...] = jnp.dot(h_ref[...], w_ref[...], preferred_element_type=F32)

    @pl.when(jnp.logical_and(j == 0, i < n_ctx_tiles))
    def _():
        k_ref[...] = z_ref[:, K_OFF:K_OFF + KV_DIM]
        v_ref[...] = z_ref[:, V_OFF:V_OFF + KV_DIM]


def _inproj(x, mods, w_in, layer, mod_idx, n_ctx):
    assert V_OFF + KV_DIM <= INPROJ_TN
    n_tok = x.shape[0]
    n_ctx_tiles = n_ctx // TOKEN_TILE
    kv_spec = pl.BlockSpec((TOKEN_TILE, KV_DIM), lambda i, j: (jnp.minimum(i, n_ctx_tiles - 1), 0))
    return pl.pallas_call(
        functools.partial(_inproj_kernel, n_ctx_tiles=n_ctx_tiles),
        grid=(n_tok // TOKEN_TILE, IN_DIM // INPROJ_TN),
        in_specs=[pl.BlockSpec((TOKEN_TILE, D_MODEL), lambda i, j: (i, 0)),
                  pl.BlockSpec((None, None, 6, D_MODEL), lambda i, j: (layer, mod_idx(i), 0, 0)),
                  pl.BlockSpec((D_MODEL, INPROJ_TN), lambda i, j: (0, j))],
        out_specs=[pl.BlockSpec((TOKEN_TILE, INPROJ_TN), lambda i, j: (i, j)), kv_spec, kv_spec],
        out_shape=[jax.ShapeDtypeStruct((n_tok, IN_DIM), F32),
                   jax.ShapeDtypeStruct((n_ctx, KV_DIM), F32),
                   jax.ShapeDtypeStruct((n_ctx, KV_DIM), F32)],
        scratch_shapes=[pltpu.VMEM((TOKEN_TILE, D_MODEL), BF16)],
        compiler_params=_params(2),
        name="inproj",
    )(x, mods, w_in)


def _with_ones(v):
    return jnp.concatenate([v, jnp.ones_like(v)], axis=1)


def _sink_attend(scores, values, sink2):
    c = ATT_SCALE * LOG2E
    m_raw = jnp.max(scores[0], -1, keepdims=True)
    for s in scores[1:]:
        m_raw = jnp.maximum(m_raw, jnp.max(s, -1, keepdims=True))
    m2 = jnp.maximum(m_raw * c, sink2)
    acc = None
    for s, v1 in zip(scores, values):
        e = jnp.exp2(s * c - m2).astype(BF16)
        part = jnp.dot(e, v1, preferred_element_type=F32)
        acc = part if acc is None else acc + part
    den = acc[:, HEAD_DIM:HEAD_DIM + 1] + jnp.exp2(sink2 - m2)
    return acc[:, :HEAD_DIM] * (1.0 / den)


def _rms_norm_store(o_scr, sq, g_ref, o_ref):
    ss = jnp.sum(sq, -1, keepdims=True)
    inv = lax.rsqrt(ss * (1.0 / ATT_DIM) + RMS_EPS)
    o_ref[...] = (o_scr[...] * inv * g_ref[...]).astype(o_ref.dtype)


def _head_cols(g, r):
    h = g * Q_PER_KV + r
    return slice(h * HEAD_DIM, (h + 1) * HEAD_DIM)


def _group_sinks(sink_ref, g, rows):
    return jnp.concatenate([jnp.full((rows, 1), sink_ref[g * Q_PER_KV + r] * LOG2E, F32)
                            for r in range(Q_PER_KV)], axis=0)


def _scatter_heads(og, g, rows, sq, o_scr):
    for r in range(Q_PER_KV):
        oh = og[r * rows:(r + 1) * rows]
        sq = sq + oh * oh
        o_scr[:, _head_cols(g, r)] = oh
    return sq


def _ctx_attn_kernel(sink_ref, q_ref, k_ref, v_ref, g_ref, o_ref, o_scr):
    c = ATT_SCALE * LOG2E
    sq = jnp.zeros((q_ref.shape[0], HEAD_DIM), F32)
    for g in range(KV_HEADS):
        kv_cols = slice(g * HEAD_DIM, (g + 1) * HEAD_DIM)
        kg = k_ref[:, kv_cols].astype(BF16)
        vg = v_ref[:, kv_cols].astype(BF16)
        for r in range(Q_PER_KV):
            cols = _head_cols(g, r)
            sink2 = sink_ref[g * Q_PER_KV + r] * LOG2E
            s = lax.dot_general(q_ref[:, cols].astype(BF16), kg, NT_DIMS, preferred_element_type=F32)
            m2 = jnp.maximum(jnp.max(s, -1, keepdims=True) * c, sink2)
            e = jnp.exp2(s * c - m2)
            den = jnp.sum(e, -1, keepdims=True) + jnp.exp2(sink2 - m2)
            p = (e * (1.0 / den)).astype(BF16)
            oh = jnp.dot(p, vg, preferred_element_type=F32)
            sq = sq + oh * oh
            o_scr[:, cols] = oh
    _rms_norm_store(o_scr, sq, g_ref, o_ref)


def _ctx_attention(z, sink, norm_g, n_batch, seq):
    kb, vb = K_OFF // KV_DIM, V_OFF // KV_DIM
    return pl.pallas_call(
        _ctx_attn_kernel,
        grid=(n_batch,),
        in_specs=[pl.BlockSpec(memory_space=pltpu.SMEM),
                  pl.BlockSpec((seq, ATT_DIM), lambda b: (b, 0)),
                  pl.BlockSpec((seq, KV_DIM), lambda b: (b, kb)),
                  pl.BlockSpec((seq, KV_DIM), lambda b: (b, vb)),
                  pl.BlockSpec((1, ATT_DIM), lambda b: (0, 0))],
        out_specs=pl.BlockSpec((seq, ATT_DIM), lambda b: (b, 0)),
        out_shape=jax.ShapeDtypeStruct((n_batch * seq, ATT_DIM), BF16),
        scratch_shapes=[pltpu.VMEM((seq, ATT_DIM), F32)],
        compiler_params=_params(1),
        name="ctx_attention",
    )(sink, z, z, z, norm_g.reshape(1, ATT_DIM))


def _rope(x, cos, sin_lo, sin_hi):
    return (x * cos + pltpu.roll(x, HEAD_DIM - ROPE_FREQS, 1) * sin_lo
            + pltpu.roll(x, ROPE_FREQS, 1) * sin_hi)


def _lat_attn_kernel(sink_ref, q_ref, k_ref, v_ref, ck_ref, cv_ref, cq_ref, slq_ref, shq_ref,
                     ca_ref, sla_ref, sha_ref, g_ref, o_ref, kr_scr, v_scr, ckb_scr, cvb_scr, o_scr):
    n = pl.program_id(1)
    seq = k_ref.shape[0]
    win = 3 * BLOCK

    @pl.when(n == 0)
    def _():
        for g in range(KV_HEADS):
            cols = slice(g * HEAD_DIM, (g + 1) * HEAD_DIM)
            kr_scr[:, cols] = _rope(k_ref[:, cols], ca_ref[...], sla_ref[...], sha_ref[...]).astype(BF16)
            v_scr[g] = _with_ones(v_ref[:, cols].astype(BF16))
            cvb_scr[g] = _with_ones(cv_ref[:, cols].astype(BF16))
        ckb_scr[...] = ck_ref[...].astype(BF16)

    k0 = pl.multiple_of(jnp.clip((n - 1) * BLOCK, 0, seq - win), BLOCK)
    rows = Q_PER_KV * BLOCK
    qpos = n * BLOCK + lax.broadcasted_iota(jnp.int32, (rows, win), 0) % BLOCK
    kpos = k0 + lax.broadcasted_iota(jnp.int32, (rows, win), 1)
    valid = jnp.abs(kpos - qpos) <= WINDOW
    sq = jnp.zeros((BLOCK, HEAD_DIM), F32)
    for g in range(KV_HEADS):
        kv_cols = slice(g * HEAD_DIM, (g + 1) * HEAD_DIM)
        k_loc = kr_scr[pl.ds(k0, win), kv_cols]
        v_loc = v_scr[g, pl.ds(k0, win), :]
        k_ctx = ckb_scr[:, kv_cols]
        v_ctx = cvb_scr[g]
        qg = jnp.concatenate(
            [_rope(q_ref[:, _head_cols(g, r)], cq_ref[...], slq_ref[...], shq_ref[...]).astype(BF16)
             for r in range(Q_PER_KV)], axis=0)
        s_loc = lax.dot_general(qg, k_loc, NT_DIMS, preferred_element_type=F32)
        s_loc = jnp.where(valid, s_loc, NEG_INF / ATT_SCALE)
        s_ctx = lax.dot_general(qg, k_ctx, NT_DIMS, preferred_element_type=F32)
        og = _sink_attend([s_loc, s_ctx], [v_loc, v_ctx], _group_sinks(sink_ref, g, BLOCK))
        sq = _scatter_heads(og, g, BLOCK, sq, o_scr)
    _rms_norm_store(o_scr, sq, g_ref, o_ref)


def _lat_attention(z, cache_k, cache_v, layer, sink, norm_g, rope_tabs, row0, n_batch, seq):
    kb, vb = K_OFF // KV_DIM, V_OFF // KV_DIM
    nq = seq // BLOCK
    qrow0, srow0 = row0 // BLOCK, row0 // seq
    past = cache_k.shape[2]
    cos, sin_lo, sin_hi = rope_tabs
    qtab = pl.BlockSpec((BLOCK, HEAD_DIM), lambda b, n: (n, 0))
    atab = pl.BlockSpec((seq, HEAD_DIM), lambda b, n: (0, 0))
    cache_spec = pl.BlockSpec((None, None, past, KV_DIM), lambda b, n: (b, layer, 0, 0))
    return pl.pallas_call(
        _lat_attn_kernel,
        grid=(n_batch, nq),
        in_specs=[pl.BlockSpec(memory_space=pltpu.SMEM),
                  pl.BlockSpec((BLOCK, ATT_DIM), lambda b, n: (qrow0 + b * nq + n, 0)),
                  pl.BlockSpec((seq, KV_DIM), lambda b, n: (srow0 + b, kb)),
                  pl.BlockSpec((seq, KV_DIM), lambda b, n: (srow0 + b, vb)),
                  cache_spec, cache_spec, qtab, qtab, qtab, atab, atab, atab,
                  pl.BlockSpec((1, ATT_DIM), lambda b, n: (0, 0))],
        out_specs=pl.BlockSpec((BLOCK, ATT_DIM), lambda b, n: (b * nq + n, 0)),
        out_shape=jax.ShapeDtypeStruct((n_batch * seq, ATT_DIM), BF16),
        scratch_shapes=[pltpu.VMEM((seq, KV_DIM), BF16), pltpu.VMEM((KV_HEADS, seq, 2 * HEAD_DIM), BF16),
                        pltpu.VMEM((past, KV_DIM), BF16), pltpu.VMEM((KV_HEADS, past, 2 * HEAD_DIM), BF16),
                        pltpu.VMEM((BLOCK, ATT_DIM), F32)],
        compiler_params=_params(2),
        name="lat_attention",
    )(sink, z, z, z, cache_k, cache_v, cos, sin_lo, sin_hi, cos, sin_lo, sin_hi,
      norm_g.reshape(1, ATT_DIM))


def _rope_tables(seq):
    rows = seq // GRID_W
    row = jnp.repeat(jnp.arange(rows, dtype=F32), GRID_W)
    col = jnp.tile(jnp.arange(GRID_W, dtype=F32), rows)
    inv = ROPE_THETA ** (-jnp.arange(ROPE_FREQS, dtype=F32) / ROPE_FREQS)
    ang_r, ang_c = row[:, None] * inv, col[:, None] * inv
    cr, sr, cc, sc = jnp.cos(ang_r), jnp.sin(ang_r), jnp.cos(ang_c), jnp.sin(ang_c)
    zero = jnp.zeros_like(sr)
    cos = jnp.concatenate([cr, cr, cc, cc], -1)
    sin_lo = jnp.concatenate([-sr, zero, -sc, zero], -1)
    sin_hi = jnp.concatenate([zero, sr, zero, sc], -1)
    return cos, sin_lo, sin_hi


def _chunk_cumsum(x, rows, reverse):
    s = 1
    while s < HG_CHUNK:
        if reverse:
            x = x + jnp.where(rows < HG_CHUNK - s, pltpu.roll(x, HG_CHUNK - s, 0), 0.0)
        else:
            x = x + jnp.where(rows >= s, pltpu.roll(x, s, 0), 0.0)
        s *= 2
    return x


def _hgrn_kernel(*refs, seq, has_init, want_state):
    hq_ref, hff_ref, hfb_ref, hi_ref, hgt_ref, lbf_ref, lbb_ref, ng_ref = refs[:8]
    pos = 8
    if has_init:
        sf0_ref, sb0_ref = refs[pos:pos + 2]
        pos += 2
    o_ref = refs[pos]
    pos += 1
    if want_state:
        sf_ref, sb_ref = refs[pos:pos + 2]
        pos += 2
    q_scr, of_scr, ob_scr, st_scr = refs[pos:pos + 4]

    C = HG_CHUNK
    n_chunks = seq // C
    q_in = hq_ref[...]
    q_scr[...] = q_in * _sigmoid(q_in)
    for h in range(HG_GROUP):
        if has_init:
            st_scr[h] = sf0_ref[h].T
            st_scr[HG_GROUP + h] = sb0_ref[h].T
        else:
            st_scr[h] = jnp.zeros((HG_DV, HG_DK), F32)
            st_scr[HG_GROUP + h] = jnp.zeros((HG_DV, HG_DK), F32)

    rows = lax.broadcasted_iota(jnp.int32, (C, HG_DK), 0)
    ti = lax.broadcasted_iota(jnp.int32, (C, C), 0)
    si = lax.broadcasted_iota(jnp.int32, (C, C), 1)
    dirs = ((hff_ref, lbf_ref, of_scr, False, si <= ti, C // 2 - 1, C - 1),
            (hfb_ref, lbb_ref, ob_scr, True, si >= ti, C // 2, 0))

    def chunk_step(c, carry):
        starts = (pl.multiple_of(c * C, C), pl.multiple_of((n_chunks - 1 - c) * C, C))
        streams = [(h, d) for h in range(HG_GROUP) for d in range(2)]
        prep = []
        for h, d in streams:
            z_ref, lb_ref, _, reverse, _, mid, last = dirs[d]
            cols = slice(h * HG_DK, (h + 1) * HG_DK)
            rs = pl.ds(starts[d], C)
            q = q_scr[rs, cols]
            v = hi_ref[rs, cols].astype(BF16)
            lb = lb_ref[:, cols]
            sig = _sigmoid(z_ref[rs, cols])
            log_f = jnp.log(lb + (1.0 - lb) * sig)
            k = (1.0 - lb) * (1.0 - sig)
            b = _chunk_cumsum(log_f, rows, reverse)
            b_mid = b[mid:mid + 1, :]
            b_last = b[last:last + 1, :]
            e_mid = jnp.exp(b - b_mid)
            qe = q * e_mid
            ke = k * (1.0 / e_mid)
            prep.append(dict(
                v=v,
                qe=qe.astype(BF16),
                ke=ke.astype(BF16),
                qb=(qe * jnp.exp(b_mid)).astype(BF16),
                kl=(ke * jnp.exp(b_last - b_mid)).astype(BF16),
                decay=jnp.exp(b_last)))
        a_raw = [lax.dot_general(p["qe"], p["ke"], NT_DIMS, preferred_element_type=F32) for p in prep]
        states = [st_scr[d * HG_GROUP + h] for h, d in streams]
        o_inter = [lax.dot_general(p["qb"], st.astype(BF16), NT_DIMS, preferred_element_type=F32)
                   for p, st in zip(prep, states)]
        u_t = [lax.dot_general(p["v"], p["kl"], TN_DIMS, preferred_element_type=F32) for p in prep]
        o_intra = [jnp.dot(jnp.where(dirs[d][4], a, 0.0).astype(BF16), p["v"], preferred_element_type=F32)
                   for (h, d), a, p in zip(streams, a_raw, prep)]
        for i, (h, d) in enumerate(streams):
            cols = slice(h * HG_DK, (h + 1) * HG_DK)
            dirs[d][2][pl.ds(starts[d], C), cols] = o_intra[i] + o_inter[i]
            st_scr[d * HG_GROUP + h] = states[i] * prep[i]["decay"] + u_t[i]
        return carry

    lax.fori_loop(0, n_chunks, chunk_step, 0, unroll=4)

    for h in range(HG_GROUP):
        cols = slice(h * HG_DK, (h + 1) * HG_DK)
        o = of_scr[:, cols] + ob_scr[:, cols]
        o = o * lax.rsqrt(jnp.mean(o * o, -1, keepdims=True) + RMS_EPS) * ng_ref[...]
        gt = hgt_ref[:, cols]
        o_ref[:, cols] = (o * (gt * _sigmoid(gt))).astype(o_ref.dtype)
        if want_state:
            sf_ref[h] = st_scr[h].T
            sb_ref[h] = st_scr[HG_GROUP + h].T


def _hgrn(z, lb_f, lb_b, norm_g, row0, n_batch, seq, init_states=None, layer=None, want_state=False):
    n_groups = HG_HEADS // HG_GROUP
    r0 = row0 // seq
    has_init = init_states is not None

    def zspec(off):
        return pl.BlockSpec((seq, HG_GW), lambda b, g, o=off // HG_GW: (r0 + b, o + g))

    lbspec = pl.BlockSpec((1, HG_GW), lambda b, g: (0, g))
    in_specs = [zspec(HQ_OFF), zspec(HFF_OFF), zspec(HFB_OFF), zspec(HI_OFF), zspec(HGT_OFF),
                lbspec, lbspec, pl.BlockSpec((1, HG_DV), lambda b, g: (0, 0))]
    args = [z, z, z, z, z, lb_f.reshape(1, HG_DIM), lb_b.reshape(1, HG_DIM), norm_g.reshape(1, HG_DV)]
    if has_init:
        st_spec = pl.BlockSpec((None, None, HG_GROUP, HG_DK, HG_DV), lambda b, g: (b, layer, g, 0, 0))
        in_specs += [st_spec, st_spec]
        args += list(init_states)
    out_specs = [pl.BlockSpec((seq, HG_GW), lambda b, g: (b, g))]
    out_shape = [jax.ShapeDtypeStruct((n_batch * seq, HG_VDIM), BF16)]
    if want_state:
        so = pl.BlockSpec((None, HG_GROUP, HG_DK, HG_DV), lambda b, g: (b, g, 0, 0))
        out_specs += [so, so]
        out_shape += [jax.ShapeDtypeStruct((n_batch, HG_HEADS, HG_DK, HG_DV), F32)] * 2
    return pl.pallas_call(
        functools.partial(_hgrn_kernel, seq=seq, has_init=has_init, want_state=want_state),
        grid=(n_batch, n_groups),
        in_specs=in_specs,
        out_specs=out_specs,
        out_shape=out_shape,
        scratch_shapes=[pltpu.VMEM((seq, HG_GW), F32), pltpu.VMEM((seq, HG_GW), F32),
                        pltpu.VMEM((seq, HG_GW), F32),
                        pltpu.VMEM((2 * HG_GROUP, HG_DV, HG_DK), F32)],
        compiler_params=_params(2),
        name=f"hgrn_t{seq}",
    )(*args)


def _oproj_kernel(attc_ref, attl_ref, hgc_ref, hgl_ref, w_ref, x_ref, mod_ref, g_ref, b_ref, o_ref, *,
                  n_ctx_tiles):
    is_ctx = pl.program_id(0) < n_ctx_tiles
    att = jnp.where(is_ctx, attc_ref[...], attl_ref[...])
    hg = jnp.where(is_ctx, hgc_ref[...], hgl_ref[...])
    mix = (jnp.dot(att, w_ref[0:ATT_DIM, :], preferred_element_type=F32)
           + jnp.dot(hg, w_ref[ATT_DIM:MIX_DIM, :], preferred_element_type=F32))
    y = DEEPNORM_ALPHA * x_ref[...] + mod_ref[2:3, :] * mix
    o_ref[...] = _layer_norm(y, g_ref[0:1, :], b_ref[0:1, :])


def _oproj(att_c, att_l, hg_c, hg_l, w_o, x, mods, ln_g, ln_b, layer, mod_idx):
    n_tok = x.shape[0]
    n_ctx_tiles = att_c.shape[0] // OPROJ_TM
    row = lambda i: (i, 0)
    ctx_row = lambda i: (jnp.minimum(i, n_ctx_tiles - 1), 0)
    lat_row = lambda i: (jnp.maximum(i - n_ctx_tiles, 0), 0)
    return pl.pallas_call(
        functools.partial(_oproj_kernel, n_ctx_tiles=n_ctx_tiles),
        grid=(n_tok // OPROJ_TM,),
        in_specs=[pl.BlockSpec((OPROJ_TM, ATT_DIM), ctx_row),
                  pl.BlockSpec((OPROJ_TM, ATT_DIM), lat_row),
                  pl.BlockSpec((OPROJ_TM, HG_VDIM), ctx_row),
                  pl.BlockSpec((OPROJ_TM, HG_VDIM), lat_row),
                  pl.BlockSpec((MIX_DIM, D_MODEL), lambda i: (0, 0)),
                  pl.BlockSpec((OPROJ_TM, D_MODEL), row),
                  pl.BlockSpec((None, None, 6, D_MODEL), lambda i: (layer, mod_idx(i), 0, 0)),
                  pl.BlockSpec((None, 2, D_MODEL), lambda i: (layer, 0, 0)),
                  pl.BlockSpec((None, 2, D_MODEL), lambda i: (layer, 0, 0))],
        out_specs=pl.BlockSpec((OPROJ_TM, D_MODEL), row),
        out_shape=jax.ShapeDtypeStruct((n_tok, D_MODEL), F32),
        compiler_params=_params(1),
        name="oproj_ln",
    )(att_c, att_l, hg_c, hg_l, w_o, x, mods, ln_g, ln_b)


def _mlp_kernel(*refs, n_cast, n_out, n_ctx_tiles):
    x_ref, mod_ref, wu_ref, wd_ref, g_ref, b_ref = refs[:6]
    cast_in = refs[6:6 + n_cast]
    outs = refs[6 + n_cast:6 + n_cast + n_out]
    cast_out = refs[6 + n_cast + n_out:6 + 2 * n_cast + n_out]
    h_ref, acc_ref = refs[6 + 2 * n_cast + n_out:]
    i, f = pl.program_id(0), pl.program_id(1)
    n_f = pl.num_programs(1)

    @pl.when(f == 0)
    def _():
        shift, scale = mod_ref[3:4, :], mod_ref[4:5, :]
        h_ref[...] = (x_ref[...] * (1.0 + scale) + shift).astype(BF16)
        acc_ref[...] = jnp.zeros_like(acc_ref)

    if n_cast:
        @pl.when(i * n_f + f < CAST_STEPS)
        def _():
            for src, dst in zip(cast_in, cast_out):
                dst[...] = src[...].astype(BF16)

    u = jnp.dot(h_ref[...], wu_ref[...], preferred_element_type=F32)
    u = jnp.square(jnp.maximum(u, 0.0)).astype(BF16)
    for n in range(0, D_MODEL, MLP_TN):
        acc_ref[:, n:n + MLP_TN] += jnp.dot(u, wd_ref[:, n:n + MLP_TN], preferred_element_type=F32)

    def finish(o_ref):
        y = DEEPNORM_ALPHA * x_ref[...] + mod_ref[5:6, :] * acc_ref[...]
        o_ref[...] = _layer_norm(y, g_ref[1:2, :], b_ref[1:2, :])

    last = f == n_f - 1
    if n_out == 1:
        pl.when(last)(lambda: finish(outs[0]))
    else:
        pl.when(jnp.logical_and(last, i < n_ctx_tiles))(lambda: finish(outs[0]))
        pl.when(jnp.logical_and(last, i >= n_ctx_tiles))(lambda: finish(outs[1]))


def _cast_specs(weights, layer):
    n_f = D_FF // MLP_TF
    in_specs, out_specs, out_shape = [], [], []
    for w in weights:
        _, rows, cols = w.shape
        slab = rows // CAST_STEPS
        in_specs.append(pl.BlockSpec(
            (None, slab, cols), lambda i, f: (layer, jnp.minimum(i * n_f + f, CAST_STEPS - 1), 0)))
        out_specs.append(pl.BlockSpec(
            (slab, cols), lambda i, f: (jnp.minimum(i * n_f + f, CAST_STEPS - 1), 0)))
        out_shape.append(jax.ShapeDtypeStruct((rows, cols), BF16))
    return in_specs, out_specs, out_shape


def _mlp(x, mods, w_up, w_down, ln_g, ln_b, layer, mod_idx, next_weights=(), split_rows=None):
    n_tok = x.shape[0]
    n_f = D_FF // MLP_TF
    assert (n_tok // MLP_TM) * n_f >= CAST_STEPS
    cast_in_specs, cast_out_specs, cast_shapes = _cast_specs(next_weights, layer + 1)
    if split_rows is None:
        n_ctx_tiles = n_tok // MLP_TM
        y_specs = [pl.BlockSpec((MLP_TM, D_MODEL), lambda i, f: (i, 0))]
        y_shapes = [jax.ShapeDtypeStruct((n_tok, D_MODEL), F32)]
    else:
        n_ctx_tiles = split_rows // MLP_TM
        y_specs = [pl.BlockSpec((MLP_TM, D_MODEL), lambda i, f: (jnp.minimum(i, n_ctx_tiles - 1), 0)),
                   pl.BlockSpec((MLP_TM, D_MODEL), lambda i, f: (jnp.maximum(i - n_ctx_tiles, 0), 0))]
        y_shapes = [jax.ShapeDtypeStruct((split_rows, D_MODEL), F32),
                    jax.ShapeDtypeStruct((n_tok - split_rows, D_MODEL), F32)]
    return pl.pallas_call(
        functools.partial(_mlp_kernel, n_cast=len(next_weights), n_out=len(y_specs),
                          n_ctx_tiles=n_ctx_tiles),
        grid=(n_tok // MLP_TM, n_f),
        in_specs=[pl.BlockSpec((MLP_TM, D_MODEL), lambda i, f: (i, 0)),
                  pl.BlockSpec((None, None, 6, D_MODEL), lambda i, f: (layer, mod_idx(i), 0, 0)),
                  pl.BlockSpec((D_MODEL, MLP_TF), lambda i, f: (0, f)),
                  pl.BlockSpec((MLP_TF, D_MODEL), lambda i, f: (f, 0)),
                  pl.BlockSpec((None, 2, D_MODEL), lambda i, f: (layer, 0, 0)),
                  pl.BlockSpec((None, 2, D_MODEL), lambda i, f: (layer, 0, 0))] + cast_in_specs,
        out_specs=y_specs + cast_out_specs,
        out_shape=y_shapes + cast_shapes,
        scratch_shapes=[pltpu.VMEM((MLP_TM, D_MODEL), BF16), pltpu.VMEM((MLP_TM, D_MODEL), F32)],
        compiler_params=_params(2),
        name="mlp_ln",
    )(x, mods, w_up, w_down, ln_g, ln_b, *next_weights)


def _lower_bounds(lb_logits):
    p = jax.nn.softmax(lb_logits.astype(F32), axis=0)
    cs = jnp.cumsum(p, axis=0)
    return cs - cs[0:1]


def kernel(x_prompt, x_sample, cache_k, cache_v, state_hgrn_fwd, state_hgrn_bwd, c, c_ctx, w_mod, b_mod,
           w_in, attn_sink, attn_norm_g, hg_lb_logits, hg_norm_g, w_o, ln_g, ln_b, w_up, w_down):
    bp, seq_p, d = x_prompt.shape
    bs, seq_s, _ = x_sample.shape
    depth = w_in.shape[0]
    past = cache_k.shape[2]
    n_ctx, n_lat = bp * seq_p, bs * seq_s

    x = jnp.concatenate([x_prompt.reshape(n_ctx, d), x_sample.reshape(n_lat, d)], 0)
    c_rows = jnp.concatenate([c_ctx[None, :], c, jnp.zeros((MOD_ROWS - 1 - bs, d), F32)], 0)
    mods = _modulation(c_rows, w_mod, b_mod)[:, :1 + bs].reshape(depth, 1 + bs, 6, d)

    weights = (w_in, w_o, w_up, w_down)
    w_in_b, w_o_b, w_up_b, w_down_b = (w[0].astype(BF16) for w in weights)
    lb_f, lb_b = _lower_bounds(hg_lb_logits[0]), _lower_bounds(hg_lb_logits[1])
    rope_tabs = _rope_tables(seq_s)
    ck = cache_k.reshape(bs, depth, past, KV_DIM)
    cv = cache_v.reshape(bs, depth, past, KV_DIM)

    new_k, new_v, new_sf, new_sb = [], [], [], []
    for l in range(depth):
        z, k_c, v_c = _inproj(x, mods, w_in_b, l, _mod_index(TOKEN_TILE, n_ctx, seq_s), n_ctx)
        new_k.append(k_c.reshape(bp, seq_p, KV_HEADS, HEAD_DIM))
        new_v.append(v_c.reshape(bp, seq_p, KV_HEADS, HEAD_DIM))

        att_c = _ctx_attention(z, attn_sink[l], attn_norm_g[l], bp, seq_p)
        att_l = _lat_attention(z, ck, cv, l, attn_sink[l], attn_norm_g[l], rope_tabs, n_ctx, bs, seq_s)

        hg_c, s_f, s_b = _hgrn(z, lb_f[l], lb_b[l], hg_norm_g[l], 0, bp, seq_p, want_state=True)
        (hg_l,) = _hgrn(z, lb_f[l], lb_b[l], hg_norm_g[l], n_ctx, bs, seq_s,
                        init_states=(state_hgrn_fwd, state_hgrn_bwd), layer=l)
        new_sf.append(s_f)
        new_sb.append(s_b)

        x = _oproj(att_c, att_l, hg_c, hg_l, w_o_b, x, mods, ln_g, ln_b, l,
                   _mod_index(OPROJ_TM, n_ctx, seq_s))
        mlp_mods = _mod_index(MLP_TM, n_ctx, seq_s)
        if l + 1 < depth:
            x, w_in_b, w_o_b, w_up_b, w_down_b = _mlp(x, mods, w_up_b, w_down_b, ln_g, ln_b, l, mlp_mods,
                                                      next_weights=weights)
        else:
            y_p, y_s = _mlp(x, mods, w_up_b, w_down_b, ln_g, ln_b, l, mlp_mods, split_rows=n_ctx)

    return (y_p.reshape(bp, seq_p, d), y_s.reshape(bs, seq_s, d),
            jnp.stack(new_k, 1), jnp.stack(new_v, 1), jnp.stack(new_sf, 1), jnp.stack(new_sb, 1))
```

```python
import functools

import jax
import jax.numpy as jnp
from jax import lax
from jax.experimental import pallas as pl
from jax.experimental.pallas import tpu as pltpu

F32 = jnp.float32
BF16 = jnp.bfloat16

D_MODEL = 2048
DEPTH = 4
GRID_W = 64
HEAD_DIM = 128
ATT_HEADS = 8
KV_HEADS = 2
Q_PER_KV = ATT_HEADS // KV_HEADS
ATT_DIM = ATT_HEADS * HEAD_DIM
KV_DIM = KV_HEADS * HEAD_DIM
WINDOW = 128
BLOCK = 128
ATT_SCALE = HEAD_DIM ** -0.5
ROPE_THETA = 10000.0
ROPE_FREQS = HEAD_DIM // 4
HG_HEADS = 8
HG_DK = 128
HG_DV = 128
HG_DIM = HG_HEADS * HG_DK
HG_VDIM = HG_HEADS * HG_DV
HG_CHUNK = 32
MIX_DIM = ATT_DIM + HG_VDIM
IN_DIM = ATT_DIM + 2 * KV_DIM + 3 * HG_DIM + 2 * HG_VDIM
D_FF = 4 * D_MODEL
LN_EPS = 1e-5
RMS_EPS = 1e-6
DEEPNORM_ALPHA = (2 * DEPTH) ** 0.25
NEG_INF = -1e30
LOG2E = 1.4426950408889634

Q_OFF = 0
K_OFF = ATT_DIM
V_OFF = K_OFF + KV_DIM
HQ_OFF = V_OFF + KV_DIM
HFF_OFF = HQ_OFF + HG_DIM
HFB_OFF = HFF_OFF + HG_DIM
HI_OFF = HFB_OFF + HG_DIM
HGT_OFF = HI_OFF + HG_VDIM

V7X_VMEM_LIMIT_BYTES = 56 * 1024 * 1024

MOD_ROWS = 8
MOD_TN = 768
TOKEN_TILE = 1024
INPROJ_TN = 1664
OPROJ_TM = 512
OPROJ_SUB = 256
MLP_TM = 512
MLP_TF = 1024
MLP_TN = 512
CAST_STEPS = 128
HG_GROUP = 4
HG_GW = HG_GROUP * HG_DK

NT_DIMS = (((1,), (1,)), ((), ()))
TN_DIMS = (((0,), (0,)), ((), ()))


def _params(n_axes):
    return pltpu.CompilerParams(dimension_semantics=("arbitrary",) * n_axes,
                                vmem_limit_bytes=V7X_VMEM_LIMIT_BYTES)


def _sigmoid(x):
    return 1.0 / (1.0 + jnp.exp(-x))


def _layer_norm(y, g, b):
    mu = jnp.mean(y, -1, keepdims=True)
    yc = y - mu
    var = jnp.mean(yc * yc, -1, keepdims=True)
    return yc * lax.rsqrt(var + LN_EPS) * g + b


def _mod_kernel(c_ref, w_ref, b_ref, o_ref):
    c = c_ref[...]
    a = (c * _sigmoid(c)).astype(BF16)
    o_ref[0] = jnp.dot(a, w_ref[0].astype(BF16), preferred_element_type=F32) + b_ref[0]


def _modulation(c_rows, w_mod, b_mod):
    depth, d, n = w_mod.shape
    return pl.pallas_call(
        _mod_kernel,
        grid=(depth, n // MOD_TN),
        in_specs=[pl.BlockSpec((MOD_ROWS, d), lambda l, j: (0, 0)),
                  pl.BlockSpec((1, d, MOD_TN), lambda l, j: (l, 0, j)),
                  pl.BlockSpec((1, 1, MOD_TN), lambda l, j: (l, 0, j))],
        out_specs=pl.BlockSpec((1, MOD_ROWS, MOD_TN), lambda l, j: (l, 0, j)),
        out_shape=jax.ShapeDtypeStruct((depth, MOD_ROWS, n), F32),
        compiler_params=_params(2),
        name="modulation",
    )(c_rows, w_mod, b_mod.reshape(depth, 1, n))


def _mod_index(tile_rows, n_ctx_rows, lat_rows):
    n_ctx_tiles = n_ctx_rows // tile_rows
    per_lat = lat_rows // tile_rows

    def idx(i):
        return jnp.where(i < n_ctx_tiles, 0, 1 + (i - n_ctx_tiles) // per_lat)
    return idx


def _pick_rows(x_refs, is_ctx, rows=slice(None)):
    if len(x_refs) == 1:
        return x_refs[0][rows]
    return jnp.where(is_ctx, x_refs[0][rows], x_refs[1][rows])


def _inproj_kernel(*refs, n_x, n_ctx_tiles):
    x_refs = refs[:n_x]
    mod_ref, w_ref, z_ref, k_ref, v_ref, h_ref = refs[n_x:]
    i, j = pl.program_id(0), pl.program_id(1)

    @pl.when(j == 0)
    def _():
        shift, scale = mod_ref[0:1, :], mod_ref[1:2, :]
        h_ref[...] = (_pick_rows(x_refs, i < n_ctx_tiles) * (1.0 + scale) + shift).astype(BF16)

    z_ref[...] = jnp.dot(h_ref[...], w_ref[...], preferred_element_type=F32)

    @pl.when(jnp.logical_and(j == 0, i < n_ctx_tiles))
    def _():
        k_ref[...] = z_ref[:, K_OFF:K_OFF + KV_DIM]
        v_ref[...] = z_ref[:, V_OFF:V_OFF + KV_DIM]


def _token_specs(xs, tile, n_ctx_tiles, n_grid_axes, single_buffer=False):
    def spec(row_of):
        index_map = ((lambda i: (row_of(i), 0)) if n_grid_axes == 1 else (lambda i, j: (row_of(i), 0)))
        if single_buffer:
            return pl.BlockSpec((tile, D_MODEL), index_map, pipeline_mode=pl.Buffered(1))
        return pl.BlockSpec((tile, D_MODEL), index_map)
    if len(xs) == 1:
        return [spec(lambda i: i)]
    return [spec(lambda i: jnp.minimum(i, n_ctx_tiles - 1)), spec(lambda i: jnp.maximum(i - n_ctx_tiles, 0))]


def _inproj(xs, mods, w_in, layer, mod_idx, n_ctx):
    assert V_OFF + KV_DIM <= INPROJ_TN
    n_tok = sum(x.shape[0] for x in xs)
    n_ctx_tiles = n_ctx // TOKEN_TILE
    kv_spec = pl.BlockSpec((TOKEN_TILE, KV_DIM), lambda i, j: (jnp.minimum(i, n_ctx_tiles - 1), 0))
    x_specs = _token_specs(xs, TOKEN_TILE, n_ctx_tiles, 2, single_buffer=len(xs) > 1)
    return pl.pallas_call(
        functools.partial(_inproj_kernel, n_x=len(xs), n_ctx_tiles=n_ctx_tiles),
        grid=(n_tok // TOKEN_TILE, IN_DIM // INPROJ_TN),
        in_specs=x_specs + [
                  pl.BlockSpec((None, None, 6, D_MODEL), lambda i, j: (layer, mod_idx(i), 0, 0)),
                  pl.BlockSpec((D_MODEL, INPROJ_TN), lambda i, j: (0, j))],
        out_specs=[pl.BlockSpec((TOKEN_TILE, INPROJ_TN), lambda i, j: (i, j)), kv_spec, kv_spec],
        out_shape=[jax.ShapeDtypeStruct((n_tok, IN_DIM), F32),
                   jax.ShapeDtypeStruct((n_ctx, KV_DIM), F32),
                   jax.ShapeDtypeStruct((n_ctx, KV_DIM), F32)],
        scratch_shapes=[pltpu.VMEM((TOKEN_TILE, D_MODEL), BF16)],
        compiler_params=_params(2),
        name="inproj",
    )(*xs, mods, w_in)


def _with_ones(v):
    return jnp.concatenate([v, jnp.ones_like(v)], axis=1)


def _sink_attend(scores, values, sink2):
    c = ATT_SCALE * LOG2E
    m_raw = jnp.max(scores[0], -1, keepdims=True)
    for s in scores[1:]:
        m_raw = jnp.maximum(m_raw, jnp.max(s, -1, keepdims=True))
    m2 = jnp.maximum(m_raw * c, sink2)
    acc = None
    for s, v1 in zip(scores, values):
        e = jnp.exp2(s * c - m2).astype(BF16)
        part = jnp.dot(e, v1, preferred_element_type=F32)
        acc = part if acc is None else acc + part
    den = acc[:, HEAD_DIM:HEAD_DIM + 1] + jnp.exp2(sink2 - m2)
    return acc[:, :HEAD_DIM] * (1.0 / den)


def _rms_norm_store(o_scr, sq, g_ref, o_ref):
    ss = jnp.sum(sq, -1, keepdims=True)
    inv = lax.rsqrt(ss * (1.0 / ATT_DIM) + RMS_EPS)
    o_ref[...] = (o_scr[...] * inv * g_ref[...]).astype(o_ref.dtype)


def _head_cols(g, r):
    h = g * Q_PER_KV + r
    return slice(h * HEAD_DIM, (h + 1) * HEAD_DIM)


def _group_sinks(sink_ref, g, rows):
    return jnp.concatenate([jnp.full((rows, 1), sink_ref[g * Q_PER_KV + r] * LOG2E, F32)
                            for r in range(Q_PER_KV)], axis=0)


def _scatter_heads(og, g, rows, sq, o_scr):
    for r in range(Q_PER_KV):
        oh = og[r * rows:(r + 1) * rows]
        sq = sq + oh * oh
        o_scr[:, _head_cols(g, r)] = oh
    return sq


def _ctx_attn_kernel(sink_ref, q_ref, k_ref, v_ref, g_ref, o_ref, o_scr):
    c = ATT_SCALE * LOG2E
    sq = jnp.zeros((q_ref.shape[0], HEAD_DIM), F32)
    for g in range(KV_HEADS):
        kv_cols = slice(g * HEAD_DIM, (g + 1) * HEAD_DIM)
        kg = k_ref[:, kv_cols].astype(BF16)
        vg = v_ref[:, kv_cols].astype(BF16)
        for r in range(Q_PER_KV):
            cols = _head_cols(g, r)
            sink2 = sink_ref[g * Q_PER_KV + r] * LOG2E
            s = lax.dot_general(q_ref[:, cols].astype(BF16), kg, NT_DIMS, preferred_element_type=F32)
            m2 = jnp.maximum(jnp.max(s, -1, keepdims=True) * c, sink2)
            e = jnp.exp2(s * c - m2)
            den = jnp.sum(e, -1, keepdims=True) + jnp.exp2(sink2 - m2)
            p = (e * (1.0 / den)).astype(BF16)
            oh = jnp.dot(p, vg, preferred_element_type=F32)
            sq = sq + oh * oh
            o_scr[:, cols] = oh
    _rms_norm_store(o_scr, sq, g_ref, o_ref)


def _ctx_attention(z, sink, norm_g, n_batch, seq):
    kb, vb = K_OFF // KV_DIM, V_OFF // KV_DIM
    return pl.pallas_call(
        _ctx_attn_kernel,
        grid=(n_batch,),
        in_specs=[pl.BlockSpec(memory_space=pltpu.SMEM),
                  pl.BlockSpec((seq, ATT_DIM), lambda b: (b, 0)),
                  pl.BlockSpec((seq, KV_DIM), lambda b: (b, kb)),
                  pl.BlockSpec((seq, KV_DIM), lambda b: (b, vb)),
                  pl.BlockSpec((1, ATT_DIM), lambda b: (0, 0))],
        out_specs=pl.BlockSpec((seq, ATT_DIM), lambda b: (b, 0)),
        out_shape=jax.ShapeDtypeStruct((n_batch * seq, ATT_DIM), BF16),
        scratch_shapes=[pltpu.VMEM((seq, ATT_DIM), F32)],
        compiler_params=_params(1),
        name="ctx_attention",
    )(sink, z, z, z, norm_g.reshape(1, ATT_DIM))


def _rope(x, cos, sin_lo, sin_hi):
    return (x * cos + pltpu.roll(x, HEAD_DIM - ROPE_FREQS, 1) * sin_lo
            + pltpu.roll(x, ROPE_FREQS, 1) * sin_hi)


def _lat_attn_kernel(sink_ref, q_ref, k_ref, v_ref, ck_ref, cv_ref, cq_ref, slq_ref, shq_ref,
                     ca_ref, sla_ref, sha_ref, g_ref, o_ref, kr_scr, v_scr, ckb_scr, cvb_scr, o_scr):
    n = pl.program_id(1)
    seq = k_ref.shape[0]
    win = 3 * BLOCK

    @pl.when(n == 0)
    def _():
        for g in range(KV_HEADS):
            cols = slice(g * HEAD_DIM, (g + 1) * HEAD_DIM)
            kr_scr[:, cols] = _rope(k_ref[:, cols], ca_ref[...], sla_ref[...], sha_ref[...]).astype(BF16)
            v_scr[g] = _with_ones(v_ref[:, cols].astype(BF16))
            cvb_scr[g] = _with_ones(cv_ref[:, cols].astype(BF16))
        ckb_scr[...] = ck_ref[...].astype(BF16)

    k0 = pl.multiple_of(jnp.clip((n - 1) * BLOCK, 0, seq - win), BLOCK)
    rows = Q_PER_KV * BLOCK
    qpos = n * BLOCK + lax.broadcasted_iota(jnp.int32, (rows, win), 0) % BLOCK
    kpos = k0 + lax.broadcasted_iota(jnp.int32, (rows, win), 1)
    valid = jnp.abs(kpos - qpos) <= WINDOW
    sq = jnp.zeros((BLOCK, HEAD_DIM), F32)
    for g in range(KV_HEADS):
        kv_cols = slice(g * HEAD_DIM, (g + 1) * HEAD_DIM)
        k_loc = kr_scr[pl.ds(k0, win), kv_cols]
        v_loc = v_scr[g, pl.ds(k0, win), :]
        k_ctx = ckb_scr[:, kv_cols]
        v_ctx = cvb_scr[g]
        qg = jnp.concatenate(
            [_rope(q_ref[:, _head_cols(g, r)], cq_ref[...], slq_ref[...], shq_ref[...]).astype(BF16)
             for r in range(Q_PER_KV)], axis=0)
        s_loc = lax.dot_general(qg, k_loc, NT_DIMS, preferred_element_type=F32)
        s_loc = jnp.where(valid, s_loc, NEG_INF / ATT_SCALE)
        s_ctx = lax.dot_general(qg, k_ctx, NT_DIMS, preferred_element_type=F32)
        og = _sink_attend([s_loc, s_ctx], [v_loc, v_ctx], _group_sinks(sink_ref, g, BLOCK))
        sq = _scatter_heads(og, g, BLOCK, sq, o_scr)
    _rms_norm_store(o_scr, sq, g_ref, o_ref)


def _lat_attention(z, cache_k, cache_v, layer, sink, norm_g, rope_tabs, row0, n_batch, seq):
    kb, vb = K_OFF // KV_DIM, V_OFF // KV_DIM
    nq = seq // BLOCK
    qrow0, srow0 = row0 // BLOCK, row0 // seq
    past = cache_k.shape[2]
    cos, sin_lo, sin_hi = rope_tabs
    qtab = pl.BlockSpec((BLOCK, HEAD_DIM), lambda b, n: (n, 0))
    atab = pl.BlockSpec((seq, HEAD_DIM), lambda b, n: (0, 0))
    cache_spec = pl.BlockSpec((None, None, past, KV_DIM), lambda b, n: (b, layer, 0, 0))
    return pl.pallas_call(
        _lat_attn_kernel,
        grid=(n_batch, nq),
        in_specs=[pl.BlockSpec(memory_space=pltpu.SMEM),
                  pl.BlockSpec((BLOCK, ATT_DIM), lambda b, n: (qrow0 + b * nq + n, 0)),
                  pl.BlockSpec((seq, KV_DIM), lambda b, n: (srow0 + b, kb)),
                  pl.BlockSpec((seq, KV_DIM), lambda b, n: (srow0 + b, vb)),
                  cache_spec, cache_spec, qtab, qtab, qtab, atab, atab, atab,
                  pl.BlockSpec((1, ATT_DIM), lambda b, n: (0, 0))],
        out_specs=pl.BlockSpec((BLOCK, ATT_DIM), lambda b, n: (b * nq + n, 0)),
        out_shape=jax.ShapeDtypeStruct((n_batch * seq, ATT_DIM), BF16),
        scratch_shapes=[pltpu.VMEM((seq, KV_DIM), BF16), pltpu.VMEM((KV_HEADS, seq, 2 * HEAD_DIM), BF16),
                        pltpu.VMEM((past, KV_DIM), BF16), pltpu.VMEM((KV_HEADS, past, 2 * HEAD_DIM), BF16),
                        pltpu.VMEM((BLOCK, ATT_DIM), F32)],
        compiler_params=_params(2),
        name="lat_attention",
    )(sink, z, z, z, cache_k, cache_v, cos, sin_lo, sin_hi, cos, sin_lo, sin_hi,
      norm_g.reshape(1, ATT_DIM))


def _rope_tables(seq):
    rows = seq // GRID_W
    row = jnp.repeat(jnp.arange(rows, dtype=F32), GRID_W)
    col = jnp.tile(jnp.arange(GRID_W, dtype=F32), rows)
    inv = ROPE_THETA ** (-jnp.arange(ROPE_FREQS, dtype=F32) / ROPE_FREQS)
    ang_r, ang_c = row[:, None] * inv, col[:, None] * inv
    cr, sr, cc, sc = jnp.cos(ang_r), jnp.sin(ang_r), jnp.cos(ang_c), jnp.sin(ang_c)
    zero = jnp.zeros_like(sr)
    cos = jnp.concatenate([cr, cr, cc, cc], -1)
    sin_lo = jnp.concatenate([-sr, zero, -sc, zero], -1)
    sin_hi = jnp.concatenate([zero, sr, zero, sc], -1)
    return cos, sin_lo, sin_hi


def _chunk_cumsum(x, rows, reverse):
    s = 1
    while s < HG_CHUNK:
        if reverse:
            x = x + jnp.where(rows < HG_CHUNK - s, pltpu.roll(x, HG_CHUNK - s, 0), 0.0)
        else:
            x = x + jnp.where(rows >= s, pltpu.roll(x, s, 0), 0.0)
        s *= 2
    return x


def _hgrn_kernel(*refs, seq, has_init, want_state):
    hq_ref, hff_ref, hfb_ref, hi_ref, hgt_ref, lbf_ref, lbb_ref, ng_ref = refs[:8]
    pos = 8
    if has_init:
        sf0_ref, sb0_ref = refs[pos:pos + 2]
        pos += 2
    o_ref = refs[pos]
    pos += 1
    if want_state:
        sf_ref, sb_ref = refs[pos:pos + 2]
        pos += 2
    q_scr, of_scr, ob_scr, st_scr = refs[pos:pos + 4]

    C = HG_CHUNK
    n_chunks = seq // C
    q_in = hq_ref[...]
    q_scr[...] = q_in * _sigmoid(q_in)
    for h in range(HG_GROUP):
        if has_init:
            st_scr[h] = sf0_ref[h].T
            st_scr[HG_GROUP + h] = sb0_ref[h].T
        else:
            st_scr[h] = jnp.zeros((HG_DV, HG_DK), F32)
            st_scr[HG_GROUP + h] = jnp.zeros((HG_DV, HG_DK), F32)

    rows = lax.broadcasted_iota(jnp.int32, (C, HG_DK), 0)
    ti = lax.broadcasted_iota(jnp.int32, (C, C), 0)
    si = lax.broadcasted_iota(jnp.int32, (C, C), 1)
    dirs = ((hff_ref, lbf_ref, of_scr, False, si <= ti, C // 2 - 1, C - 1),
            (hfb_ref, lbb_ref, ob_scr, True, si >= ti, C // 2, 0))

    def chunk_step(c, carry):
        starts = (pl.multiple_of(c * C, C), pl.multiple_of((n_chunks - 1 - c) * C, C))
        streams = [(h, d) for h in range(HG_GROUP) for d in range(2)]
        prep = []
        for h, d in streams:
            z_ref, lb_ref, _, reverse, _, mid, last = dirs[d]
            cols = slice(h * HG_DK, (h + 1) * HG_DK)
            rs = pl.ds(starts[d], C)
            q = q_scr[rs, cols]
            v = hi_ref[rs, cols].astype(BF16)
            lb = lb_ref[:, cols]
            sig = _sigmoid(z_ref[rs, cols])
            log_f = jnp.log(lb + (1.0 - lb) * sig)
            k = (1.0 - lb) * (1.0 - sig)
            b = _chunk_cumsum(log_f, rows, reverse)
            b_mid = b[mid:mid + 1, :]
            b_last = b[last:last + 1, :]
            e_mid = jnp.exp(b - b_mid)
            qe = q * e_mid
            ke = k * (1.0 / e_mid)
            prep.append(dict(
                v=v,
                qe=qe.astype(BF16),
                ke=ke.astype(BF16),
                qb=(qe * jnp.exp(b_mid)).astype(BF16),
                kl=(ke * jnp.exp(b_last - b_mid)).astype(BF16),
                decay=jnp.exp(b_last)))
        a_raw = [lax.dot_general(p["qe"], p["ke"], NT_DIMS, preferred_element_type=F32) for p in prep]
        states = [st_scr[d * HG_GROUP + h] for h, d in streams]
        o_inter = [lax.dot_general(p["qb"], st.astype(BF16), NT_DIMS, preferred_element_type=F32)
                   for p, st in zip(prep, states)]
        u_t = [lax.dot_general(p["v"], p["kl"], TN_DIMS, preferred_element_type=F32) for p in prep]
        o_intra = [jnp.dot(jnp.where(dirs[d][4], a, 0.0).astype(BF16), p["v"], preferred_element_type=F32)
                   for (h, d), a, p in zip(streams, a_raw, prep)]
        for i, (h, d) in enumerate(streams):
            cols = slice(h * HG_DK, (h + 1) * HG_DK)
            dirs[d][2][pl.ds(starts[d], C), cols] = o_intra[i] + o_inter[i]
            st_scr[d * HG_GROUP + h] = states[i] * prep[i]["decay"] + u_t[i]
        return carry

    lax.fori_loop(0, n_chunks, chunk_step, 0, unroll=4)

    for h in range(HG_GROUP):
        cols = slice(h * HG_DK, (h + 1) * HG_DK)
        o = of_scr[:, cols] + ob_scr[:, cols]
        o = o * lax.rsqrt(jnp.mean(o * o, -1, keepdims=True) + RMS_EPS) * ng_ref[...]
        gt = hgt_ref[:, cols]
        o_ref[:, cols] = (o * (gt * _sigmoid(gt))).astype(o_ref.dtype)
        if want_state:
            sf_ref[h] = st_scr[h].T
            sb_ref[h] = st_scr[HG_GROUP + h].T


def _hgrn(z, lb_f, lb_b, norm_g, row0, n_batch, seq, init_states=None, layer=None, want_state=False):
    n_groups = HG_HEADS // HG_GROUP
    r0 = row0 // seq
    has_init = init_states is not None

    def zspec(off):
        return pl.BlockSpec((seq, HG_GW), lambda b, g, o=off // HG_GW: (r0 + b, o + g))

    lbspec = pl.BlockSpec((1, HG_GW), lambda b, g: (0, g))
    in_specs = [zspec(HQ_OFF), zspec(HFF_OFF), zspec(HFB_OFF), zspec(HI_OFF), zspec(HGT_OFF),
                lbspec, lbspec, pl.BlockSpec((1, HG_DV), lambda b, g: (0, 0))]
    args = [z, z, z, z, z, lb_f.reshape(1, HG_DIM), lb_b.reshape(1, HG_DIM), norm_g.reshape(1, HG_DV)]
    if has_init:
        st_spec = pl.BlockSpec((None, None, HG_GROUP, HG_DK, HG_DV), lambda b, g: (b, layer, g, 0, 0))
        in_specs += [st_spec, st_spec]
        args += list(init_states)
    out_specs = [pl.BlockSpec((seq, HG_GW), lambda b, g: (b, g))]
    out_shape = [jax.ShapeDtypeStruct((n_batch * seq, HG_VDIM), BF16)]
    if want_state:
        so = pl.BlockSpec((None, HG_GROUP, HG_DK, HG_DV), lambda b, g: (b, g, 0, 0))
        out_specs += [so, so]
        out_shape += [jax.ShapeDtypeStruct((n_batch, HG_HEADS, HG_DK, HG_DV), F32)] * 2
    return pl.pallas_call(
        functools.partial(_hgrn_kernel, seq=seq, has_init=has_init, want_state=want_state),
        grid=(n_batch, n_groups),
        in_specs=in_specs,
        out_specs=out_specs,
        out_shape=out_shape,
        scratch_shapes=[pltpu.VMEM((seq, HG_GW), F32), pltpu.VMEM((seq, HG_GW), F32),
                        pltpu.VMEM((seq, HG_GW), F32),
                        pltpu.VMEM((2 * HG_GROUP, HG_DV, HG_DK), F32)],
        compiler_params=_params(2),
        name=f"hgrn_t{seq}",
    )(*args)


def _oproj_kernel(*refs, n_x, n_ctx_tiles):
    attc_ref, attl_ref, hgc_ref, hgl_ref, w_ref = refs[:5]
    x_refs = refs[5:5 + n_x]
    mod_ref, g_ref, b_ref, o_ref = refs[5 + n_x:]
    is_ctx = pl.program_id(0) < n_ctx_tiles
    for r0 in range(0, OPROJ_TM, OPROJ_SUB):
        rows = slice(r0, r0 + OPROJ_SUB)
        att = jnp.where(is_ctx, attc_ref[rows], attl_ref[rows])
        hg = jnp.where(is_ctx, hgc_ref[rows], hgl_ref[rows])
        mix = (jnp.dot(att, w_ref[0:ATT_DIM, :], preferred_element_type=F32)
               + jnp.dot(hg, w_ref[ATT_DIM:MIX_DIM, :], preferred_element_type=F32))
        y = DEEPNORM_ALPHA * _pick_rows(x_refs, is_ctx, rows) + mod_ref[2:3, :] * mix
        o_ref[rows] = _layer_norm(y, g_ref[0:1, :], b_ref[0:1, :])


def _oproj(att_c, att_l, hg_c, hg_l, w_o, xs, mods, ln_g, ln_b, layer, mod_idx):
    n_tok = sum(x.shape[0] for x in xs)
    n_ctx_tiles = att_c.shape[0] // OPROJ_TM
    row = lambda i: (i, 0)
    ctx_row = lambda i: (jnp.minimum(i, n_ctx_tiles - 1), 0)
    lat_row = lambda i: (jnp.maximum(i - n_ctx_tiles, 0), 0)
    return pl.pallas_call(
        functools.partial(_oproj_kernel, n_x=len(xs), n_ctx_tiles=n_ctx_tiles),
        grid=(n_tok // OPROJ_TM,),
        in_specs=[pl.BlockSpec((OPROJ_TM, ATT_DIM), ctx_row),
                  pl.BlockSpec((OPROJ_TM, ATT_DIM), lat_row),
                  pl.BlockSpec((OPROJ_TM, HG_VDIM), ctx_row),
                  pl.BlockSpec((OPROJ_TM, HG_VDIM), lat_row),
                  pl.BlockSpec((MIX_DIM, D_MODEL), lambda i: (0, 0))]
        + _token_specs(xs, OPROJ_TM, n_ctx_tiles, 1) + [
                  pl.BlockSpec((None, None, 6, D_MODEL), lambda i: (layer, mod_idx(i), 0, 0)),
                  pl.BlockSpec((None, 2, D_MODEL), lambda i: (layer, 0, 0)),
                  pl.BlockSpec((None, 2, D_MODEL), lambda i: (layer, 0, 0))],
        out_specs=pl.BlockSpec((OPROJ_TM, D_MODEL), row),
        out_shape=jax.ShapeDtypeStruct((n_tok, D_MODEL), F32),
        compiler_params=_params(1),
        name="oproj_ln",
    )(att_c, att_l, hg_c, hg_l, w_o, *xs, mods, ln_g, ln_b)


def _mlp_kernel(*refs, n_cast, n_out, n_ctx_tiles):
    x_ref, mod_ref, wu_ref, wd_ref, g_ref, b_ref = refs[:6]
    cast_in = refs[6:6 + n_cast]
    outs = refs[6 + n_cast:6 + n_cast + n_out]
    cast_out = refs[6 + n_cast + n_out:6 + 2 * n_cast + n_out]
    h_ref, acc_ref = refs[6 + 2 * n_cast + n_out:]
    i, f = pl.program_id(0), pl.program_id(1)
    n_f = pl.num_programs(1)

    @pl.when(f == 0)
    def _():
        shift, scale = mod_ref[3:4, :], mod_ref[4:5, :]
        h_ref[...] = (x_ref[...] * (1.0 + scale) + shift).astype(BF16)
        acc_ref[...] = jnp.zeros_like(acc_ref)

    if n_cast:
        @pl.when(i * n_f + f < CAST_STEPS)
        def _():
            for src, dst in zip(cast_in, cast_out):
                dst[...] = src[...].astype(BF16)

    u = jnp.dot(h_ref[...], wu_ref[...], preferred_element_type=F32)
    u = jnp.square(jnp.maximum(u, 0.0)).astype(BF16)
    for n in range(0, D_MODEL, MLP_TN):
        acc_ref[:, n:n + MLP_TN] += jnp.dot(u, wd_ref[:, n:n + MLP_TN], preferred_element_type=F32)

    def finish(o_ref):
        y = DEEPNORM_ALPHA * x_ref[...] + mod_ref[5:6, :] * acc_ref[...]
        o_ref[...] = _layer_norm(y, g_ref[1:2, :], b_ref[1:2, :])

    last = f == n_f - 1
    if n_out == 1:
        pl.when(last)(lambda: finish(outs[0]))
    else:
        pl.when(jnp.logical_and(last, i < n_ctx_tiles))(lambda: finish(outs[0]))
        pl.when(jnp.logical_and(last, i >= n_ctx_tiles))(lambda: finish(outs[1]))


def _cast_specs(weights, layer):
    n_f = D_FF // MLP_TF
    in_specs, out_specs, out_shape = [], [], []
    for w in weights:
        _, rows, cols = w.shape
        slab = rows // CAST_STEPS
        in_specs.append(pl.BlockSpec(
            (None, slab, cols), lambda i, f: (layer, jnp.minimum(i * n_f + f, CAST_STEPS - 1), 0)))
        out_specs.append(pl.BlockSpec(
            (slab, cols), lambda i, f: (jnp.minimum(i * n_f + f, CAST_STEPS - 1), 0)))
        out_shape.append(jax.ShapeDtypeStruct((rows, cols), BF16))
    return in_specs, out_specs, out_shape


def _mlp(x, mods, w_up, w_down, ln_g, ln_b, layer, mod_idx, next_weights=(), split_rows=None):
    n_tok = x.shape[0]
    n_f = D_FF // MLP_TF
    assert (n_tok // MLP_TM) * n_f >= CAST_STEPS
    cast_in_specs, cast_out_specs, cast_shapes = _cast_specs(next_weights, layer + 1)
    if split_rows is None:
        n_ctx_tiles = n_tok // MLP_TM
        y_specs = [pl.BlockSpec((MLP_TM, D_MODEL), lambda i, f: (i, 0))]
        y_shapes = [jax.ShapeDtypeStruct((n_tok, D_MODEL), F32)]
    else:
        n_ctx_tiles = split_rows // MLP_TM
        y_specs = [pl.BlockSpec((MLP_TM, D_MODEL), lambda i, f: (jnp.minimum(i, n_ctx_tiles - 1), 0)),
                   pl.BlockSpec((MLP_TM, D_MODEL), lambda i, f: (jnp.maximum(i - n_ctx_tiles, 0), 0))]
        y_shapes = [jax.ShapeDtypeStruct((split_rows, D_MODEL), F32),
                    jax.ShapeDtypeStruct((n_tok - split_rows, D_MODEL), F32)]
    return pl.pallas_call(
        functools.partial(_mlp_kernel, n_cast=len(next_weights), n_out=len(y_specs),
                          n_ctx_tiles=n_ctx_tiles),
        grid=(n_tok // MLP_TM, n_f),
        in_specs=[pl.BlockSpec((MLP_TM, D_MODEL), lambda i, f: (i, 0)),
                  pl.BlockSpec((None, None, 6, D_MODEL), lambda i, f: (layer, mod_idx(i), 0, 0)),
                  pl.BlockSpec((D_MODEL, MLP_TF), lambda i, f: (0, f)),
                  pl.BlockSpec((MLP_TF, D_MODEL), lambda i, f: (f, 0)),
                  pl.BlockSpec((None, 2, D_MODEL), lambda i, f: (layer, 0, 0)),
                  pl.BlockSpec((None, 2, D_MODEL), lambda i, f: (layer, 0, 0))] + cast_in_specs,
        out_specs=y_specs + cast_out_specs,
        out_shape=y_shapes + cast_shapes,
        scratch_shapes=[pltpu.VMEM((MLP_TM, D_MODEL), BF16), pltpu.VMEM((MLP_TM, D_MODEL), F32)],
        compiler_params=_params(2),
        name="mlp_ln",
    )(x, mods, w_up, w_down, ln_g, ln_b, *next_weights)


def _lower_bounds(lb_logits):
    p = jax.nn.softmax(lb_logits.astype(F32), axis=0)
    cs = jnp.cumsum(p, axis=0)
    return cs - cs[0:1]


def kernel(x_prompt, x_sample, cache_k, cache_v, state_hgrn_fwd, state_hgrn_bwd, c, c_ctx, w_mod, b_mod,
           w_in, attn_sink, attn_norm_g, hg_lb_logits, hg_norm_g, w_o, ln_g, ln_b, w_up, w_down):
    bp, seq_p, d = x_prompt.shape
    bs, seq_s, _ = x_sample.shape
    depth = w_in.shape[0]
    past = cache_k.shape[2]
    n_ctx, n_lat = bp * seq_p, bs * seq_s

    xs = (x_prompt.reshape(n_ctx, d), x_sample.reshape(n_lat, d))
    c_rows = jnp.concatenate([c_ctx[None, :], c, jnp.zeros((MOD_ROWS - 1 - bs, d), F32)], 0)
    mods = _modulation(c_rows, w_mod, b_mod)[:, :1 + bs].reshape(depth, 1 + bs, 6, d)

    weights = (w_in, w_o, w_up, w_down)
    w_in_b, w_o_b, w_up_b, w_down_b = (w[0].astype(BF16) for w in weights)
    lb_f, lb_b = _lower_bounds(hg_lb_logits[0]), _lower_bounds(hg_lb_logits[1])
    rope_tabs = _rope_tables(seq_s)
    ck = cache_k.reshape(bs, depth, past, KV_DIM)
    cv = cache_v.reshape(bs, depth, past, KV_DIM)

    new_k, new_v, new_sf, new_sb = [], [], [], []
    for l in range(depth):
        z, k_c, v_c = _inproj(xs, mods, w_in_b, l, _mod_index(TOKEN_TILE, n_ctx, seq_s), n_ctx)
        new_k.append(k_c.reshape(bp, seq_p, KV_DIM))
        new_v.append(v_c.reshape(bp, seq_p, KV_DIM))

        att_c = _ctx_attention(z, attn_sink[l], attn_norm_g[l], bp, seq_p)
        att_l = _lat_attention(z, ck, cv, l, attn_sink[l], attn_norm_g[l], rope_tabs, n_ctx, bs, seq_s)

        hg_c, s_f, s_b = _hgrn(z, lb_f[l], lb_b[l], hg_norm_g[l], 0, bp, seq_p, want_state=True)
        (hg_l,) = _hgrn(z, lb_f[l], lb_b[l], hg_norm_g[l], n_ctx, bs, seq_s,
                        init_states=(state_hgrn_fwd, state_hgrn_bwd), layer=l)
        new_sf.append(s_f)
        new_sb.append(s_b)

        x = _oproj(att_c, att_l, hg_c, hg_l, w_o_b, xs, mods, ln_g, ln_b, l,
                   _mod_index(OPROJ_TM, n_ctx, seq_s))
        mlp_mods = _mod_index(MLP_TM, n_ctx, seq_s)
        if l + 1 < depth:
            x, w_in_b, w_o_b, w_up_b, w_down_b = _mlp(x, mods, w_up_b, w_down_b, ln_g, ln_b, l, mlp_mods,
                                                      next_weights=weights)
            xs = (x,)
        else:
            y_p, y_s = _mlp(x, mods, w_up_b, w_down_b, ln_g, ln_b, l, mlp_mods, split_rows=n_ctx)

    cache_shape = (bp, depth, seq_p, KV_HEADS, HEAD_DIM)
    return (y_p.reshape(bp, seq_p, d), y_s.reshape(bs, seq_s, d),
            jnp.stack(new_k, 1).reshape(cache_shape), jnp.stack(new_v, 1).reshape(cache_shape),
            jnp.stack(new_sf, 1), jnp.stack(new_sb, 1))
```

```python
import functools

import jax
import jax.numpy as jnp
from jax import lax
from jax.experimental import pallas as pl
from jax.experimental.pallas import tpu as pltpu

F32 = jnp.float32
BF16 = jnp.bfloat16

D_MODEL = 2048
DEPTH = 4
GRID_W = 64
HEAD_DIM = 128
ATT_HEADS = 8
KV_HEADS = 2
Q_PER_KV = ATT_HEADS // KV_HEADS
ATT_DIM = ATT_HEADS * HEAD_DIM
KV_DIM = KV_HEADS * HEAD_DIM
WINDOW = 128
BLOCK = 128
ATT_SCALE = HEAD_DIM ** -0.5
ROPE_THETA = 10000.0
ROPE_FREQS = HEAD_DIM // 4
HG_HEADS = 8
HG_DK = 128
HG_DV = 128
HG_DIM = HG_HEADS * HG_DK
HG_VDIM = HG_HEADS * HG_DV
HG_CHUNK = 32
MIX_DIM = ATT_DIM + HG_VDIM
IN_DIM = ATT_DIM + 2 * KV_DIM + 3 * HG_DIM + 2 * HG_VDIM
D_FF = 4 * D_MODEL
LN_EPS = 1e-5
RMS_EPS = 1e-6
DEEPNORM_ALPHA = (2 * DEPTH) ** 0.25
NEG_INF = -1e30
LOG2E = 1.4426950408889634

Q_OFF = 0
K_OFF = ATT_DIM
V_OFF = K_OFF + KV_DIM
HQ_OFF = V_OFF + KV_DIM
HFF_OFF = HQ_OFF + HG_DIM
HFB_OFF = HFF_OFF + HG_DIM
HI_OFF = HFB_OFF + HG_DIM
HGT_OFF = HI_OFF + HG_VDIM

V7X_VMEM_LIMIT_BYTES = 56 * 1024 * 1024

MOD_ROWS = 8
MOD_TN = 768
TOKEN_TILE = 1024
INPROJ_TN = 1664
OPROJ_TM = 512
OPROJ_SUB = 256
MLP_TM = 512
MLP_TF = 1024
MLP_TN = 512
CAST_STEPS = 128
HG_GROUP = 4
HG_GW = HG_GROUP * HG_DK

NT_DIMS = (((1,), (1,)), ((), ()))
TN_DIMS = (((0,), (0,)), ((), ()))


def _params(n_axes):
    return pltpu.CompilerParams(dimension_semantics=("arbitrary",) * n_axes,
                                vmem_limit_bytes=V7X_VMEM_LIMIT_BYTES)


def _sigmoid(x):
    return 1.0 / (1.0 + jnp.exp(-x))


def _layer_norm(y, g, b):
    mu = jnp.mean(y, -1, keepdims=True)
    yc = y - mu
    var = jnp.mean(yc * yc, -1, keepdims=True)
    return yc * lax.rsqrt(var + LN_EPS) * g + b


def _mod_kernel(c_ref, w_ref, b_ref, o_ref):
    c = c_ref[...]
    a = (c * _sigmoid(c)).astype(BF16)
    o_ref[0] = jnp.dot(a, w_ref[0].astype(BF16), preferred_element_type=F32) + b_ref[0]


def _modulation(c_rows, w_mod, b_mod):
    depth, d, n = w_mod.shape
    return pl.pallas_call(
        _mod_kernel,
        grid=(depth, n // MOD_TN),
        in_specs=[pl.BlockSpec((MOD_ROWS, d), lambda l, j: (0, 0)),
                  pl.BlockSpec((1, d, MOD_TN), lambda l, j: (l, 0, j)),
                  pl.BlockSpec((1, 1, MOD_TN), lambda l, j: (l, 0, j))],
        out_specs=pl.BlockSpec((1, MOD_ROWS, MOD_TN), lambda l, j: (l, 0, j)),
        out_shape=jax.ShapeDtypeStruct((depth, MOD_ROWS, n), F32),
        compiler_params=_params(2),
        name="modulation",
    )(c_rows, w_mod, b_mod.reshape(depth, 1, n))


def _mod_index(tile_rows, n_ctx_rows, lat_rows):
    n_ctx_tiles = n_ctx_rows // tile_rows
    per_lat = lat_rows // tile_rows

    def idx(i):
        return jnp.where(i < n_ctx_tiles, 0, 1 + (i - n_ctx_tiles) // per_lat)
    return idx


def _pick_rows(x_refs, is_ctx, rows=slice(None)):
    if len(x_refs) == 1:
        return x_refs[0][rows]
    return jnp.where(is_ctx, x_refs[0][rows], x_refs[1][rows])


def _inproj_kernel(*refs, n_x, n_ctx_tiles):
    x_refs = refs[:n_x]
    mod_ref, w_ref, z_ref, k_ref, v_ref, h_ref = refs[n_x:]
    i, j = pl.program_id(0), pl.program_id(1)

    @pl.when(j == 0)
    def _():
        shift, scale = mod_ref[0:1, :], mod_ref[1:2, :]
        h_ref[...] = (_pick_rows(x_refs, i < n_ctx_tiles) * (1.0 + scale) + shift).astype(BF16)

    z_ref[...] = jnp.dot(h_ref[...], w_ref[...], preferred_element_type=F32)

    @pl.when(jnp.logical_and(j == 0, i < n_ctx_tiles))
    def _():
        k_ref[...] = z_ref[:, K_OFF:K_OFF + KV_DIM]
        v_ref[...] = z_ref[:, V_OFF:V_OFF + KV_DIM]


def _token_specs(xs, tile, n_ctx_tiles, n_grid_axes):
    def spec(row_of):
        index_map = ((lambda i: (row_of(i), 0)) if n_grid_axes == 1 else (lambda i, j: (row_of(i), 0)))
        return pl.BlockSpec((tile, D_MODEL), index_map)
    if len(xs) == 1:
        return [spec(lambda i: i)]
    return [spec(lambda i: jnp.minimum(i, n_ctx_tiles - 1)), spec(lambda i: jnp.maximum(i - n_ctx_tiles, 0))]


def _inproj_tile(xs):
    return TOKEN_TILE // len(xs)


def _inproj(xs, mods, w_in, layer, mod_idx, n_ctx):
    assert V_OFF + KV_DIM <= INPROJ_TN
    tile = _inproj_tile(xs)
    n_tok = sum(x.shape[0] for x in xs)
    n_ctx_tiles = n_ctx // tile
    kv_spec = pl.BlockSpec((tile, KV_DIM), lambda i, j: (jnp.minimum(i, n_ctx_tiles - 1), 0))
    return pl.pallas_call(
        functools.partial(_inproj_kernel, n_x=len(xs), n_ctx_tiles=n_ctx_tiles),
        grid=(n_tok // tile, IN_DIM // INPROJ_TN),
        in_specs=_token_specs(xs, tile, n_ctx_tiles, 2) + [
                  pl.BlockSpec((None, None, 6, D_MODEL), lambda i, j: (layer, mod_idx(i), 0, 0)),
                  pl.BlockSpec((D_MODEL, INPROJ_TN), lambda i, j: (0, j))],
        out_specs=[pl.BlockSpec((tile, INPROJ_TN), lambda i, j: (i, j)), kv_spec, kv_spec],
        out_shape=[jax.ShapeDtypeStruct((n_tok, IN_DIM), F32),
                   jax.ShapeDtypeStruct((n_ctx, KV_DIM), F32),
                   jax.ShapeDtypeStruct((n_ctx, KV_DIM), F32)],
        scratch_shapes=[pltpu.VMEM((tile, D_MODEL), BF16)],
        compiler_params=_params(2),
        name="inproj",
    )(*xs, mods, w_in)


def _with_ones(v):
    return jnp.concatenate([v, jnp.ones_like(v)], axis=1)


def _sink_attend(scores, values, sink2):
    c = ATT_SCALE * LOG2E
    m_raw = jnp.max(scores[0], -1, keepdims=True)
    for s in scores[1:]:
        m_raw = jnp.maximum(m_raw, jnp.max(s, -1, keepdims=True))
    m2 = jnp.maximum(m_raw * c, sink2)
    acc = None
    for s, v1 in zip(scores, values):
        e = jnp.exp2(s * c - m2).astype(BF16)
        part = jnp.dot(e, v1, preferred_element_type=F32)
        acc = part if acc is None else acc + part
    den = acc[:, HEAD_DIM:HEAD_DIM + 1] + jnp.exp2(sink2 - m2)
    return acc[:, :HEAD_DIM] * (1.0 / den)


def _rms_norm_store(o_scr, sq, g_ref, o_ref):
    ss = jnp.sum(sq, -1, keepdims=True)
    inv = lax.rsqrt(ss * (1.0 / ATT_DIM) + RMS_EPS)
    o_ref[...] = (o_scr[...] * inv * g_ref[...]).astype(o_ref.dtype)


def _head_cols(g, r):
    h = g * Q_PER_KV + r
    return slice(h * HEAD_DIM, (h + 1) * HEAD_DIM)


def _group_sinks(sink_ref, g, rows):
    return jnp.concatenate([jnp.full((rows, 1), sink_ref[g * Q_PER_KV + r] * LOG2E, F32)
                            for r in range(Q_PER_KV)], axis=0)


def _scatter_heads(og, g, rows, sq, o_scr):
    for r in range(Q_PER_KV):
        oh = og[r * rows:(r + 1) * rows]
        sq = sq + oh * oh
        o_scr[:, _head_cols(g, r)] = oh
    return sq


def _ctx_attn_kernel(sink_ref, q_ref, k_ref, v_ref, g_ref, o_ref, o_scr):
    c = ATT_SCALE * LOG2E
    sq = jnp.zeros((q_ref.shape[0], HEAD_DIM), F32)
    for g in range(KV_HEADS):
        kv_cols = slice(g * HEAD_DIM, (g + 1) * HEAD_DIM)
        kg = k_ref[:, kv_cols].astype(BF16)
        vg = v_ref[:, kv_cols].astype(BF16)
        for r in range(Q_PER_KV):
            cols = _head_cols(g, r)
            sink2 = sink_ref[g * Q_PER_KV + r] * LOG2E
            s = lax.dot_general(q_ref[:, cols].astype(BF16), kg, NT_DIMS, preferred_element_type=F32)
            m2 = jnp.maximum(jnp.max(s, -1, keepdims=True) * c, sink2)
            e = jnp.exp2(s * c - m2)
            den = jnp.sum(e, -1, keepdims=True) + jnp.exp2(sink2 - m2)
            p = (e * (1.0 / den)).astype(BF16)
            oh = jnp.dot(p, vg, preferred_element_type=F32)
            sq = sq + oh * oh
            o_scr[:, cols] = oh
    _rms_norm_store(o_scr, sq, g_ref, o_ref)


def _ctx_attention(z, sink, norm_g, n_batch, seq):
    kb, vb = K_OFF // KV_DIM, V_OFF // KV_DIM
    return pl.pallas_call(
        _ctx_attn_kernel,
        grid=(n_batch,),
        in_specs=[pl.BlockSpec(memory_space=pltpu.SMEM),
                  pl.BlockSpec((seq, ATT_DIM), lambda b: (b, 0)),
                  pl.BlockSpec((seq, KV_DIM), lambda b: (b, kb)),
                  pl.BlockSpec((seq, KV_DIM), lambda b: (b, vb)),
                  pl.BlockSpec((1, ATT_DIM), lambda b: (0, 0))],
        out_specs=pl.BlockSpec((seq, ATT_DIM), lambda b: (b, 0)),
        out_shape=jax.ShapeDtypeStruct((n_batch * seq, ATT_DIM), BF16),
        scratch_shapes=[pltpu.VMEM((seq, ATT_DIM), F32)],
        compiler_params=_params(1),
        name="ctx_attention",
    )(sink, z, z, z, norm_g.reshape(1, ATT_DIM))


def _rope(x, cos, sin_lo, sin_hi):
    return (x * cos + pltpu.roll(x, HEAD_DIM - ROPE_FREQS, 1) * sin_lo
            + pltpu.roll(x, ROPE_FREQS, 1) * sin_hi)


def _lat_attn_kernel(sink_ref, q_ref, k_ref, v_ref, ck_ref, cv_ref, cq_ref, slq_ref, shq_ref,
                     ca_ref, sla_ref, sha_ref, g_ref, o_ref, kr_scr, v_scr, ckb_scr, cvb_scr, o_scr):
    n = pl.program_id(1)
    seq = k_ref.shape[0]
    win = 3 * BLOCK

    @pl.when(n == 0)
    def _():
        for g in range(KV_HEADS):
            cols = slice(g * HEAD_DIM, (g + 1) * HEAD_DIM)
            kr_scr[:, cols] = _rope(k_ref[:, cols], ca_ref[...], sla_ref[...], sha_ref[...]).astype(BF16)
            v_scr[g] = _with_ones(v_ref[:, cols].astype(BF16))
            cvb_scr[g] = _with_ones(cv_ref[:, cols].astype(BF16))
        ckb_scr[...] = ck_ref[...].astype(BF16)

    k0 = pl.multiple_of(jnp.clip((n - 1) * BLOCK, 0, seq - win), BLOCK)
    rows = Q_PER_KV * BLOCK
    qpos = n * BLOCK + lax.broadcasted_iota(jnp.int32, (rows, win), 0) % BLOCK
    kpos = k0 + lax.broadcasted_iota(jnp.int32, (rows, win), 1)
    valid = jnp.abs(kpos - qpos) <= WINDOW
    sq = jnp.zeros((BLOCK, HEAD_DIM), F32)
    for g in range(KV_HEADS):
        kv_cols = slice(g * HEAD_DIM, (g + 1) * HEAD_DIM)
        k_loc = kr_scr[pl.ds(k0, win), kv_cols]
        v_loc = v_scr[g, pl.ds(k0, win), :]
        k_ctx = ckb_scr[:, kv_cols]
        v_ctx = cvb_scr[g]
        qg = jnp.concatenate(
            [_rope(q_ref[:, _head_cols(g, r)], cq_ref[...], slq_ref[...], shq_ref[...]).astype(BF16)
             for r in range(Q_PER_KV)], axis=0)
        s_loc = lax.dot_general(qg, k_loc, NT_DIMS, preferred_element_type=F32)
        s_loc = jnp.where(valid, s_loc, NEG_INF / ATT_SCALE)
        s_ctx = lax.dot_general(qg, k_ctx, NT_DIMS, preferred_element_type=F32)
        og = _sink_attend([s_loc, s_ctx], [v_loc, v_ctx], _group_sinks(sink_ref, g, BLOCK))
        sq = _scatter_heads(og, g, BLOCK, sq, o_scr)
    _rms_norm_store(o_scr, sq, g_ref, o_ref)


def _lat_attention(z, cache_k, cache_v, layer, sink, norm_g, rope_tabs, row0, n_batch, seq):
    kb, vb = K_OFF // KV_DIM, V_OFF // KV_DIM
    nq = seq // BLOCK
    qrow0, srow0 = row0 // BLOCK, row0 // seq
    past = cache_k.shape[2]
    cos, sin_lo, sin_hi = rope_tabs
    qtab = pl.BlockSpec((BLOCK, HEAD_DIM), lambda b, n: (n, 0))
    atab = pl.BlockSpec((seq, HEAD_DIM), lambda b, n: (0, 0))
    cache_spec = pl.BlockSpec((None, None, past, KV_DIM), lambda b, n: (b, layer, 0, 0))
    return pl.pallas_call(
        _lat_attn_kernel,
        grid=(n_batch, nq),
        in_specs=[pl.BlockSpec(memory_space=pltpu.SMEM),
                  pl.BlockSpec((BLOCK, ATT_DIM), lambda b, n: (qrow0 + b * nq + n, 0)),
                  pl.BlockSpec((seq, KV_DIM), lambda b, n: (srow0 + b, kb)),
                  pl.BlockSpec((seq, KV_DIM), lambda b, n: (srow0 + b, vb)),
                  cache_spec, cache_spec, qtab, qtab, qtab, atab, atab, atab,
                  pl.BlockSpec((1, ATT_DIM), lambda b, n: (0, 0))],
        out_specs=pl.BlockSpec((BLOCK, ATT_DIM), lambda b, n: (b * nq + n, 0)),
        out_shape=jax.ShapeDtypeStruct((n_batch * seq, ATT_DIM), BF16),
        scratch_shapes=[pltpu.VMEM((seq, KV_DIM), BF16), pltpu.VMEM((KV_HEADS, seq, 2 * HEAD_DIM), BF16),
                        pltpu.VMEM((past, KV_DIM), BF16), pltpu.VMEM((KV_HEADS, past, 2 * HEAD_DIM), BF16),
                        pltpu.VMEM((BLOCK, ATT_DIM), F32)],
        compiler_params=_params(2),
        name="lat_attention",
    )(sink, z, z, z, cache_k, cache_v, cos, sin_lo, sin_hi, cos, sin_lo, sin_hi,
      norm_g.reshape(1, ATT_DIM))


def _rope_tables(seq):
    rows = seq // GRID_W
    row = jnp.repeat(jnp.arange(rows, dtype=F32), GRID_W)
    col = jnp.tile(jnp.arange(GRID_W, dtype=F32), rows)
    inv = ROPE_THETA ** (-jnp.arange(ROPE_FREQS, dtype=F32) / ROPE_FREQS)
    ang_r, ang_c = row[:, None] * inv, col[:, None] * inv
    cr, sr, cc, sc = jnp.cos(ang_r), jnp.sin(ang_r), jnp.cos(ang_c), jnp.sin(ang_c)
    zero = jnp.zeros_like(sr)
    cos = jnp.concatenate([cr, cr, cc, cc], -1)
    sin_lo = jnp.concatenate([-sr, zero, -sc, zero], -1)
    sin_hi = jnp.concatenate([zero, sr, zero, sc], -1)
    return cos, sin_lo, sin_hi


def _chunk_cumsum(x, rows, reverse):
    s = 1
    while s < HG_CHUNK:
        if reverse:
            x = x + jnp.where(rows < HG_CHUNK - s, pltpu.roll(x, HG_CHUNK - s, 0), 0.0)
        else:
            x = x + jnp.where(rows >= s, pltpu.roll(x, s, 0), 0.0)
        s *= 2
    return x


def _hgrn_kernel(*refs, seq, has_init, want_state):
    hq_ref, hff_ref, hfb_ref, hi_ref, hgt_ref, lbf_ref, lbb_ref, ng_ref = refs[:8]
    pos = 8
    if has_init:
        sf0_ref, sb0_ref = refs[pos:pos + 2]
        pos += 2
    o_ref = refs[pos]
    pos += 1
    if want_state:
        sf_ref, sb_ref = refs[pos:pos + 2]
        pos += 2
    q_scr, of_scr, ob_scr, st_scr = refs[pos:pos + 4]

    C = HG_CHUNK
    n_chunks = seq // C
    q_in = hq_ref[...]
    q_scr[...] = q_in * _sigmoid(q_in)
    for h in range(HG_GROUP):
        if has_init:
            st_scr[h] = sf0_ref[h].T
            st_scr[HG_GROUP + h] = sb0_ref[h].T
        else:
            st_scr[h] = jnp.zeros((HG_DV, HG_DK), F32)
            st_scr[HG_GROUP + h] = jnp.zeros((HG_DV, HG_DK), F32)

    rows = lax.broadcasted_iota(jnp.int32, (C, HG_DK), 0)
    ti = lax.broadcasted_iota(jnp.int32, (2 * C, 2 * C), 0)
    si = lax.broadcasted_iota(jnp.int32, (2 * C, 2 * C), 1)
    same_chunk = (ti // C) == (si // C)
    p1_sees_p0 = jnp.logical_and(ti >= C, si < C)
    pair_mask_f = jnp.logical_or(jnp.logical_and(same_chunk, si <= ti), p1_sees_p0)
    pair_mask_b = jnp.logical_or(jnp.logical_and(same_chunk, si >= ti), p1_sees_p0)
    dirs = ((hff_ref, lbf_ref, of_scr, False, pair_mask_f, C // 2 - 1, C - 1),
            (hfb_ref, lbb_ref, ob_scr, True, pair_mask_b, C // 2, 0))
    n_pairs = n_chunks // 2
    zeros = jnp.zeros((C, HG_DK), BF16)

    def chunk_prep(z_ref, lb_ref, reverse, mid, last, rs, cols):
        q = q_scr[rs, cols]
        v = hi_ref[rs, cols].astype(BF16)
        lb = lb_ref[:, cols]
        sig = _sigmoid(z_ref[rs, cols])
        log_f = jnp.log(lb + (1.0 - lb) * sig)
        k = (1.0 - lb) * (1.0 - sig)
        b = _chunk_cumsum(log_f, rows, reverse)
        b_mid = b[mid:mid + 1, :]
        b_last = b[last:last + 1, :]
        e_mid = jnp.exp(b - b_mid)
        qe = q * e_mid
        ke = k * (1.0 / e_mid)
        return dict(v=v, qe=qe.astype(BF16), ke=ke.astype(BF16), qb=qe * jnp.exp(b_mid),
                    kl=ke * jnp.exp(b_last - b_mid), decay=jnp.exp(b_last))

    def pair_step(c, carry):
        f0 = pl.multiple_of(c * 2 * C, C)
        b1 = pl.multiple_of((n_pairs - 1 - c) * 2 * C, C)
        starts = ((f0, pl.multiple_of(f0 + C, C)), (pl.multiple_of(b1 + C, C), b1))
        streams = [(h, d) for h in range(HG_GROUP) for d in range(2)]
        prep = []
        for h, d in streams:
            z_ref, lb_ref, _, reverse, _, mid, last = dirs[d]
            cols = slice(h * HG_DK, (h + 1) * HG_DK)
            p0, p1 = (chunk_prep(z_ref, lb_ref, reverse, mid, last, pl.ds(r, C), cols) for r in starts[d])
            cat = lambda a, b: jnp.concatenate([a, b], axis=0)
            prep.append(dict(
                v=cat(p0["v"], p1["v"]),
                a_lhs=cat(jnp.concatenate([p0["qe"], zeros, zeros], axis=1),
                          jnp.concatenate([zeros, p1["qb"].astype(BF16), p1["qe"]], axis=1)),
                a_rhs=cat(jnp.concatenate([p0["ke"], p0["kl"].astype(BF16), zeros], axis=1),
                          jnp.concatenate([zeros, zeros, p1["ke"]], axis=1)),
                qb=cat(p0["qb"], p1["qb"] * p0["decay"]).astype(BF16),
                kl=cat(p0["kl"] * p1["decay"], p1["kl"]).astype(BF16),
                decay=p0["decay"] * p1["decay"]))
        a_raw = [lax.dot_general(p["a_lhs"], p["a_rhs"], NT_DIMS, preferred_element_type=F32) for p in prep]
        states = [st_scr[d * HG_GROUP + h] for h, d in streams]
        o_inter = [lax.dot_general(p["qb"], st.astype(BF16), NT_DIMS, preferred_element_type=F32)
                   for p, st in zip(prep, states)]
        u_t = [lax.dot_general(p["v"], p["kl"], TN_DIMS, preferred_element_type=F32) for p in prep]
        o_intra = [jnp.dot(jnp.where(dirs[d][4], a, 0.0).astype(BF16), p["v"], preferred_element_type=F32)
                   for (h, d), a, p in zip(streams, a_raw, prep)]
        for i, (h, d) in enumerate(streams):
            cols = slice(h * HG_DK, (h + 1) * HG_DK)
            o = o_intra[i] + o_inter[i]
            dirs[d][2][pl.ds(starts[d][0], C), cols] = o[:C]
            dirs[d][2][pl.ds(starts[d][1], C), cols] = o[C:]
            st_scr[d * HG_GROUP + h] = states[i] * prep[i]["decay"] + u_t[i]
        return carry

    lax.fori_loop(0, n_pairs, pair_step, 0, unroll=2)

    for h in range(HG_GROUP):
        cols = slice(h * HG_DK, (h + 1) * HG_DK)
        o = of_scr[:, cols] + ob_scr[:, cols]
        o = o * lax.rsqrt(jnp.mean(o * o, -1, keepdims=True) + RMS_EPS) * ng_ref[...]
        gt = hgt_ref[:, cols]
        o_ref[:, cols] = (o * (gt * _sigmoid(gt))).astype(o_ref.dtype)
        if want_state:
            sf_ref[h] = st_scr[h].T
            sb_ref[h] = st_scr[HG_GROUP + h].T


def _hgrn(z, lb_f, lb_b, norm_g, row0, n_batch, seq, init_states=None, layer=None, want_state=False):
    n_groups = HG_HEADS // HG_GROUP
    r0 = row0 // seq
    has_init = init_states is not None

    def zspec(off):
        return pl.BlockSpec((seq, HG_GW), lambda b, g, o=off // HG_GW: (r0 + b, o + g))

    lbspec = pl.BlockSpec((1, HG_GW), lambda b, g: (0, g))
    in_specs = [zspec(HQ_OFF), zspec(HFF_OFF), zspec(HFB_OFF), zspec(HI_OFF), zspec(HGT_OFF),
                lbspec, lbspec, pl.BlockSpec((1, HG_DV), lambda b, g: (0, 0))]
    args = [z, z, z, z, z, lb_f.reshape(1, HG_DIM), lb_b.reshape(1, HG_DIM), norm_g.reshape(1, HG_DV)]
    if has_init:
        st_spec = pl.BlockSpec((None, None, HG_GROUP, HG_DK, HG_DV), lambda b, g: (b, layer, g, 0, 0))
        in_specs += [st_spec, st_spec]
        args += list(init_states)
    out_specs = [pl.BlockSpec((seq, HG_GW), lambda b, g: (b, g))]
    out_shape = [jax.ShapeDtypeStruct((n_batch * seq, HG_VDIM), BF16)]
    if want_state:
        so = pl.BlockSpec((None, HG_GROUP, HG_DK, HG_DV), lambda b, g: (b, g, 0, 0))
        out_specs += [so, so]
        out_shape += [jax.ShapeDtypeStruct((n_batch, HG_HEADS, HG_DK, HG_DV), F32)] * 2
    return pl.pallas_call(
        functools.partial(_hgrn_kernel, seq=seq, has_init=has_init, want_state=want_state),
        grid=(n_batch, n_groups),
        in_specs=in_specs,
        out_specs=out_specs,
        out_shape=out_shape,
        scratch_shapes=[pltpu.VMEM((seq, HG_GW), F32), pltpu.VMEM((seq, HG_GW), F32),
                        pltpu.VMEM((seq, HG_GW), F32),
                        pltpu.VMEM((2 * HG_GROUP, HG_DV, HG_DK), F32)],
        compiler_params=_params(2),
        name=f"hgrn_t{seq}",
    )(*args)


def _oproj_kernel(*refs, n_x, n_ctx_tiles):
    attc_ref, attl_ref, hgc_ref, hgl_ref, w_ref = refs[:5]
    x_refs = refs[5:5 + n_x]
    mod_ref, g_ref, b_ref, o_ref = refs[5 + n_x:]
    is_ctx = pl.program_id(0) < n_ctx_tiles
    for r0 in range(0, OPROJ_TM, OPROJ_SUB):
        rows = slice(r0, r0 + OPROJ_SUB)
        att = jnp.where(is_ctx, attc_ref[rows], attl_ref[rows])
        hg = jnp.where(is_ctx, hgc_ref[rows], hgl_ref[rows])
        mix = (jnp.dot(att, w_ref[0:ATT_DIM, :], preferred_element_type=F32)
               + jnp.dot(hg, w_ref[ATT_DIM:MIX_DIM, :], preferred_element_type=F32))
        y = DEEPNORM_ALPHA * _pick_rows(x_refs, is_ctx, rows) + mod_ref[2:3, :] * mix
        o_ref[rows] = _layer_norm(y, g_ref[0:1, :], b_ref[0:1, :])


def _oproj(att_c, att_l, hg_c, hg_l, w_o, xs, mods, ln_g, ln_b, layer, mod_idx):
    n_tok = sum(x.shape[0] for x in xs)
    n_ctx_tiles = att_c.shape[0] // OPROJ_TM
    row = lambda i: (i, 0)
    ctx_row = lambda i: (jnp.minimum(i, n_ctx_tiles - 1), 0)
    lat_row = lambda i: (jnp.maximum(i - n_ctx_tiles, 0), 0)
    return pl.pallas_call(
        functools.partial(_oproj_kernel, n_x=len(xs), n_ctx_tiles=n_ctx_tiles),
        grid=(n_tok // OPROJ_TM,),
        in_specs=[pl.BlockSpec((OPROJ_TM, ATT_DIM), ctx_row),
                  pl.BlockSpec((OPROJ_TM, ATT_DIM), lat_row),
                  pl.BlockSpec((OPROJ_TM, HG_VDIM), ctx_row),
                  pl.BlockSpec((OPROJ_TM, HG_VDIM), lat_row),
                  pl.BlockSpec((MIX_DIM, D_MODEL), lambda i: (0, 0))]
        + _token_specs(xs, OPROJ_TM, n_ctx_tiles, 1) + [
                  pl.BlockSpec((None, None, 6, D_MODEL), lambda i: (layer, mod_idx(i), 0, 0)),
                  pl.BlockSpec((None, 2, D_MODEL), lambda i: (layer, 0, 0)),
                  pl.BlockSpec((None, 2, D_MODEL), lambda i: (layer, 0, 0))],
        out_specs=pl.BlockSpec((OPROJ_TM, D_MODEL), row),
        out_shape=jax.ShapeDtypeStruct((n_tok, D_MODEL), F32),
        compiler_params=_params(1),
        name="oproj_ln",
    )(att_c, att_l, hg_c, hg_l, w_o, *xs, mods, ln_g, ln_b)


def _mlp_kernel(*refs, n_cast, n_out, n_ctx_tiles):
    x_ref, mod_ref, wu_ref, wd_ref, g_ref, b_ref = refs[:6]
    cast_in = refs[6:6 + n_cast]
    outs = refs[6 + n_cast:6 + n_cast + n_out]
    cast_out = refs[6 + n_cast + n_out:6 + 2 * n_cast + n_out]
    h_ref, acc_ref = refs[6 + 2 * n_cast + n_out:]
    i, f = pl.program_id(0), pl.program_id(1)
    n_f = pl.num_programs(1)

    @pl.when(f == 0)
    def _():
        shift, scale = mod_ref[3:4, :], mod_ref[4:5, :]
        h_ref[...] = (x_ref[...] * (1.0 + scale) + shift).astype(BF16)
        acc_ref[...] = jnp.zeros_like(acc_ref)

    if n_cast:
        @pl.when(i * n_f + f < CAST_STEPS)
        def _():
            for src, dst in zip(cast_in, cast_out):
                dst[...] = src[...].astype(BF16)

    u = jnp.dot(h_ref[...], wu_ref[...], preferred_element_type=F32)
    u = jnp.square(jnp.maximum(u, 0.0)).astype(BF16)
    for n in range(0, D_MODEL, MLP_TN):
        acc_ref[:, n:n + MLP_TN] += jnp.dot(u, wd_ref[:, n:n + MLP_TN], preferred_element_type=F32)

    def finish(o_ref):
        y = DEEPNORM_ALPHA * x_ref[...] + mod_ref[5:6, :] * acc_ref[...]
        o_ref[...] = _layer_norm(y, g_ref[1:2, :], b_ref[1:2, :])

    last = f == n_f - 1
    if n_out == 1:
        pl.when(last)(lambda: finish(outs[0]))
    else:
        pl.when(jnp.logical_and(last, i < n_ctx_tiles))(lambda: finish(outs[0]))
        pl.when(jnp.logical_and(last, i >= n_ctx_tiles))(lambda: finish(outs[1]))


def _cast_specs(weights, layer):
    n_f = D_FF // MLP_TF
    in_specs, out_specs, out_shape = [], [], []
    for w in weights:
        _, rows, cols = w.shape
        slab = rows // CAST_STEPS
        in_specs.append(pl.BlockSpec(
            (None, slab, cols), lambda i, f: (layer, jnp.minimum(i * n_f + f, CAST_STEPS - 1), 0)))
        out_specs.append(pl.BlockSpec(
            (slab, cols), lambda i, f: (jnp.minimum(i * n_f + f, CAST_STEPS - 1), 0)))
        out_shape.append(jax.ShapeDtypeStruct((rows, cols), BF16))
    return in_specs, out_specs, out_shape


def _mlp(x, mods, w_up, w_down, ln_g, ln_b, layer, mod_idx, next_weights=(), split_rows=None):
    n_tok = x.shape[0]
    n_f = D_FF // MLP_TF
    assert (n_tok // MLP_TM) * n_f >= CAST_STEPS
    cast_in_specs, cast_out_specs, cast_shapes = _cast_specs(next_weights, layer + 1)
    if split_rows is None:
        n_ctx_tiles = n_tok // MLP_TM
        y_specs = [pl.BlockSpec((MLP_TM, D_MODEL), lambda i, f: (i, 0))]
        y_shapes = [jax.ShapeDtypeStruct((n_tok, D_MODEL), F32)]
    else:
        n_ctx_tiles = split_rows // MLP_TM
        y_specs = [pl.BlockSpec((MLP_TM, D_MODEL), lambda i, f: (jnp.minimum(i, n_ctx_tiles - 1), 0)),
                   pl.BlockSpec((MLP_TM, D_MODEL), lambda i, f: (jnp.maximum(i - n_ctx_tiles, 0), 0))]
        y_shapes = [jax.ShapeDtypeStruct((split_rows, D_MODEL), F32),
                    jax.ShapeDtypeStruct((n_tok - split_rows, D_MODEL), F32)]
    return pl.pallas_call(
        functools.partial(_mlp_kernel, n_cast=len(next_weights), n_out=len(y_specs),
                          n_ctx_tiles=n_ctx_tiles),
        grid=(n_tok // MLP_TM, n_f),
        in_specs=[pl.BlockSpec((MLP_TM, D_MODEL), lambda i, f: (i, 0)),
                  pl.BlockSpec((None, None, 6, D_MODEL), lambda i, f: (layer, mod_idx(i), 0, 0)),
                  pl.BlockSpec((D_MODEL, MLP_TF), lambda i, f: (0, f)),
                  pl.BlockSpec((MLP_TF, D_MODEL), lambda i, f: (f, 0)),
                  pl.BlockSpec((None, 2, D_MODEL), lambda i, f: (layer, 0, 0)),
                  pl.BlockSpec((None, 2, D_MODEL), lambda i, f: (layer, 0, 0))] + cast_in_specs,
        out_specs=y_specs + cast_out_specs,
        out_shape=y_shapes + cast_shapes,
        scratch_shapes=[pltpu.VMEM((MLP_TM, D_MODEL), BF16), pltpu.VMEM((MLP_TM, D_MODEL), F32)],
        compiler_params=_params(2),
        name="mlp_ln",
    )(x, mods, w_up, w_down, ln_g, ln_b, *next_weights)


def _cache_pack_kernel(*refs, depth):
    k_refs, v_refs = refs[:depth], refs[depth:2 * depth]
    ok_ref, ov_ref = refs[2 * depth:]
    for l in range(depth):
        for h in range(KV_HEADS):
            cols = slice(h * HEAD_DIM, (h + 1) * HEAD_DIM)
            ok_ref[0, l, :, h, :] = k_refs[l][:, cols]
            ov_ref[0, l, :, h, :] = v_refs[l][:, cols]


def _cache_pack(ks, vs, n_batch, seq):
    depth = len(ks)
    in_spec = pl.BlockSpec((seq, KV_DIM), lambda b: (b, 0))
    out_spec = pl.BlockSpec((1, depth, seq, KV_HEADS, HEAD_DIM), lambda b: (b, 0, 0, 0, 0))
    out_shape = jax.ShapeDtypeStruct((n_batch, depth, seq, KV_HEADS, HEAD_DIM), F32)
    return pl.pallas_call(
        functools.partial(_cache_pack_kernel, depth=depth),
        grid=(n_batch,),
        in_specs=[in_spec] * (2 * depth),
        out_specs=[out_spec, out_spec],
        out_shape=[out_shape, out_shape],
        compiler_params=_params(1),
        name="cache_pack",
    )(*ks, *vs)


def _lower_bounds(lb_logits):
    p = jax.nn.softmax(lb_logits.astype(F32), axis=0)
    cs = jnp.cumsum(p, axis=0)
    return cs - cs[0:1]


def kernel(x_prompt, x_sample, cache_k, cache_v, state_hgrn_fwd, state_hgrn_bwd, c, c_ctx, w_mod, b_mod,
           w_in, attn_sink, attn_norm_g, hg_lb_logits, hg_norm_g, w_o, ln_g, ln_b, w_up, w_down):
    bp, seq_p, d = x_prompt.shape
    bs, seq_s, _ = x_sample.shape
    depth = w_in.shape[0]
    past = cache_k.shape[2]
    n_ctx, n_lat = bp * seq_p, bs * seq_s

    xs = (x_prompt.reshape(n_ctx, d), x_sample.reshape(n_lat, d))
    c_rows = jnp.concatenate([c_ctx[None, :], c, jnp.zeros((MOD_ROWS - 1 - bs, d), F32)], 0)
    mods = _modulation(c_rows, w_mod, b_mod)[:, :1 + bs].reshape(depth, 1 + bs, 6, d)

    weights = (w_in, w_o, w_up, w_down)
    w_in_b, w_o_b, w_up_b, w_down_b = (w[0].astype(BF16) for w in weights)
    lb_f, lb_b = _lower_bounds(hg_lb_logits[0]), _lower_bounds(hg_lb_logits[1])
    rope_tabs = _rope_tables(seq_s)
    ck = cache_k.reshape(bs, depth, past, KV_DIM)
    cv = cache_v.reshape(bs, depth, past, KV_DIM)

    new_k, new_v, new_sf, new_sb = [], [], [], []
    for l in range(depth):
        z, k_c, v_c = _inproj(xs, mods, w_in_b, l, _mod_index(_inproj_tile(xs), n_ctx, seq_s), n_ctx)
        new_k.append(k_c)
        new_v.append(v_c)

        att_c = _ctx_attention(z, attn_sink[l], attn_norm_g[l], bp, seq_p)
        att_l = _lat_attention(z, ck, cv, l, attn_sink[l], attn_norm_g[l], rope_tabs, n_ctx, bs, seq_s)

        hg_c, s_f, s_b = _hgrn(z, lb_f[l], lb_b[l], hg_norm_g[l], 0, bp, seq_p, want_state=True)
        (hg_l,) = _hgrn(z, lb_f[l], lb_b[l], hg_norm_g[l], n_ctx, bs, seq_s,
                        init_states=(state_hgrn_fwd, state_hgrn_bwd), layer=l)
        new_sf.append(s_f)
        new_sb.append(s_b)

        x = _oproj(att_c, att_l, hg_c, hg_l, w_o_b, xs, mods, ln_g, ln_b, l,
                   _mod_index(OPROJ_TM, n_ctx, seq_s))
        mlp_mods = _mod_index(MLP_TM, n_ctx, seq_s)
        if l + 1 < depth:
            x, w_in_b, w_o_b, w_up_b, w_down_b = _mlp(x, mods, w_up_b, w_down_b, ln_g, ln_b, l, mlp_mods,
                                                      next_weights=weights)
            xs = (x,)
        else:
            y_p, y_s = _mlp(x, mods, w_up_b, w_down_b, ln_g, ln_b, l, mlp_mods, split_rows=n_ctx)

    new_cache_k, new_cache_v = _cache_pack(new_k, new_v, bp, seq_p)
    return (y_p.reshape(bp, seq_p, d), y_s.reshape(bs, seq_s, d), new_cache_k, new_cache_v,
            jnp.stack(new_sf, 1), jnp.stack(new_sb, 1))
```

```python
import functools

import jax
import jax.numpy as jnp
from jax import lax
from jax.experimental import pallas as pl
from jax.experimental.pallas import tpu as pltpu

F32 = jnp.float32
BF16 = jnp.bfloat16

D_MODEL = 2048
DEPTH = 4
GRID_W = 64
HEAD_DIM = 128
ATT_HEADS = 8
KV_HEADS = 2
Q_PER_KV = ATT_HEADS // KV_HEADS
ATT_DIM = ATT_HEADS * HEAD_DIM
KV_DIM = KV_HEADS * HEAD_DIM
WINDOW = 128
BLOCK = 128
ATT_SCALE = HEAD_DIM ** -0.5
ROPE_THETA = 10000.0
ROPE_FREQS = HEAD_DIM // 4
HG_HEADS = 8
HG_DK = 128
HG_DV = 128
HG_DIM = HG_HEADS * HG_DK
HG_VDIM = HG_HEADS * HG_DV
HG_CHUNK = 32
MIX_DIM = ATT_DIM + HG_VDIM
IN_DIM = ATT_DIM + 2 * KV_DIM + 3 * HG_DIM + 2 * HG_VDIM
D_FF = 4 * D_MODEL
LN_EPS = 1e-5
RMS_EPS = 1e-6
DEEPNORM_ALPHA = (2 * DEPTH) ** 0.25
NEG_INF = -1e30
LOG2E = 1.4426950408889634

Q_OFF = 0
K_OFF = ATT_DIM
V_OFF = K_OFF + KV_DIM
HQ_OFF = V_OFF + KV_DIM
HFF_OFF = HQ_OFF + HG_DIM
HFB_OFF = HFF_OFF + HG_DIM
HI_OFF = HFB_OFF + HG_DIM
HGT_OFF = HI_OFF + HG_VDIM

V7X_VMEM_LIMIT_BYTES = 56 * 1024 * 1024

MOD_ROWS = 8
MOD_TN = 768
TOKEN_TILE = 1024
INPROJ_TN = 1664
OPROJ_TM = 512
OPROJ_SUB = 256
MLP_TM = 512
MLP_TF = 1024
MLP_TN = 512
CAST_STEPS = 128
HG_GROUP = 4
HG_GW = HG_GROUP * HG_DK

NT_DIMS = (((1,), (1,)), ((), ()))
TN_DIMS = (((0,), (0,)), ((), ()))


def _params(n_axes):
    return pltpu.CompilerParams(dimension_semantics=("arbitrary",) * n_axes,
                                vmem_limit_bytes=V7X_VMEM_LIMIT_BYTES)


def _sigmoid(x):
    return 1.0 / (1.0 + jnp.exp(-x))


def _layer_norm(y, g, b):
    mu = jnp.mean(y, -1, keepdims=True)
    yc = y - mu
    var = jnp.mean(yc * yc, -1, keepdims=True)
    return yc * lax.rsqrt(var + LN_EPS) * g + b


def _mod_kernel(c_ref, w_ref, b_ref, o_ref):
    c = c_ref[...]
    a = (c * _sigmoid(c)).astype(BF16)
    o_ref[0] = jnp.dot(a, w_ref[0].astype(BF16), preferred_element_type=F32) + b_ref[0]


def _modulation(c_rows, w_mod, b_mod):
    depth, d, n = w_mod.shape
    return pl.pallas_call(
        _mod_kernel,
        grid=(depth, n // MOD_TN),
        in_specs=[pl.BlockSpec((MOD_ROWS, d), lambda l, j: (0, 0)),
                  pl.BlockSpec((1, d, MOD_TN), lambda l, j: (l, 0, j)),
                  pl.BlockSpec((1, 1, MOD_TN), lambda l, j: (l, 0, j))],
        out_specs=pl.BlockSpec((1, MOD_ROWS, MOD_TN), lambda l, j: (l, 0, j)),
        out_shape=jax.ShapeDtypeStruct((depth, MOD_ROWS, n), F32),
        compiler_params=_params(2),
        name="modulation",
    )(c_rows, w_mod, b_mod.reshape(depth, 1, n))


def _mod_index(tile_rows, n_ctx_rows, lat_rows):
    n_ctx_tiles = n_ctx_rows // tile_rows
    per_lat = lat_rows // tile_rows

    def idx(i):
        return jnp.where(i < n_ctx_tiles, 0, 1 + (i - n_ctx_tiles) // per_lat)
    return idx


def _pick_rows(x_refs, is_ctx, rows=slice(None)):
    if len(x_refs) == 1:
        return x_refs[0][rows]
    return jnp.where(is_ctx, x_refs[0][rows], x_refs[1][rows])


def _inproj_kernel(*refs, n_x, n_ctx_tiles):
    x_refs = refs[:n_x]
    mod_ref, w_ref, z_ref, k_ref, v_ref, h_ref = refs[n_x:]
    i, j = pl.program_id(0), pl.program_id(1)

    @pl.when(j == 0)
    def _():
        shift, scale = mod_ref[0:1, :], mod_ref[1:2, :]
        h_ref[...] = (_pick_rows(x_refs, i < n_ctx_tiles) * (1.0 + scale) + shift).astype(BF16)

    z_ref[...] = jnp.dot(h_ref[...], w_ref[...], preferred_element_type=F32)

    @pl.when(jnp.logical_and(j == 0, i < n_ctx_tiles))
    def _():
        k_ref[...] = z_ref[:, K_OFF:K_OFF + KV_DIM]
        v_ref[...] = z_ref[:, V_OFF:V_OFF + KV_DIM]


def _token_specs(xs, tile, n_ctx_tiles, n_grid_axes):
    def spec(row_of):
        index_map = ((lambda i: (row_of(i), 0)) if n_grid_axes == 1 else (lambda i, j: (row_of(i), 0)))
        return pl.BlockSpec((tile, D_MODEL), index_map)
    if len(xs) == 1:
        return [spec(lambda i: i)]
    return [spec(lambda i: jnp.minimum(i, n_ctx_tiles - 1)), spec(lambda i: jnp.maximum(i - n_ctx_tiles, 0))]


def _inproj_tile(xs):
    return TOKEN_TILE // len(xs)


def _inproj(xs, mods, w_in, layer, mod_idx, n_ctx):
    assert V_OFF + KV_DIM <= INPROJ_TN
    tile = _inproj_tile(xs)
    n_tok = sum(x.shape[0] for x in xs)
    n_ctx_tiles = n_ctx // tile
    kv_spec = pl.BlockSpec((tile, KV_DIM), lambda i, j: (jnp.minimum(i, n_ctx_tiles - 1), 0))
    return pl.pallas_call(
        functools.partial(_inproj_kernel, n_x=len(xs), n_ctx_tiles=n_ctx_tiles),
        grid=(n_tok // tile, IN_DIM // INPROJ_TN),
        in_specs=_token_specs(xs, tile, n_ctx_tiles, 2) + [
                  pl.BlockSpec((None, None, 6, D_MODEL), lambda i, j: (layer, mod_idx(i), 0, 0)),
                  pl.BlockSpec((D_MODEL, INPROJ_TN), lambda i, j: (0, j))],
        out_specs=[pl.BlockSpec((tile, INPROJ_TN), lambda i, j: (i, j)), kv_spec, kv_spec],
        out_shape=[jax.ShapeDtypeStruct((n_tok, IN_DIM), F32),
                   jax.ShapeDtypeStruct((n_ctx, KV_DIM), F32),
                   jax.ShapeDtypeStruct((n_ctx, KV_DIM), F32)],
        scratch_shapes=[pltpu.VMEM((tile, D_MODEL), BF16)],
        compiler_params=_params(2),
        name="inproj",
    )(*xs, mods, w_in)


def _with_ones(v):
    return jnp.concatenate([v, jnp.ones_like(v)], axis=1)


def _sink_attend(scores, values, sink2):
    c = ATT_SCALE * LOG2E
    m_raw = jnp.max(scores[0], -1, keepdims=True)
    for s in scores[1:]:
        m_raw = jnp.maximum(m_raw, jnp.max(s, -1, keepdims=True))
    m2 = jnp.maximum(m_raw * c, sink2)
    acc = None
    for s, v1 in zip(scores, values):
        e = jnp.exp2(s * c - m2).astype(BF16)
        part = jnp.dot(e, v1, preferred_element_type=F32)
        acc = part if acc is None else acc + part
    den = acc[:, HEAD_DIM:HEAD_DIM + 1] + jnp.exp2(sink2 - m2)
    return acc[:, :HEAD_DIM] * (1.0 / den)


def _rms_norm_store(o_scr, sq, g_ref, o_ref):
    ss = jnp.sum(sq, -1, keepdims=True)
    inv = lax.rsqrt(ss * (1.0 / ATT_DIM) + RMS_EPS)
    o_ref[...] = (o_scr[...] * inv * g_ref[...]).astype(o_ref.dtype)


def _head_cols(g, r):
    h = g * Q_PER_KV + r
    return slice(h * HEAD_DIM, (h + 1) * HEAD_DIM)


def _group_sinks(sink_ref, g, rows):
    return jnp.concatenate([jnp.full((rows, 1), sink_ref[g * Q_PER_KV + r] * LOG2E, F32)
                            for r in range(Q_PER_KV)], axis=0)


def _scatter_heads(og, g, rows, sq, o_scr):
    for r in range(Q_PER_KV):
        oh = og[r * rows:(r + 1) * rows]
        sq = sq + oh * oh
        o_scr[:, _head_cols(g, r)] = oh
    return sq


def _ctx_attn_kernel(sink_ref, q_ref, k_ref, v_ref, g_ref, o_ref, o_scr):
    c = ATT_SCALE * LOG2E
    sq = jnp.zeros((q_ref.shape[0], HEAD_DIM), F32)
    kv_cols = [slice(g * HEAD_DIM, (g + 1) * HEAD_DIM) for g in range(KV_HEADS)]
    ks = [k_ref[:, cols].astype(BF16) for cols in kv_cols]
    vs = [v_ref[:, cols].astype(BF16) for cols in kv_cols]
    heads = [(g, r) for g in range(KV_HEADS) for r in range(Q_PER_KV)]
    scores = [lax.dot_general(q_ref[:, _head_cols(g, r)].astype(BF16), ks[g], NT_DIMS,
                              preferred_element_type=F32) for g, r in heads]
    probs = []
    for (g, r), s in zip(heads, scores):
        sink2 = sink_ref[g * Q_PER_KV + r] * LOG2E
        m2 = jnp.maximum(jnp.max(s, -1, keepdims=True) * c, sink2)
        e = jnp.exp2(s * c - m2)
        den = jnp.sum(e, -1, keepdims=True) + jnp.exp2(sink2 - m2)
        probs.append((e * (1.0 / den)).astype(BF16))
    outs = [jnp.dot(p, vs[g], preferred_element_type=F32) for (g, r), p in zip(heads, probs)]
    for (g, r), oh in zip(heads, outs):
        sq = sq + oh * oh
        o_scr[:, _head_cols(g, r)] = oh
    _rms_norm_store(o_scr, sq, g_ref, o_ref)


def _ctx_attention(z, sink, norm_g, n_batch, seq):
    kb, vb = K_OFF // KV_DIM, V_OFF // KV_DIM
    return pl.pallas_call(
        _ctx_attn_kernel,
        grid=(n_batch,),
        in_specs=[pl.BlockSpec(memory_space=pltpu.SMEM),
                  pl.BlockSpec((seq, ATT_DIM), lambda b: (b, 0)),
                  pl.BlockSpec((seq, KV_DIM), lambda b: (b, kb)),
                  pl.BlockSpec((seq, KV_DIM), lambda b: (b, vb)),
                  pl.BlockSpec((1, ATT_DIM), lambda b: (0, 0))],
        out_specs=pl.BlockSpec((seq, ATT_DIM), lambda b: (b, 0)),
        out_shape=jax.ShapeDtypeStruct((n_batch * seq, ATT_DIM), BF16),
        scratch_shapes=[pltpu.VMEM((seq, ATT_DIM), F32)],
        compiler_params=_params(1),
        name="ctx_attention",
    )(sink, z, z, z, norm_g.reshape(1, ATT_DIM))


def _rope(x, cos, sin_lo, sin_hi):
    return (x * cos + pltpu.roll(x, HEAD_DIM - ROPE_FREQS, 1) * sin_lo
            + pltpu.roll(x, ROPE_FREQS, 1) * sin_hi)


def _lat_attn_kernel(sink_ref, q_ref, k_ref, v_ref, ck_ref, cv_ref, cq_ref, slq_ref, shq_ref,
                     ca_ref, sla_ref, sha_ref, g_ref, o_ref, kr_scr, v_scr, ckb_scr, cvb_scr, o_scr):
    n = pl.program_id(1)
    seq = k_ref.shape[0]
    win = 3 * BLOCK

    @pl.when(n == 0)
    def _():
        for g in range(KV_HEADS):
            cols = slice(g * HEAD_DIM, (g + 1) * HEAD_DIM)
            kr_scr[:, cols] = _rope(k_ref[:, cols], ca_ref[...], sla_ref[...], sha_ref[...]).astype(BF16)
            v_scr[g] = _with_ones(v_ref[:, cols].astype(BF16))
            cvb_scr[g] = _with_ones(cv_ref[:, cols].astype(BF16))
        ckb_scr[...] = ck_ref[...].astype(BF16)

    k0 = pl.multiple_of(jnp.clip((n - 1) * BLOCK, 0, seq - win), BLOCK)
    rows = Q_PER_KV * BLOCK
    qpos = n * BLOCK + lax.broadcasted_iota(jnp.int32, (rows, win), 0) % BLOCK
    kpos = k0 + lax.broadcasted_iota(jnp.int32, (rows, win), 1)
    valid = jnp.abs(kpos - qpos) <= WINDOW
    sq = jnp.zeros((BLOCK, HEAD_DIM), F32)
    groups = range(KV_HEADS)
    kv_cols = [slice(g * HEAD_DIM, (g + 1) * HEAD_DIM) for g in groups]
    qgs = [jnp.concatenate(
        [_rope(q_ref[:, _head_cols(g, r)], cq_ref[...], slq_ref[...], shq_ref[...]).astype(BF16)
         for r in range(Q_PER_KV)], axis=0) for g in groups]
    s_locs = [jnp.where(valid, lax.dot_general(qgs[g], kr_scr[pl.ds(k0, win), kv_cols[g]], NT_DIMS,
                                               preferred_element_type=F32), NEG_INF / ATT_SCALE)
              for g in groups]
    s_ctxs = [lax.dot_general(qgs[g], ckb_scr[:, kv_cols[g]], NT_DIMS, preferred_element_type=F32)
              for g in groups]
    ogs = [_sink_attend([s_locs[g], s_ctxs[g]], [v_scr[g, pl.ds(k0, win), :], cvb_scr[g]],
                        _group_sinks(sink_ref, g, BLOCK)) for g in groups]
    for g in groups:
        sq = _scatter_heads(ogs[g], g, BLOCK, sq, o_scr)
    _rms_norm_store(o_scr, sq, g_ref, o_ref)


def _lat_attention(z, cache_k, cache_v, layer, sink, norm_g, rope_tabs, row0, n_batch, seq):
    kb, vb = K_OFF // KV_DIM, V_OFF // KV_DIM
    nq = seq // BLOCK
    qrow0, srow0 = row0 // BLOCK, row0 // seq
    past = cache_k.shape[2]
    cos, sin_lo, sin_hi = rope_tabs
    qtab = pl.BlockSpec((BLOCK, HEAD_DIM), lambda b, n: (n, 0))
    atab = pl.BlockSpec((seq, HEAD_DIM), lambda b, n: (0, 0))
    cache_spec = pl.BlockSpec((None, None, past, KV_DIM), lambda b, n: (b, layer, 0, 0))
    return pl.pallas_call(
        _lat_attn_kernel,
        grid=(n_batch, nq),
        in_specs=[pl.BlockSpec(memory_space=pltpu.SMEM),
                  pl.BlockSpec((BLOCK, ATT_DIM), lambda b, n: (qrow0 + b * nq + n, 0)),
                  pl.BlockSpec((seq, KV_DIM), lambda b, n: (srow0 + b, kb)),
                  pl.BlockSpec((seq, KV_DIM), lambda b, n: (srow0 + b, vb)),
                  cache_spec, cache_spec, qtab, qtab, qtab, atab, atab, atab,
                  pl.BlockSpec((1, ATT_DIM), lambda b, n: (0, 0))],
        out_specs=pl.BlockSpec((BLOCK, ATT_DIM), lambda b, n: (b * nq + n, 0)),
        out_shape=jax.ShapeDtypeStruct((n_batch * seq, ATT_DIM), BF16),
        scratch_shapes=[pltpu.VMEM((seq, KV_DIM), BF16), pltpu.VMEM((KV_HEADS, seq, 2 * HEAD_DIM), BF16),
                        pltpu.VMEM((past, KV_DIM), BF16), pltpu.VMEM((KV_HEADS, past, 2 * HEAD_DIM), BF16),
                        pltpu.VMEM((BLOCK, ATT_DIM), F32)],
        compiler_params=_params(2),
        name="lat_attention",
    )(sink, z, z, z, cache_k, cache_v, cos, sin_lo, sin_hi, cos, sin_lo, sin_hi,
      norm_g.reshape(1, ATT_DIM))


def _rope_tables(seq):
    rows = seq // GRID_W
    row = jnp.repeat(jnp.arange(rows, dtype=F32), GRID_W)
    col = jnp.tile(jnp.arange(GRID_W, dtype=F32), rows)
    inv = ROPE_THETA ** (-jnp.arange(ROPE_FREQS, dtype=F32) / ROPE_FREQS)
    ang_r, ang_c = row[:, None] * inv, col[:, None] * inv
    cr, sr, cc, sc = jnp.cos(ang_r), jnp.sin(ang_r), jnp.cos(ang_c), jnp.sin(ang_c)
    zero = jnp.zeros_like(sr)
    cos = jnp.concatenate([cr, cr, cc, cc], -1)
    sin_lo = jnp.concatenate([-sr, zero, -sc, zero], -1)
    sin_hi = jnp.concatenate([zero, sr, zero, sc], -1)
    return cos, sin_lo, sin_hi


def _chunk_cumsum(x, rows, reverse):
    s = 1
    while s < HG_CHUNK:
        if reverse:
            x = x + jnp.where(rows < HG_CHUNK - s, pltpu.roll(x, HG_CHUNK - s, 0), 0.0)
        else:
            x = x + jnp.where(rows >= s, pltpu.roll(x, s, 0), 0.0)
        s *= 2
    return x


def _hgrn_kernel(*refs, seq, has_init, want_state):
    hq_ref, hff_ref, hfb_ref, hi_ref, hgt_ref, lbf_ref, lbb_ref, ng_ref = refs[:8]
    pos = 8
    if has_init:
        sf0_ref, sb0_ref = refs[pos:pos + 2]
        pos += 2
    if want_state:
        pos += 2
    o_ref = refs[pos]
    pos += 1
    if want_state:
        sf_ref, sb_ref = refs[pos:pos + 2]
        pos += 2
    q_scr, of_scr, ob_scr, st_scr = refs[pos:pos + 4]

    C = HG_CHUNK
    n_chunks = seq // C
    q_in = hq_ref[...]
    q_scr[...] = q_in * _sigmoid(q_in)
    for h in range(HG_GROUP):
        if has_init:
            st_scr[h] = sf0_ref[h].T
            st_scr[HG_GROUP + h] = sb0_ref[h].T
        else:
            st_scr[h] = jnp.zeros((HG_DV, HG_DK), F32)
            st_scr[HG_GROUP + h] = jnp.zeros((HG_DV, HG_DK), F32)

    rows = lax.broadcasted_iota(jnp.int32, (C, HG_DK), 0)
    ti = lax.broadcasted_iota(jnp.int32, (2 * C, 2 * C), 0)
    si = lax.broadcasted_iota(jnp.int32, (2 * C, 2 * C), 1)
    same_chunk = (ti // C) == (si // C)
    p1_sees_p0 = jnp.logical_and(ti >= C, si < C)
    pair_mask_f = jnp.logical_or(jnp.logical_and(same_chunk, si <= ti), p1_sees_p0)
    pair_mask_b = jnp.logical_or(jnp.logical_and(same_chunk, si >= ti), p1_sees_p0)
    dirs = ((hff_ref, lbf_ref, of_scr, False, pair_mask_f, C // 2 - 1, C - 1),
            (hfb_ref, lbb_ref, ob_scr, True, pair_mask_b, C // 2, 0))
    n_pairs = n_chunks // 2
    zeros = jnp.zeros((C, HG_DK), BF16)

    def chunk_prep(z_ref, lb_ref, reverse, mid, last, rs, cols):
        q = q_scr[rs, cols]
        v = hi_ref[rs, cols].astype(BF16)
        lb = lb_ref[:, cols]
        sig = _sigmoid(z_ref[rs, cols])
        log_f = jnp.log(lb + (1.0 - lb) * sig)
        k = (1.0 - lb) * (1.0 - sig)
        b = _chunk_cumsum(log_f, rows, reverse)
        b_mid = b[mid:mid + 1, :]
        b_last = b[last:last + 1, :]
        e_mid = jnp.exp(b - b_mid)
        qe = q * e_mid
        ke = k * (1.0 / e_mid)
        return dict(v=v, qe=qe.astype(BF16), ke=ke.astype(BF16), qb=qe * jnp.exp(b_mid),
                    kl=ke * jnp.exp(b_last - b_mid), decay=jnp.exp(b_last))

    def pair_step(c, carry):
        f0 = pl.multiple_of(c * 2 * C, C)
        b1 = pl.multiple_of((n_pairs - 1 - c) * 2 * C, C)
        starts = ((f0, pl.multiple_of(f0 + C, C)), (pl.multiple_of(b1 + C, C), b1))
        streams = [(h, d) for h in range(HG_GROUP) for d in range(2)]
        prep = []
        for h, d in streams:
            z_ref, lb_ref, _, reverse, _, mid, last = dirs[d]
            cols = slice(h * HG_DK, (h + 1) * HG_DK)
            p0, p1 = (chunk_prep(z_ref, lb_ref, reverse, mid, last, pl.ds(r, C), cols) for r in starts[d])
            cat = lambda a, b: jnp.concatenate([a, b], axis=0)
            prep.append(dict(
                v=cat(p0["v"], p1["v"]),
                a_lhs=cat(jnp.concatenate([p0["qe"], zeros, zeros], axis=1),
                          jnp.concatenate([zeros, p1["qb"].astype(BF16), p1["qe"]], axis=1)),
                a_rhs=cat(jnp.concatenate([p0["ke"], p0["kl"].astype(BF16), zeros], axis=1),
                          jnp.concatenate([zeros, zeros, p1["ke"]], axis=1)),
                qb=cat(p0["qb"], p1["qb"] * p0["decay"]).astype(BF16),
                kl=cat(p0["kl"] * p1["decay"], p1["kl"]).astype(BF16),
                decay=p0["decay"] * p1["decay"]))
        a_raw = [lax.dot_general(p["a_lhs"], p["a_rhs"], NT_DIMS, preferred_element_type=F32) for p in prep]
        states = [st_scr[d * HG_GROUP + h] for h, d in streams]
        o_inter = [lax.dot_general(p["qb"], st.astype(BF16), NT_DIMS, preferred_element_type=F32)
                   for p, st in zip(prep, states)]
        u_t = [lax.dot_general(p["v"], p["kl"], TN_DIMS, preferred_element_type=F32) for p in prep]
        o_intra = [jnp.dot(jnp.where(dirs[d][4], a, 0.0).astype(BF16), p["v"], preferred_element_type=F32)
                   for (h, d), a, p in zip(streams, a_raw, prep)]
        for i, (h, d) in enumerate(streams):
            cols = slice(h * HG_DK, (h + 1) * HG_DK)
            o = o_intra[i] + o_inter[i]
            dirs[d][2][pl.ds(starts[d][0], C), cols] = o[:C]
            dirs[d][2][pl.ds(starts[d][1], C), cols] = o[C:]
            st_scr[d * HG_GROUP + h] = states[i] * prep[i]["decay"] + u_t[i]
        return carry

    lax.fori_loop(0, n_pairs, pair_step, 0, unroll=2)

    for h in range(HG_GROUP):
        cols = slice(h * HG_DK, (h + 1) * HG_DK)
        o = of_scr[:, cols] + ob_scr[:, cols]
        o = o * lax.rsqrt(jnp.mean(o * o, -1, keepdims=True) + RMS_EPS) * ng_ref[...]
        gt = hgt_ref[:, cols]
        o_ref[:, cols] = (o * (gt * _sigmoid(gt))).astype(o_ref.dtype)
        if want_state:
            sf_ref[h] = st_scr[h].T
            sb_ref[h] = st_scr[HG_GROUP + h].T


def _hgrn(z, lb_f, lb_b, norm_g, row0, n_batch, seq, init_states=None, layer=None, state_bufs=None):
    n_groups = HG_HEADS // HG_GROUP
    r0 = row0 // seq
    has_init = init_states is not None
    want_state = state_bufs is not None

    def zspec(off):
        return pl.BlockSpec((seq, HG_GW), lambda b, g, o=off // HG_GW: (r0 + b, o + g))

    lbspec = pl.BlockSpec((1, HG_GW), lambda b, g: (0, g))
    in_specs = [zspec(HQ_OFF), zspec(HFF_OFF), zspec(HFB_OFF), zspec(HI_OFF), zspec(HGT_OFF),
                lbspec, lbspec, pl.BlockSpec((1, HG_DV), lambda b, g: (0, 0))]
    args = [z, z, z, z, z, lb_f.reshape(1, HG_DIM), lb_b.reshape(1, HG_DIM), norm_g.reshape(1, HG_DV)]
    if has_init:
        st_spec = pl.BlockSpec((None, None, HG_GROUP, HG_DK, HG_DV), lambda b, g: (b, layer, g, 0, 0))
        in_specs += [st_spec, st_spec]
        args += list(init_states)
    out_specs = [pl.BlockSpec((seq, HG_GW), lambda b, g: (b, g))]
    out_shape = [jax.ShapeDtypeStruct((n_batch * seq, HG_VDIM), BF16)]
    aliases = {}
    if want_state:
        so = pl.BlockSpec((None, None, HG_GROUP, HG_DK, HG_DV), lambda b, g: (layer, b, g, 0, 0))
        for buf in state_bufs:
            aliases[len(args)] = len(out_specs)
            in_specs.append(pl.BlockSpec(memory_space=pl.ANY))
            args.append(buf)
            out_specs.append(so)
            out_shape.append(jax.ShapeDtypeStruct(buf.shape, buf.dtype))
    return pl.pallas_call(
        functools.partial(_hgrn_kernel, seq=seq, has_init=has_init, want_state=want_state),
        grid=(n_batch, n_groups),
        in_specs=in_specs,
        out_specs=out_specs,
        out_shape=out_shape,
        scratch_shapes=[pltpu.VMEM((seq, HG_GW), F32), pltpu.VMEM((seq, HG_GW), F32),
                        pltpu.VMEM((seq, HG_GW), F32),
                        pltpu.VMEM((2 * HG_GROUP, HG_DV, HG_DK), F32)],
        input_output_aliases=aliases,
        compiler_params=_params(2),
        name=f"hgrn_t{seq}",
    )(*args)


def _oproj_kernel(*refs, n_x, n_ctx_tiles):
    attc_ref, attl_ref, hgc_ref, hgl_ref, w_ref = refs[:5]
    x_refs = refs[5:5 + n_x]
    mod_ref, g_ref, b_ref, o_ref = refs[5 + n_x:]
    is_ctx = pl.program_id(0) < n_ctx_tiles
    for r0 in range(0, OPROJ_TM, OPROJ_SUB):
        rows = slice(r0, r0 + OPROJ_SUB)
        att = jnp.where(is_ctx, attc_ref[rows], attl_ref[rows])
        hg = jnp.where(is_ctx, hgc_ref[rows], hgl_ref[rows])
        mix = (jnp.dot(att, w_ref[0:ATT_DIM, :], preferred_element_type=F32)
               + jnp.dot(hg, w_ref[ATT_DIM:MIX_DIM, :], preferred_element_type=F32))
        y = DEEPNORM_ALPHA * _pick_rows(x_refs, is_ctx, rows) + mod_ref[2:3, :] * mix
        o_ref[rows] = _layer_norm(y, g_ref[0:1, :], b_ref[0:1, :])


def _oproj(att_c, att_l, hg_c, hg_l, w_o, xs, mods, ln_g, ln_b, layer, mod_idx):
    n_tok = sum(x.shape[0] for x in xs)
    n_ctx_tiles = att_c.shape[0] // OPROJ_TM
    row = lambda i: (i, 0)
    ctx_row = lambda i: (jnp.minimum(i, n_ctx_tiles - 1), 0)
    lat_row = lambda i: (jnp.maximum(i - n_ctx_tiles, 0), 0)
    return pl.pallas_call(
        functools.partial(_oproj_kernel, n_x=len(xs), n_ctx_tiles=n_ctx_tiles),
        grid=(n_tok // OPROJ_TM,),
        in_specs=[pl.BlockSpec((OPROJ_TM, ATT_DIM), ctx_row),
                  pl.BlockSpec((OPROJ_TM, ATT_DIM), lat_row),
                  pl.BlockSpec((OPROJ_TM, HG_VDIM), ctx_row),
                  pl.BlockSpec((OPROJ_TM, HG_VDIM), lat_row),
                  pl.BlockSpec((MIX_DIM, D_MODEL), lambda i: (0, 0))]
        + _token_specs(xs, OPROJ_TM, n_ctx_tiles, 1) + [
                  pl.BlockSpec((None, None, 6, D_MODEL), lambda i: (layer, mod_idx(i), 0, 0)),
                  pl.BlockSpec((None, 2, D_MODEL), lambda i: (layer, 0, 0)),
                  pl.BlockSpec((None, 2, D_MODEL), lambda i: (layer, 0, 0))],
        out_specs=pl.BlockSpec((OPROJ_TM, D_MODEL), row),
        out_shape=jax.ShapeDtypeStruct((n_tok, D_MODEL), F32),
        compiler_params=_params(1),
        name="oproj_ln",
    )(att_c, att_l, hg_c, hg_l, w_o, *xs, mods, ln_g, ln_b)


def _mlp_kernel(*refs, n_cast, n_out, n_ctx_tiles):
    x_ref, mod_ref, wu_ref, wd_ref, g_ref, b_ref = refs[:6]
    cast_in = refs[6:6 + n_cast]
    outs = refs[6 + n_cast:6 + n_cast + n_out]
    cast_out = refs[6 + n_cast + n_out:6 + 2 * n_cast + n_out]
    h_ref, acc_ref = refs[6 + 2 * n_cast + n_out:]
    i, f = pl.program_id(0), pl.program_id(1)
    n_f = pl.num_programs(1)

    @pl.when(f == 0)
    def _():
        shift, scale = mod_ref[3:4, :], mod_ref[4:5, :]
        h_ref[...] = (x_ref[...] * (1.0 + scale) + shift).astype(BF16)
        acc_ref[...] = jnp.zeros_like(acc_ref)

    if n_cast:
        @pl.when(i * n_f + f < CAST_STEPS)
        def _():
            for src, dst in zip(cast_in, cast_out):
                dst[...] = src[...].astype(BF16)

    u = jnp.dot(h_ref[...], wu_ref[...], preferred_element_type=F32)
    u = jnp.square(jnp.maximum(u, 0.0)).astype(BF16)
    for n in range(0, D_MODEL, MLP_TN):
        acc_ref[:, n:n + MLP_TN] += jnp.dot(u, wd_ref[:, n:n + MLP_TN], preferred_element_type=F32)

    def finish(o_ref):
        y = DEEPNORM_ALPHA * x_ref[...] + mod_ref[5:6, :] * acc_ref[...]
        o_ref[...] = _layer_norm(y, g_ref[1:2, :], b_ref[1:2, :])

    last = f == n_f - 1
    if n_out == 1:
        pl.when(last)(lambda: finish(outs[0]))
    else:
        pl.when(jnp.logical_and(last, i < n_ctx_tiles))(lambda: finish(outs[0]))
        pl.when(jnp.logical_and(last, i >= n_ctx_tiles))(lambda: finish(outs[1]))


def _cast_specs(weights, layer):
    n_f = D_FF // MLP_TF
    in_specs, out_specs, out_shape = [], [], []
    for w in weights:
        _, rows, cols = w.shape
        slab = rows // CAST_STEPS
        in_specs.append(pl.BlockSpec(
            (None, slab, cols), lambda i, f: (layer, jnp.minimum(i * n_f + f, CAST_STEPS - 1), 0)))
        out_specs.append(pl.BlockSpec(
            (slab, cols), lambda i, f: (jnp.minimum(i * n_f + f, CAST_STEPS - 1), 0)))
        out_shape.append(jax.ShapeDtypeStruct((rows, cols), BF16))
    return in_specs, out_specs, out_shape


def _mlp(x, mods, w_up, w_down, ln_g, ln_b, layer, mod_idx, next_weights=(), split_rows=None):
    n_tok = x.shape[0]
    n_f = D_FF // MLP_TF
    assert (n_tok // MLP_TM) * n_f >= CAST_STEPS
    cast_in_specs, cast_out_specs, cast_shapes = _cast_specs(next_weights, layer + 1)
    if split_rows is None:
        n_ctx_tiles = n_tok // MLP_TM
        y_specs = [pl.BlockSpec((MLP_TM, D_MODEL), lambda i, f: (i, 0))]
        y_shapes = [jax.ShapeDtypeStruct((n_tok, D_MODEL), F32)]
    else:
        n_ctx_tiles = split_rows // MLP_TM
        y_specs = [pl.BlockSpec((MLP_TM, D_MODEL), lambda i, f: (jnp.minimum(i, n_ctx_tiles - 1), 0)),
                   pl.BlockSpec((MLP_TM, D_MODEL), lambda i, f: (jnp.maximum(i - n_ctx_tiles, 0), 0))]
        y_shapes = [jax.ShapeDtypeStruct((split_rows, D_MODEL), F32),
                    jax.ShapeDtypeStruct((n_tok - split_rows, D_MODEL), F32)]
    return pl.pallas_call(
        functools.partial(_mlp_kernel, n_cast=len(next_weights), n_out=len(y_specs),
                          n_ctx_tiles=n_ctx_tiles),
        grid=(n_tok // MLP_TM, n_f),
        in_specs=[pl.BlockSpec((MLP_TM, D_MODEL), lambda i, f: (i, 0)),
                  pl.BlockSpec((None, None, 6, D_MODEL), lambda i, f: (layer, mod_idx(i), 0, 0)),
                  pl.BlockSpec((D_MODEL, MLP_TF), lambda i, f: (0, f)),
                  pl.BlockSpec((MLP_TF, D_MODEL), lambda i, f: (f, 0)),
                  pl.BlockSpec((None, 2, D_MODEL), lambda i, f: (layer, 0, 0)),
                  pl.BlockSpec((None, 2, D_MODEL), lambda i, f: (layer, 0, 0))] + cast_in_specs,
        out_specs=y_specs + cast_out_specs,
        out_shape=y_shapes + cast_shapes,
        scratch_shapes=[pltpu.VMEM((MLP_TM, D_MODEL), BF16), pltpu.VMEM((MLP_TM, D_MODEL), F32)],
        compiler_params=_params(2),
        name="mlp_ln",
    )(x, mods, w_up, w_down, ln_g, ln_b, *next_weights)


def _cache_pack_kernel(*refs, depth):
    k_refs, v_refs = refs[:depth], refs[depth:2 * depth]
    ok_ref, ov_ref = refs[2 * depth:]
    for l in range(depth):
        for h in range(KV_HEADS):
            cols = slice(h * HEAD_DIM, (h + 1) * HEAD_DIM)
            ok_ref[0, l, :, h, :] = k_refs[l][:, cols]
            ov_ref[0, l, :, h, :] = v_refs[l][:, cols]


def _cache_pack(ks, vs, n_batch, seq):
    depth = len(ks)
    in_spec = pl.BlockSpec((seq, KV_DIM), lambda b: (b, 0))
    out_spec = pl.BlockSpec((1, depth, seq, KV_HEADS, HEAD_DIM), lambda b: (b, 0, 0, 0, 0))
    out_shape = jax.ShapeDtypeStruct((n_batch, depth, seq, KV_HEADS, HEAD_DIM), F32)
    return pl.pallas_call(
        functools.partial(_cache_pack_kernel, depth=depth),
        grid=(n_batch,),
        in_specs=[in_spec] * (2 * depth),
        out_specs=[out_spec, out_spec],
        out_shape=[out_shape, out_shape],
        compiler_params=_params(1),
        name="cache_pack",
    )(*ks, *vs)


def _lower_bounds(lb_logits):
    p = jax.nn.softmax(lb_logits.astype(F32), axis=0)
    cs = jnp.cumsum(p, axis=0)
    return cs - cs[0:1]


def kernel(x_prompt, x_sample, cache_k, cache_v, state_hgrn_fwd, state_hgrn_bwd, c, c_ctx, w_mod, b_mod,
           w_in, attn_sink, attn_norm_g, hg_lb_logits, hg_norm_g, w_o, ln_g, ln_b, w_up, w_down):
    bp, seq_p, d = x_prompt.shape
    bs, seq_s, _ = x_sample.shape
    depth = w_in.shape[0]
    past = cache_k.shape[2]
    n_ctx, n_lat = bp * seq_p, bs * seq_s

    xs = (x_prompt.reshape(n_ctx, d), x_sample.reshape(n_lat, d))
    c_rows = jnp.concatenate([c_ctx[None, :], c, jnp.zeros((MOD_ROWS - 1 - bs, d), F32)], 0)
    mods = _modulation(c_rows, w_mod, b_mod)[:, :1 + bs].reshape(depth, 1 + bs, 6, d)

    weights = (w_in, w_o, w_up, w_down)
    w_in_b, w_o_b, w_up_b, w_down_b = (w[0].astype(BF16) for w in weights)
    lb_f, lb_b = _lower_bounds(hg_lb_logits[0]), _lower_bounds(hg_lb_logits[1])
    rope_tabs = _rope_tables(seq_s)
    ck = cache_k.reshape(bs, depth, past, KV_DIM)
    cv = cache_v.reshape(bs, depth, past, KV_DIM)

    new_k, new_v = [], []
    s_f = jnp.zeros((depth, bp, HG_HEADS, HG_DK, HG_DV), F32)
    s_b = jnp.zeros((depth, bp, HG_HEADS, HG_DK, HG_DV), F32)
    for l in range(depth):
        z, k_c, v_c = _inproj(xs, mods, w_in_b, l, _mod_index(_inproj_tile(xs), n_ctx, seq_s), n_ctx)
        new_k.append(k_c)
        new_v.append(v_c)

        att_c = _ctx_attention(z, attn_sink[l], attn_norm_g[l], bp, seq_p)
        att_l = _lat_attention(z, ck, cv, l, attn_sink[l], attn_norm_g[l], rope_tabs, n_ctx, bs, seq_s)

        hg_c, s_f, s_b = _hgrn(z, lb_f[l], lb_b[l], hg_norm_g[l], 0, bp, seq_p, layer=l,
                               state_bufs=(s_f, s_b))
        (hg_l,) = _hgrn(z, lb_f[l], lb_b[l], hg_norm_g[l], n_ctx, bs, seq_s,
                        init_states=(state_hgrn_fwd, state_hgrn_bwd), layer=l)

        x = _oproj(att_c, att_l, hg_c, hg_l, w_o_b, xs, mods, ln_g, ln_b, l,
                   _mod_index(OPROJ_TM, n_ctx, seq_s))
        mlp_mods = _mod_index(MLP_TM, n_ctx, seq_s)
        if l + 1 < depth:
            x, w_in_b, w_o_b, w_up_b, w_down_b = _mlp(x, mods, w_up_b, w_down_b, ln_g, ln_b, l, mlp_mods,
                                                      next_weights=weights)
            xs = (x,)
        else:
            y_p, y_s = _mlp(x, mods, w_up_b, w_down_b, ln_g, ln_b, l, mlp_mods, split_rows=n_ctx)

    new_cache_k, new_cache_v = _cache_pack(new_k, new_v, bp, seq_p)
    return (y_p.reshape(bp, seq_p, d), y_s.reshape(bs, seq_s, d), new_cache_k, new_cache_v,
            jnp.swapaxes(s_f, 0, 1), jnp.swapaxes(s_b, 0, 1))
```

```python
import functools

import jax
import jax.numpy as jnp
from jax import lax
from jax.experimental import pallas as pl
from jax.experimental.pallas import tpu as pltpu

F32 = jnp.float32
BF16 = jnp.bfloat16

D_MODEL = 2048
DEPTH = 4
GRID_W = 64
HEAD_DIM = 128
ATT_HEADS = 8
KV_HEADS = 2
Q_PER_KV = ATT_HEADS // KV_HEADS
ATT_DIM = ATT_HEADS * HEAD_DIM
KV_DIM = KV_HEADS * HEAD_DIM
WINDOW = 128
BLOCK = 128
ATT_SCALE = HEAD_DIM ** -0.5
ROPE_THETA = 10000.0
ROPE_FREQS = HEAD_DIM // 4
HG_HEADS = 8
HG_DK = 128
HG_DV = 128
HG_DIM = HG_HEADS * HG_DK
HG_VDIM = HG_HEADS * HG_DV
HG_CHUNK = 32
MIX_DIM = ATT_DIM + HG_VDIM
IN_DIM = ATT_DIM + 2 * KV_DIM + 3 * HG_DIM + 2 * HG_VDIM
D_FF = 4 * D_MODEL
LN_EPS = 1e-5
RMS_EPS = 1e-6
DEEPNORM_ALPHA = (2 * DEPTH) ** 0.25
NEG_INF = -1e30
LOG2E = 1.4426950408889634

Q_OFF = 0
K_OFF = ATT_DIM
V_OFF = K_OFF + KV_DIM
HQ_OFF = V_OFF + KV_DIM
HFF_OFF = HQ_OFF + HG_DIM
HFB_OFF = HFF_OFF + HG_DIM
HI_OFF = HFB_OFF + HG_DIM
HGT_OFF = HI_OFF + HG_VDIM

V7X_VMEM_LIMIT_BYTES = 56 * 1024 * 1024

MOD_ROWS = 8
MOD_TN = 768
TOKEN_TILE = 1024
INPROJ_TN = 1664
OPROJ_TM = 512
OPROJ_SUB = 256
MLP_TM = 512
MLP_TF = 1024
MLP_TN = 512
CAST_STEPS = 128
HG_GROUP = 4
CTX_ATT_SEQS = 2
HG_GW = HG_GROUP * HG_DK

NT_DIMS = (((1,), (1,)), ((), ()))
TN_DIMS = (((0,), (0,)), ((), ()))


def _params(n_axes):
    return pltpu.CompilerParams(dimension_semantics=("arbitrary",) * n_axes,
                                vmem_limit_bytes=V7X_VMEM_LIMIT_BYTES)


def _sigmoid(x):
    return 1.0 / (1.0 + jnp.exp(-x))


def _layer_norm(y, g, b):
    mu = jnp.mean(y, -1, keepdims=True)
    yc = y - mu
    var = jnp.mean(yc * yc, -1, keepdims=True)
    return yc * lax.rsqrt(var + LN_EPS) * g + b


def _mod_kernel(c_ref, w_ref, b_ref, o_ref):
    c = c_ref[...]
    a = (c * _sigmoid(c)).astype(BF16)
    o_ref[0] = jnp.dot(a, w_ref[0].astype(BF16), preferred_element_type=F32) + b_ref[0]


def _modulation(c_rows, w_mod, b_mod):
    depth, d, n = w_mod.shape
    return pl.pallas_call(
        _mod_kernel,
        grid=(depth, n // MOD_TN),
        in_specs=[pl.BlockSpec((MOD_ROWS, d), lambda l, j: (0, 0)),
                  pl.BlockSpec((1, d, MOD_TN), lambda l, j: (l, 0, j)),
                  pl.BlockSpec((1, 1, MOD_TN), lambda l, j: (l, 0, j))],
        out_specs=pl.BlockSpec((1, MOD_ROWS, MOD_TN), lambda l, j: (l, 0, j)),
        out_shape=jax.ShapeDtypeStruct((depth, MOD_ROWS, n), F32),
        compiler_params=_params(2),
        name="modulation",
    )(c_rows, w_mod, b_mod.reshape(depth, 1, n))


def _mod_index(tile_rows, n_ctx_rows, lat_rows):
    n_ctx_tiles = n_ctx_rows // tile_rows
    per_lat = lat_rows // tile_rows

    def idx(i):
        return jnp.where(i < n_ctx_tiles, 0, 1 + (i - n_ctx_tiles) // per_lat)
    return idx


def _pick_rows(x_refs, is_ctx, rows=slice(None)):
    if len(x_refs) == 1:
        return x_refs[0][rows]
    return jnp.where(is_ctx, x_refs[0][rows], x_refs[1][rows])


def _inproj_kernel(*refs, n_x, n_ctx_tiles):
    x_refs = refs[:n_x]
    mod_ref, w_ref, z_ref, k_ref, v_ref, h_ref = refs[n_x:]
    i, j = pl.program_id(0), pl.program_id(1)

    @pl.when(j == 0)
    def _():
        shift, scale = mod_ref[0:1, :], mod_ref[1:2, :]
        h_ref[...] = (_pick_rows(x_refs, i < n_ctx_tiles) * (1.0 + scale) + shift).astype(BF16)

    z_ref[...] = jnp.dot(h_ref[...], w_ref[...], preferred_element_type=F32)

    @pl.when(jnp.logical_and(j == 0, i < n_ctx_tiles))
    def _():
        k_ref[...] = z_ref[:, K_OFF:K_OFF + KV_DIM]
        v_ref[...] = z_ref[:, V_OFF:V_OFF + KV_DIM]


def _token_specs(xs, tile, n_ctx_tiles, n_grid_axes):
    def spec(row_of):
        index_map = ((lambda i: (row_of(i), 0)) if n_grid_axes == 1 else (lambda i, j: (row_of(i), 0)))
        return pl.BlockSpec((tile, D_MODEL), index_map)
    if len(xs) == 1:
        return [spec(lambda i: i)]
    return [spec(lambda i: jnp.minimum(i, n_ctx_tiles - 1)), spec(lambda i: jnp.maximum(i - n_ctx_tiles, 0))]


def _inproj_tile(xs):
    return TOKEN_TILE // len(xs)


def _inproj(xs, mods, w_in, layer, mod_idx, n_ctx):
    assert V_OFF + KV_DIM <= INPROJ_TN
    tile = _inproj_tile(xs)
    n_tok = sum(x.shape[0] for x in xs)
    n_ctx_tiles = n_ctx // tile
    kv_spec = pl.BlockSpec((tile, KV_DIM), lambda i, j: (jnp.minimum(i, n_ctx_tiles - 1), 0))
    return pl.pallas_call(
        functools.partial(_inproj_kernel, n_x=len(xs), n_ctx_tiles=n_ctx_tiles),
        grid=(n_tok // tile, IN_DIM // INPROJ_TN),
        in_specs=_token_specs(xs, tile, n_ctx_tiles, 2) + [
                  pl.BlockSpec((None, None, 6, D_MODEL), lambda i, j: (layer, mod_idx(i), 0, 0)),
                  pl.BlockSpec((D_MODEL, INPROJ_TN), lambda i, j: (0, j))],
        out_specs=[pl.BlockSpec((tile, INPROJ_TN), lambda i, j: (i, j)), kv_spec, kv_spec],
        out_shape=[jax.ShapeDtypeStruct((n_tok, IN_DIM), F32),
                   jax.ShapeDtypeStruct((n_ctx, KV_DIM), F32),
                   jax.ShapeDtypeStruct((n_ctx, KV_DIM), F32)],
        scratch_shapes=[pltpu.VMEM((tile, D_MODEL), BF16)],
        compiler_params=_params(2),
        name="inproj",
    )(*xs, mods, w_in)


def _with_ones(v):
    return jnp.concatenate([v, jnp.ones_like(v)], axis=1)


def _sink_attend(scores, values, sink2):
    c = ATT_SCALE * LOG2E
    m_raw = jnp.max(scores[0], -1, keepdims=True)
    for s in scores[1:]:
        m_raw = jnp.maximum(m_raw, jnp.max(s, -1, keepdims=True))
    m2 = jnp.maximum(m_raw * c, sink2)
    acc = None
    for s, v1 in zip(scores, values):
        e = jnp.exp2(s * c - m2).astype(BF16)
        part = jnp.dot(e, v1, preferred_element_type=F32)
        acc = part if acc is None else acc + part
    den = acc[:, HEAD_DIM:HEAD_DIM + 1] + jnp.exp2(sink2 - m2)
    return acc[:, :HEAD_DIM] * (1.0 / den)


def _rms_norm_store(o_scr, sq, g_ref, o_ref):
    ss = jnp.sum(sq, -1, keepdims=True)
    inv = lax.rsqrt(ss * (1.0 / ATT_DIM) + RMS_EPS)
    o_ref[...] = (o_scr[...] * inv * g_ref[...]).astype(o_ref.dtype)


def _head_cols(g, r):
    h = g * Q_PER_KV + r
    return slice(h * HEAD_DIM, (h + 1) * HEAD_DIM)


def _group_sinks(sink_ref, g, rows):
    return jnp.concatenate([jnp.full((rows, 1), sink_ref[g * Q_PER_KV + r] * LOG2E, F32)
                            for r in range(Q_PER_KV)], axis=0)


def _scatter_heads(og, g, rows, sq, o_scr):
    for r in range(Q_PER_KV):
        oh = og[r * rows:(r + 1) * rows]
        sq = sq + oh * oh
        o_scr[:, _head_cols(g, r)] = oh
    return sq


def _ctx_attn_kernel(sink_ref, q_ref, k_ref, v_ref, g_ref, o_ref, o_scr, *, seq):
    c = ATT_SCALE * LOG2E
    kv_cols = [slice(g * HEAD_DIM, (g + 1) * HEAD_DIM) for g in range(KV_HEADS)]
    heads = [(g, r) for g in range(KV_HEADS) for r in range(Q_PER_KV)]
    for r0 in range(0, q_ref.shape[0], seq):
        rows = slice(r0, r0 + seq)
        sq = jnp.zeros((seq, HEAD_DIM), F32)
        ks = [k_ref[rows, cols].astype(BF16) for cols in kv_cols]
        vs = [v_ref[rows, cols].astype(BF16) for cols in kv_cols]
        scores = [lax.dot_general(q_ref[rows, _head_cols(g, r)].astype(BF16), ks[g], NT_DIMS,
                                  preferred_element_type=F32) for g, r in heads]
        probs = []
        for (g, r), s in zip(heads, scores):
            sink2 = sink_ref[g * Q_PER_KV + r] * LOG2E
            m2 = jnp.maximum(jnp.max(s, -1, keepdims=True) * c, sink2)
            e = jnp.exp2(s * c - m2)
            den = jnp.sum(e, -1, keepdims=True) + jnp.exp2(sink2 - m2)
            probs.append((e * (1.0 / den)).astype(BF16))
        outs = [jnp.dot(p, vs[g], preferred_element_type=F32) for (g, r), p in zip(heads, probs)]
        for (g, r), oh in zip(heads, outs):
            sq = sq + oh * oh
            o_scr[rows, _head_cols(g, r)] = oh
        _rms_norm_store(o_scr.at[rows], sq, g_ref, o_ref.at[rows])


def _ctx_attention(z, sink, norm_g, n_batch, seq):
    kb, vb = K_OFF // KV_DIM, V_OFF // KV_DIM
    rows = CTX_ATT_SEQS * seq
    return pl.pallas_call(
        functools.partial(_ctx_attn_kernel, seq=seq),
        grid=(n_batch // CTX_ATT_SEQS,),
        in_specs=[pl.BlockSpec(memory_space=pltpu.SMEM),
                  pl.BlockSpec((rows, ATT_DIM), lambda b: (b, 0)),
                  pl.BlockSpec((rows, KV_DIM), lambda b: (b, kb)),
                  pl.BlockSpec((rows, KV_DIM), lambda b: (b, vb)),
                  pl.BlockSpec((1, ATT_DIM), lambda b: (0, 0))],
        out_specs=pl.BlockSpec((rows, ATT_DIM), lambda b: (b, 0)),
        out_shape=jax.ShapeDtypeStruct((n_batch * seq, ATT_DIM), BF16),
        scratch_shapes=[pltpu.VMEM((rows, ATT_DIM), F32)],
        compiler_params=_params(1),
        name="ctx_attention",
    )(sink, z, z, z, norm_g.reshape(1, ATT_DIM))


def _rope(x, cos, sin_lo, sin_hi):
    return (x * cos + pltpu.roll(x, HEAD_DIM - ROPE_FREQS, 1) * sin_lo
            + pltpu.roll(x, ROPE_FREQS, 1) * sin_hi)


def _lat_attn_kernel(sink_ref, q_ref, k_ref, v_ref, ck_ref, cv_ref, cq_ref, slq_ref, shq_ref,
                     ca_ref, sla_ref, sha_ref, g_ref, o_ref, kr_scr, v_scr, ckb_scr, cvb_scr, o_scr):
    n = pl.program_id(1)
    seq = k_ref.shape[0]
    win = 3 * BLOCK

    @pl.when(n == 0)
    def _():
        for g in range(KV_HEADS):
            cols = slice(g * HEAD_DIM, (g + 1) * HEAD_DIM)
            kr_scr[:, cols] = _rope(k_ref[:, cols], ca_ref[...], sla_ref[...], sha_ref[...]).astype(BF16)
            v_scr[g] = _with_ones(v_ref[:, cols].astype(BF16))
            cvb_scr[g] = _with_ones(cv_ref[:, cols].astype(BF16))
        ckb_scr[...] = ck_ref[...].astype(BF16)

    k0 = pl.multiple_of(jnp.clip((n - 1) * BLOCK, 0, seq - win), BLOCK)
    rows = Q_PER_KV * BLOCK
    qpos = n * BLOCK + lax.broadcasted_iota(jnp.int32, (rows, win), 0) % BLOCK
    kpos = k0 + lax.broadcasted_iota(jnp.int32, (rows, win), 1)
    valid = jnp.abs(kpos - qpos) <= WINDOW
    sq = jnp.zeros((BLOCK, HEAD_DIM), F32)
    for g in range(KV_HEADS):
        kv_cols = slice(g * HEAD_DIM, (g + 1) * HEAD_DIM)
        k_loc = kr_scr[pl.ds(k0, win), kv_cols]
        v_loc = v_scr[g, pl.ds(k0, win), :]
        k_ctx = ckb_scr[:, kv_cols]
        v_ctx = cvb_scr[g]
        qg = jnp.concatenate(
            [_rope(q_ref[:, _head_cols(g, r)], cq_ref[...], slq_ref[...], shq_ref[...]).astype(BF16)
             for r in range(Q_PER_KV)], axis=0)
        s_loc = lax.dot_general(qg, k_loc, NT_DIMS, preferred_element_type=F32)
        s_loc = jnp.where(valid, s_loc, NEG_INF / ATT_SCALE)
        s_ctx = lax.dot_general(qg, k_ctx, NT_DIMS, preferred_element_type=F32)
        og = _sink_attend([s_loc, s_ctx], [v_loc, v_ctx], _group_sinks(sink_ref, g, BLOCK))
        sq = _scatter_heads(og, g, BLOCK, sq, o_scr)
    _rms_norm_store(o_scr, sq, g_ref, o_ref)


def _lat_attention(z, cache_k, cache_v, layer, sink, norm_g, rope_tabs, row0, n_batch, seq):
    kb, vb = K_OFF // KV_DIM, V_OFF // KV_DIM
    nq = seq // BLOCK
    qrow0, srow0 = row0 // BLOCK, row0 // seq
    past = cache_k.shape[2]
    cos, sin_lo, sin_hi = rope_tabs
    qtab = pl.BlockSpec((BLOCK, HEAD_DIM), lambda b, n: (n, 0))
    atab = pl.BlockSpec((seq, HEAD_DIM), lambda b, n: (0, 0))
    cache_spec = pl.BlockSpec((None, None, past, KV_DIM), lambda b, n: (b, layer, 0, 0))
    return pl.pallas_call(
        _lat_attn_kernel,
        grid=(n_batch, nq),
        in_specs=[pl.BlockSpec(memory_space=pltpu.SMEM),
                  pl.BlockSpec((BLOCK, ATT_DIM), lambda b, n: (qrow0 + b * nq + n, 0)),
                  pl.BlockSpec((seq, KV_DIM), lambda b, n: (srow0 + b, kb)),
                  pl.BlockSpec((seq, KV_DIM), lambda b, n: (srow0 + b, vb)),
                  cache_spec, cache_spec, qtab, qtab, qtab, atab, atab, atab,
                  pl.BlockSpec((1, ATT_DIM), lambda b, n: (0, 0))],
        out_specs=pl.BlockSpec((BLOCK, ATT_DIM), lambda b, n: (b * nq + n, 0)),
        out_shape=jax.ShapeDtypeStruct((n_batch * seq, ATT_DIM), BF16),
        scratch_shapes=[pltpu.VMEM((seq, KV_DIM), BF16), pltpu.VMEM((KV_HEADS, seq, 2 * HEAD_DIM), BF16),
                        pltpu.VMEM((past, KV_DIM), BF16), pltpu.VMEM((KV_HEADS, past, 2 * HEAD_DIM), BF16),
                        pltpu.VMEM((BLOCK, ATT_DIM), F32)],
        compiler_params=_params(2),
        name="lat_attention",
    )(sink, z, z, z, cache_k, cache_v, cos, sin_lo, sin_hi, cos, sin_lo, sin_hi,
      norm_g.reshape(1, ATT_DIM))


def _rope_tables(seq):
    rows = seq // GRID_W
    row = jnp.repeat(jnp.arange(rows, dtype=F32), GRID_W)
    col = jnp.tile(jnp.arange(GRID_W, dtype=F32), rows)
    inv = ROPE_THETA ** (-jnp.arange(ROPE_FREQS, dtype=F32) / ROPE_FREQS)
    ang_r, ang_c = row[:, None] * inv, col[:, None] * inv
    cr, sr, cc, sc = jnp.cos(ang_r), jnp.sin(ang_r), jnp.cos(ang_c), jnp.sin(ang_c)
    zero = jnp.zeros_like(sr)
    cos = jnp.concatenate([cr, cr, cc, cc], -1)
    sin_lo = jnp.concatenate([-sr, zero, -sc, zero], -1)
    sin_hi = jnp.concatenate([zero, sr, zero, sc], -1)
    return cos, sin_lo, sin_hi


def _chunk_cumsum(x, rows, reverse):
    s = 1
    while s < HG_CHUNK:
        if reverse:
            x = x + jnp.where(rows < HG_CHUNK - s, pltpu.roll(x, HG_CHUNK - s, 0), 0.0)
        else:
            x = x + jnp.where(rows >= s, pltpu.roll(x, s, 0), 0.0)
        s *= 2
    return x


def _hgrn_kernel(*refs, seq, has_init, want_state):
    hq_ref, hff_ref, hfb_ref, hi_ref, hgt_ref, lbf_ref, lbb_ref, ng_ref = refs[:8]
    pos = 8
    if has_init:
        sf0_ref, sb0_ref = refs[pos:pos + 2]
        pos += 2
    if want_state:
        pos += 2
    o_ref = refs[pos]
    pos += 1
    if want_state:
        sf_ref, sb_ref = refs[pos:pos + 2]
        pos += 2
    q_scr, of_scr, ob_scr, st_scr = refs[pos:pos + 4]

    C = HG_CHUNK
    n_chunks = seq // C
    q_in = hq_ref[...]
    q_scr[...] = q_in * _sigmoid(q_in)
    for h in range(HG_GROUP):
        if has_init:
            st_scr[h] = sf0_ref[h].T
            st_scr[HG_GROUP + h] = sb0_ref[h].T
        else:
            st_scr[h] = jnp.zeros((HG_DV, HG_DK), F32)
            st_scr[HG_GROUP + h] = jnp.zeros((HG_DV, HG_DK), F32)

    rows = lax.broadcasted_iota(jnp.int32, (C, HG_DK), 0)
    ti = lax.broadcasted_iota(jnp.int32, (2 * C, 2 * C), 0)
    si = lax.broadcasted_iota(jnp.int32, (2 * C, 2 * C), 1)
    same_chunk = (ti // C) == (si // C)
    p1_sees_p0 = jnp.logical_and(ti >= C, si < C)
    pair_mask_f = jnp.logical_or(jnp.logical_and(same_chunk, si <= ti), p1_sees_p0)
    pair_mask_b = jnp.logical_or(jnp.logical_and(same_chunk, si >= ti), p1_sees_p0)
    dirs = ((hff_ref, lbf_ref, of_scr, False, pair_mask_f, C // 2 - 1, C - 1),
            (hfb_ref, lbb_ref, ob_scr, True, pair_mask_b, C // 2, 0))
    n_pairs = n_chunks // 2
    zeros = jnp.zeros((C, HG_DK), BF16)

    def chunk_prep(z_ref, lb_ref, reverse, mid, last, rs, cols):
        q = q_scr[rs, cols]
        v = hi_ref[rs, cols].astype(BF16)
        lb = lb_ref[:, cols]
        sig = _sigmoid(z_ref[rs, cols])
        log_f = jnp.log(lb + (1.0 - lb) * sig)
        k = (1.0 - lb) * (1.0 - sig)
        b = _chunk_cumsum(log_f, rows, reverse)
        b_mid = b[mid:mid + 1, :]
        b_last = b[last:last + 1, :]
        e_mid = jnp.exp(b - b_mid)
        qe = q * e_mid
        ke = k * (1.0 / e_mid)
        return dict(v=v, qe=qe.astype(BF16), ke=ke.astype(BF16), qb=qe * jnp.exp(b_mid),
                    kl=ke * jnp.exp(b_last - b_mid), decay=jnp.exp(b_last))

    def pair_step(c, carry):
        f0 = pl.multiple_of(c * 2 * C, C)
        b1 = pl.multiple_of((n_pairs - 1 - c) * 2 * C, C)
        starts = ((f0, pl.multiple_of(f0 + C, C)), (pl.multiple_of(b1 + C, C), b1))
        streams = [(h, d) for h in range(HG_GROUP) for d in range(2)]
        prep = []
        for h, d in streams:
            z_ref, lb_ref, _, reverse, _, mid, last = dirs[d]
            cols = slice(h * HG_DK, (h + 1) * HG_DK)
            p0, p1 = (chunk_prep(z_ref, lb_ref, reverse, mid, last, pl.ds(r, C), cols) for r in starts[d])
            cat = lambda a, b: jnp.concatenate([a, b], axis=0)
            prep.append(dict(
                v=cat(p0["v"], p1["v"]),
                a_lhs=cat(jnp.concatenate([p0["qe"], zeros, zeros], axis=1),
                          jnp.concatenate([zeros, p1["qb"].astype(BF16), p1["qe"]], axis=1)),
                a_rhs=cat(jnp.concatenate([p0["ke"], p0["kl"].astype(BF16), zeros], axis=1),
                          jnp.concatenate([zeros, zeros, p1["ke"]], axis=1)),
                qb=cat(p0["qb"], p1["qb"] * p0["decay"]).astype(BF16),
                kl=cat(p0["kl"] * p1["decay"], p1["kl"]).astype(BF16),
                decay=p0["decay"] * p1["decay"]))
        a_raw = [lax.dot_general(p["a_lhs"], p["a_rhs"], NT_DIMS, preferred_element_type=F32) for p in prep]
        states = [st_scr[d * HG_GROUP + h] for h, d in streams]
        o_inter = [lax.dot_general(p["qb"], st.astype(BF16), NT_DIMS, preferred_element_type=F32)
                   for p, st in zip(prep, states)]
        u_t = [lax.dot_general(p["v"], p["kl"], TN_DIMS, preferred_element_type=F32) for p in prep]
        o_intra = [jnp.dot(jnp.where(dirs[d][4], a, 0.0).astype(BF16), p["v"], preferred_element_type=F32)
                   for (h, d), a, p in zip(streams, a_raw, prep)]
        for i, (h, d) in enumerate(streams):
            cols = slice(h * HG_DK, (h + 1) * HG_DK)
            o = o_intra[i] + o_inter[i]
            dirs[d][2][pl.ds(starts[d][0], C), cols] = o[:C]
            dirs[d][2][pl.ds(starts[d][1], C), cols] = o[C:]
            st_scr[d * HG_GROUP + h] = states[i] * prep[i]["decay"] + u_t[i]
        return carry

    lax.fori_loop(0, n_pairs, pair_step, 0, unroll=2)

    for h in range(HG_GROUP):
        cols = slice(h * HG_DK, (h + 1) * HG_DK)
        o = of_scr[:, cols] + ob_scr[:, cols]
        o = o * lax.rsqrt(jnp.mean(o * o, -1, keepdims=True) + RMS_EPS) * ng_ref[...]
        gt = hgt_ref[:, cols]
        o_ref[:, cols] = (o * (gt * _sigmoid(gt))).astype(o_ref.dtype)
        if want_state:
            sf_ref[h] = st_scr[h].T
            sb_ref[h] = st_scr[HG_GROUP + h].T


def _hgrn(z, lb_f, lb_b, norm_g, row0, n_batch, seq, init_states=None, layer=None, state_bufs=None):
    n_groups = HG_HEADS // HG_GROUP
    r0 = row0 // seq
    has_init = init_states is not None
    want_state = state_bufs is not None

    def zspec(off):
        return pl.BlockSpec((seq, HG_GW), lambda b, g, o=off // HG_GW: (r0 + b, o + g))

    lbspec = pl.BlockSpec((1, HG_GW), lambda b, g: (0, g))
    in_specs = [zspec(HQ_OFF), zspec(HFF_OFF), zspec(HFB_OFF), zspec(HI_OFF), zspec(HGT_OFF),
                lbspec, lbspec, pl.BlockSpec((1, HG_DV), lambda b, g: (0, 0))]
    args = [z, z, z, z, z, lb_f.reshape(1, HG_DIM), lb_b.reshape(1, HG_DIM), norm_g.reshape(1, HG_DV)]
    if has_init:
        st_spec = pl.BlockSpec((None, None, HG_GROUP, HG_DK, HG_DV), lambda b, g: (b, layer, g, 0, 0))
        in_specs += [st_spec, st_spec]
        args += list(init_states)
    out_specs = [pl.BlockSpec((seq, HG_GW), lambda b, g: (b, g))]
    out_shape = [jax.ShapeDtypeStruct((n_batch * seq, HG_VDIM), BF16)]
    aliases = {}
    if want_state:
        so = pl.BlockSpec((None, None, HG_GROUP, HG_DK, HG_DV), lambda b, g: (layer, b, g, 0, 0))
        for buf in state_bufs:
            aliases[len(args)] = len(out_specs)
            in_specs.append(pl.BlockSpec(memory_space=pl.ANY))
            args.append(buf)
            out_specs.append(so)
            out_shape.append(jax.ShapeDtypeStruct(buf.shape, buf.dtype))
    return pl.pallas_call(
        functools.partial(_hgrn_kernel, seq=seq, has_init=has_init, want_state=want_state),
        grid=(n_batch, n_groups),
        in_specs=in_specs,
        out_specs=out_specs,
        out_shape=out_shape,
        scratch_shapes=[pltpu.VMEM((seq, HG_GW), F32), pltpu.VMEM((seq, HG_GW), F32),
                        pltpu.VMEM((seq, HG_GW), F32),
                        pltpu.VMEM((2 * HG_GROUP, HG_DV, HG_DK), F32)],
        input_output_aliases=aliases,
        compiler_params=_params(2),
        name=f"hgrn_t{seq}",
    )(*args)


def _oproj_kernel(*refs, n_x, n_ctx_tiles):
    attc_ref, attl_ref, hgc_ref, hgl_ref, w_ref = refs[:5]
    x_refs = refs[5:5 + n_x]
    mod_ref, g_ref, b_ref, o_ref = refs[5 + n_x:]
    is_ctx = pl.program_id(0) < n_ctx_tiles
    for r0 in range(0, OPROJ_TM, OPROJ_SUB):
        rows = slice(r0, r0 + OPROJ_SUB)
        att = jnp.where(is_ctx, attc_ref[rows], attl_ref[rows])
        hg = jnp.where(is_ctx, hgc_ref[rows], hgl_ref[rows])
        mix = (jnp.dot(att, w_ref[0:ATT_DIM, :], preferred_element_type=F32)
               + jnp.dot(hg, w_ref[ATT_DIM:MIX_DIM, :], preferred_element_type=F32))
        y = DEEPNORM_ALPHA * _pick_rows(x_refs, is_ctx, rows) + mod_ref[2:3, :] * mix
        o_ref[rows] = _layer_norm(y, g_ref[0:1, :], b_ref[0:1, :])


def _oproj(att_c, att_l, hg_c, hg_l, w_o, xs, mods, ln_g, ln_b, layer, mod_idx):
    n_tok = sum(x.shape[0] for x in xs)
    n_ctx_tiles = att_c.shape[0] // OPROJ_TM
    row = lambda i: (i, 0)
    ctx_row = lambda i: (jnp.minimum(i, n_ctx_tiles - 1), 0)
    lat_row = lambda i: (jnp.maximum(i - n_ctx_tiles, 0), 0)
    return pl.pallas_call(
        functools.partial(_oproj_kernel, n_x=len(xs), n_ctx_tiles=n_ctx_tiles),
        grid=(n_tok // OPROJ_TM,),
        in_specs=[pl.BlockSpec((OPROJ_TM, ATT_DIM), ctx_row),
                  pl.BlockSpec((OPROJ_TM, ATT_DIM), lat_row),
                  pl.BlockSpec((OPROJ_TM, HG_VDIM), ctx_row),
                  pl.BlockSpec((OPROJ_TM, HG_VDIM), lat_row),
                  pl.BlockSpec((MIX_DIM, D_MODEL), lambda i: (0, 0))]
        + _token_specs(xs, OPROJ_TM, n_ctx_tiles, 1) + [
                  pl.BlockSpec((None, None, 6, D_MODEL), lambda i: (layer, mod_idx(i), 0, 0)),
                  pl.BlockSpec((None, 2, D_MODEL), lambda i: (layer, 0, 0)),
                  pl.BlockSpec((None, 2, D_MODEL), lambda i: (layer, 0, 0))],
        out_specs=pl.BlockSpec((OPROJ_TM, D_MODEL), row),
        out_shape=jax.ShapeDtypeStruct((n_tok, D_MODEL), F32),
        compiler_params=_params(1),
        name="oproj_ln",
    )(att_c, att_l, hg_c, hg_l, w_o, *xs, mods, ln_g, ln_b)


def _mlp_kernel(*refs, n_cast, n_out, n_ctx_tiles):
    x_ref, mod_ref, wu_ref, wd_ref, g_ref, b_ref = refs[:6]
    cast_in = refs[6:6 + n_cast]
    outs = refs[6 + n_cast:6 + n_cast + n_out]
    cast_out = refs[6 + n_cast + n_out:6 + 2 * n_cast + n_out]
    h_ref, acc_ref = refs[6 + 2 * n_cast + n_out:]
    i, f = pl.program_id(0), pl.program_id(1)
    n_f = pl.num_programs(1)

    @pl.when(f == 0)
    def _():
        shift, scale = mod_ref[3:4, :], mod_ref[4:5, :]
        h_ref[...] = (x_ref[...] * (1.0 + scale) + shift).astype(BF16)
        acc_ref[...] = jnp.zeros_like(acc_ref)

    if n_cast:
        @pl.when(i * n_f + f < CAST_STEPS)
        def _():
            for src, dst in zip(cast_in, cast_out):
                dst[...] = src[...].astype(BF16)

    u = jnp.dot(h_ref[...], wu_ref[...], preferred_element_type=F32)
    u = jnp.square(jnp.maximum(u, 0.0)).astype(BF16)
    for n in range(0, D_MODEL, MLP_TN):
        acc_ref[:, n:n + MLP_TN] += jnp.dot(u, wd_ref[:, n:n + MLP_TN], preferred_element_type=F32)

    def finish(o_ref):
        y = DEEPNORM_ALPHA * x_ref[...] + mod_ref[5:6, :] * acc_ref[...]
        o_ref[...] = _layer_norm(y, g_ref[1:2, :], b_ref[1:2, :])

    last = f == n_f - 1
    if n_out == 1:
        pl.when(last)(lambda: finish(outs[0]))
    else:
        pl.when(jnp.logical_and(last, i < n_ctx_tiles))(lambda: finish(outs[0]))
        pl.when(jnp.logical_and(last, i >= n_ctx_tiles))(lambda: finish(outs[1]))


def _cast_specs(weights, layer):
    n_f = D_FF // MLP_TF
    in_specs, out_specs, out_shape = [], [], []
    for w in weights:
        _, rows, cols = w.shape
        slab = rows // CAST_STEPS
        in_specs.append(pl.BlockSpec(
            (None, slab, cols), lambda i, f: (layer, jnp.minimum(i * n_f + f, CAST_STEPS - 1), 0)))
        out_specs.append(pl.BlockSpec(
            (slab, cols), lambda i, f: (jnp.minimum(i * n_f + f, CAST_STEPS - 1), 0)))
        out_shape.append(jax.ShapeDtypeStruct((rows, cols), BF16))
    return in_specs, out_specs, out_shape


def _mlp(x, mods, w_up, w_down, ln_g, ln_b, layer, mod_idx, next_weights=(), split_rows=None):
    n_tok = x.shape[0]
    n_f = D_FF // MLP_TF
    assert (n_tok // MLP_TM) * n_f >= CAST_STEPS
    cast_in_specs, cast_out_specs, cast_shapes = _cast_specs(next_weights, layer + 1)
    if split_rows is None:
        n_ctx_tiles = n_tok // MLP_TM
        y_specs = [pl.BlockSpec((MLP_TM, D_MODEL), lambda i, f: (i, 0))]
        y_shapes = [jax.ShapeDtypeStruct((n_tok, D_MODEL), F32)]
    else:
        n_ctx_tiles = split_rows // MLP_TM
        y_specs = [pl.BlockSpec((MLP_TM, D_MODEL), lambda i, f: (jnp.minimum(i, n_ctx_tiles - 1), 0)),
                   pl.BlockSpec((MLP_TM, D_MODEL), lambda i, f: (jnp.maximum(i - n_ctx_tiles, 0), 0))]
        y_shapes = [jax.ShapeDtypeStruct((split_rows, D_MODEL), F32),
                    jax.ShapeDtypeStruct((n_tok - split_rows, D_MODEL), F32)]
    return pl.pallas_call(
        functools.partial(_mlp_kernel, n_cast=len(next_weights), n_out=len(y_specs),
                          n_ctx_tiles=n_ctx_tiles),
        grid=(n_tok // MLP_TM, n_f),
        in_specs=[pl.BlockSpec((MLP_TM, D_MODEL), lambda i, f: (i, 0)),
                  pl.BlockSpec((None, None, 6, D_MODEL), lambda i, f: (layer, mod_idx(i), 0, 0)),
                  pl.BlockSpec((D_MODEL, MLP_TF), lambda i, f: (0, f)),
                  pl.BlockSpec((MLP_TF, D_MODEL), lambda i, f: (f, 0)),
                  pl.BlockSpec((None, 2, D_MODEL), lambda i, f: (layer, 0, 0)),
                  pl.BlockSpec((None, 2, D_MODEL), lambda i, f: (layer, 0, 0))] + cast_in_specs,
        out_specs=y_specs + cast_out_specs,
        out_shape=y_shapes + cast_shapes,
        scratch_shapes=[pltpu.VMEM((MLP_TM, D_MODEL), BF16), pltpu.VMEM((MLP_TM, D_MODEL), F32)],
        compiler_params=_params(2),
        name="mlp_ln",
    )(x, mods, w_up, w_down, ln_g, ln_b, *next_weights)


def _cache_pack_kernel(*refs, depth):
    k_refs, v_refs = refs[:depth], refs[depth:2 * depth]
    ok_ref, ov_ref = refs[2 * depth:]
    for l in range(depth):
        for h in range(KV_HEADS):
            cols = slice(h * HEAD_DIM, (h + 1) * HEAD_DIM)
            ok_ref[0, l, :, h, :] = k_refs[l][:, cols]
            ov_ref[0, l, :, h, :] = v_refs[l][:, cols]


def _cache_pack(ks, vs, n_batch, seq):
    depth = len(ks)
    in_spec = pl.BlockSpec((seq, KV_DIM), lambda b: (b, 0))
    out_spec = pl.BlockSpec((1, depth, seq, KV_HEADS, HEAD_DIM), lambda b: (b, 0, 0, 0, 0))
    out_shape = jax.ShapeDtypeStruct((n_batch, depth, seq, KV_HEADS, HEAD_DIM), F32)
    return pl.pallas_call(
        functools.partial(_cache_pack_kernel, depth=depth),
        grid=(n_batch,),
        in_specs=[in_spec] * (2 * depth),
        out_specs=[out_spec, out_spec],
        out_shape=[out_shape, out_shape],
        compiler_params=_params(1),
        name="cache_pack",
    )(*ks, *vs)


def _lower_bounds(lb_logits):
    p = jax.nn.softmax(lb_logits.astype(F32), axis=0)
    cs = jnp.cumsum(p, axis=0)
    return cs - cs[0:1]


def kernel(x_prompt, x_sample, cache_k, cache_v, state_hgrn_fwd, state_hgrn_bwd, c, c_ctx, w_mod, b_mod,
           w_in, attn_sink, attn_norm_g, hg_lb_logits, hg_norm_g, w_o, ln_g, ln_b, w_up, w_down):
    bp, seq_p, d = x_prompt.shape
    bs, seq_s, _ = x_sample.shape
    depth = w_in.shape[0]
    past = cache_k.shape[2]
    n_ctx, n_lat = bp * seq_p, bs * seq_s

    xs = (x_prompt.reshape(n_ctx, d), x_sample.reshape(n_lat, d))
    c_rows = jnp.concatenate([c_ctx[None, :], c, jnp.zeros((MOD_ROWS - 1 - bs, d), F32)], 0)
    mods = _modulation(c_rows, w_mod, b_mod)[:, :1 + bs].reshape(depth, 1 + bs, 6, d)

    weights = (w_in, w_o, w_up, w_down)
    w_in_b, w_o_b, w_up_b, w_down_b = (w[0].astype(BF16) for w in weights)
    lb_f, lb_b = _lower_bounds(hg_lb_logits[0]), _lower_bounds(hg_lb_logits[1])
    rope_tabs = _rope_tables(seq_s)
    ck = cache_k.reshape(bs, depth, past, KV_DIM)
    cv = cache_v.reshape(bs, depth, past, KV_DIM)

    new_k, new_v = [], []
    s_f = jnp.zeros((depth, bp, HG_HEADS, HG_DK, HG_DV), F32)
    s_b = jnp.zeros((depth, bp, HG_HEADS, HG_DK, HG_DV), F32)
    for l in range(depth):
        z, k_c, v_c = _inproj(xs, mods, w_in_b, l, _mod_index(_inproj_tile(xs), n_ctx, seq_s), n_ctx)
        new_k.append(k_c)
        new_v.append(v_c)

        att_c = _ctx_attention(z, attn_sink[l], attn_norm_g[l], bp, seq_p)
        att_l = _lat_attention(z, ck, cv, l, attn_sink[l], attn_norm_g[l], rope_tabs, n_ctx, bs, seq_s)

        hg_c, s_f, s_b = _hgrn(z, lb_f[l], lb_b[l], hg_norm_g[l], 0, bp, seq_p, layer=l,
                               state_bufs=(s_f, s_b))
        (hg_l,) = _hgrn(z, lb_f[l], lb_b[l], hg_norm_g[l], n_ctx, bs, seq_s,
                        init_states=(state_hgrn_fwd, state_hgrn_bwd), layer=l)

        x = _oproj(att_c, att_l, hg_c, hg_l, w_o_b, xs, mods, ln_g, ln_b, l,
                   _mod_index(OPROJ_TM, n_ctx, seq_s))
        mlp_mods = _mod_index(MLP_TM, n_ctx, seq_s)
        if l + 1 < depth:
            x, w_in_b, w_o_b, w_up_b, w_down_b = _mlp(x, mods, w_up_b, w_down_b, ln_g, ln_b, l, mlp_mods,
                                                      next_weights=weights)
            xs = (x,)
        else:
            y_p, y_s = _mlp(x, mods, w_up_b, w_down_b, ln_g, ln_b, l, mlp_mods, split_rows=n_ctx)

    new_cache_k, new_cache_v = _cache_pack(new_k, new_v, bp, seq_p)
    return (y_p.reshape(bp, seq_p, d), y_s.reshape(bs, seq_s, d), new_cache_k, new_cache_v,
            jnp.swapaxes(s_f, 0, 1), jnp.swapaxes(s_b, 0, 1))
```

```python
import functools

import jax
import jax.numpy as jnp
from jax import lax
from jax.experimental import pallas as pl
from jax.experimental.pallas import tpu as pltpu

F32 = jnp.float32
BF16 = jnp.bfloat16

D_MODEL = 2048
DEPTH = 4
GRID_W = 64
HEAD_DIM = 128
ATT_HEADS = 8
KV_HEADS = 2
Q_PER_KV = ATT_HEADS // KV_HEADS
ATT_DIM = ATT_HEADS * HEAD_DIM
KV_DIM = KV_HEADS * HEAD_DIM
WINDOW = 128
BLOCK = 128
ATT_SCALE = HEAD_DIM ** -0.5
ROPE_THETA = 10000.0
ROPE_FREQS = HEAD_DIM // 4
HG_HEADS = 8
HG_DK = 128
HG_DV = 128
HG_DIM = HG_HEADS * HG_DK
HG_VDIM = HG_HEADS * HG_DV
HG_CHUNK = 32
MIX_DIM = ATT_DIM + HG_VDIM
IN_DIM = ATT_DIM + 2 * KV_DIM + 3 * HG_DIM + 2 * HG_VDIM
D_FF = 4 * D_MODEL
LN_EPS = 1e-5
RMS_EPS = 1e-6
DEEPNORM_ALPHA = (2 * DEPTH) ** 0.25
NEG_INF = -1e30
LOG2E = 1.4426950408889634

Q_OFF = 0
K_OFF = ATT_DIM
V_OFF = K_OFF + KV_DIM
HQ_OFF = V_OFF + KV_DIM
HFF_OFF = HQ_OFF + HG_DIM
HFB_OFF = HFF_OFF + HG_DIM
HI_OFF = HFB_OFF + HG_DIM
HGT_OFF = HI_OFF + HG_VDIM

V7X_VMEM_LIMIT_BYTES = 56 * 1024 * 1024

MOD_ROWS = 8
MOD_TN = 768
TOKEN_TILE = 1024
INPROJ_TN = 1664
OPROJ_TM = 512
OPROJ_SUB = 256
MLP_TM = 512
MLP_TF = 1024
MLP_TN = 512
CAST_STEPS = 128
HG_GROUP = 4
CTX_ATT_SEQS = 2
HG_SEQS = 2
HG_MAX_ROWS = 512
HG_GW = HG_GROUP * HG_DK

NT_DIMS = (((1,), (1,)), ((), ()))
TN_DIMS = (((0,), (0,)), ((), ()))


def _params(n_axes):
    return pltpu.CompilerParams(dimension_semantics=("arbitrary",) * n_axes,
                                vmem_limit_bytes=V7X_VMEM_LIMIT_BYTES)


def _sigmoid(x):
    return 1.0 / (1.0 + jnp.exp(-x))


def _layer_norm(y, g, b):
    mu = jnp.mean(y, -1, keepdims=True)
    yc = y - mu
    var = jnp.mean(yc * yc, -1, keepdims=True)
    return yc * lax.rsqrt(var + LN_EPS) * g + b


def _mod_kernel(c_ref, w_ref, b_ref, o_ref):
    c = c_ref[...]
    a = (c * _sigmoid(c)).astype(BF16)
    o_ref[0] = jnp.dot(a, w_ref[0].astype(BF16), preferred_element_type=F32) + b_ref[0]


def _modulation(c_rows, w_mod, b_mod):
    depth, d, n = w_mod.shape
    return pl.pallas_call(
        _mod_kernel,
        grid=(depth, n // MOD_TN),
        in_specs=[pl.BlockSpec((MOD_ROWS, d), lambda l, j: (0, 0)),
                  pl.BlockSpec((1, d, MOD_TN), lambda l, j: (l, 0, j)),
                  pl.BlockSpec((1, 1, MOD_TN), lambda l, j: (l, 0, j))],
        out_specs=pl.BlockSpec((1, MOD_ROWS, MOD_TN), lambda l, j: (l, 0, j)),
        out_shape=jax.ShapeDtypeStruct((depth, MOD_ROWS, n), F32),
        compiler_params=_params(2),
        name="modulation",
    )(c_rows, w_mod, b_mod.reshape(depth, 1, n))


def _mod_index(tile_rows, n_ctx_rows, lat_rows):
    n_ctx_tiles = n_ctx_rows // tile_rows
    per_lat = lat_rows // tile_rows

    def idx(i):
        return jnp.where(i < n_ctx_tiles, 0, 1 + (i - n_ctx_tiles) // per_lat)
    return idx


def _pick_rows(x_refs, is_ctx, rows=slice(None)):
    if len(x_refs) == 1:
        return x_refs[0][rows]
    return jnp.where(is_ctx, x_refs[0][rows], x_refs[1][rows])


def _inproj_kernel(*refs, n_x, n_ctx_tiles):
    x_refs = refs[:n_x]
    mod_ref, w_ref, z_ref, k_ref, v_ref, h_ref = refs[n_x:]
    i, j = pl.program_id(0), pl.program_id(1)

    @pl.when(j == 0)
    def _():
        shift, scale = mod_ref[0:1, :], mod_ref[1:2, :]
        h_ref[...] = (_pick_rows(x_refs, i < n_ctx_tiles) * (1.0 + scale) + shift).astype(BF16)

    z_ref[...] = jnp.dot(h_ref[...], w_ref[...], preferred_element_type=F32)

    @pl.when(jnp.logical_and(j == 0, i < n_ctx_tiles))
    def _():
        k_ref[...] = z_ref[:, K_OFF:K_OFF + KV_DIM]
        v_ref[...] = z_ref[:, V_OFF:V_OFF + KV_DIM]


def _token_specs(xs, tile, n_ctx_tiles, n_grid_axes):
    def spec(row_of):
        index_map = ((lambda i: (row_of(i), 0)) if n_grid_axes == 1 else (lambda i, j: (row_of(i), 0)))
        return pl.BlockSpec((tile, D_MODEL), index_map)
    if len(xs) == 1:
        return [spec(lambda i: i)]
    return [spec(lambda i: jnp.minimum(i, n_ctx_tiles - 1)), spec(lambda i: jnp.maximum(i - n_ctx_tiles, 0))]


def _inproj_tile(xs):
    return TOKEN_TILE // len(xs)


def _inproj(xs, mods, w_in, layer, mod_idx, n_ctx):
    assert V_OFF + KV_DIM <= INPROJ_TN
    tile = _inproj_tile(xs)
    n_tok = sum(x.shape[0] for x in xs)
    n_ctx_tiles = n_ctx // tile
    kv_spec = pl.BlockSpec((tile, KV_DIM), lambda i, j: (jnp.minimum(i, n_ctx_tiles - 1), 0))
    return pl.pallas_call(
        functools.partial(_inproj_kernel, n_x=len(xs), n_ctx_tiles=n_ctx_tiles),
        grid=(n_tok // tile, IN_DIM // INPROJ_TN),
        in_specs=_token_specs(xs, tile, n_ctx_tiles, 2) + [
                  pl.BlockSpec((None, None, 6, D_MODEL), lambda i, j: (layer, mod_idx(i), 0, 0)),
                  pl.BlockSpec((D_MODEL, INPROJ_TN), lambda i, j: (0, j))],
        out_specs=[pl.BlockSpec((tile, INPROJ_TN), lambda i, j: (i, j)), kv_spec, kv_spec],
        out_shape=[jax.ShapeDtypeStruct((n_tok, IN_DIM), F32),
                   jax.ShapeDtypeStruct((n_ctx, KV_DIM), F32),
                   jax.ShapeDtypeStruct((n_ctx, KV_DIM), F32)],
        scratch_shapes=[pltpu.VMEM((tile, D_MODEL), BF16)],
        compiler_params=_params(2),
        name="inproj",
    )(*xs, mods, w_in)


def _with_ones(v):
    return jnp.concatenate([v, jnp.ones_like(v)], axis=1)


def _sink_attend(scores, values, sink2):
    c = ATT_SCALE * LOG2E
    m_raw = jnp.max(scores[0], -1, keepdims=True)
    for s in scores[1:]:
        m_raw = jnp.maximum(m_raw, jnp.max(s, -1, keepdims=True))
    m2 = jnp.maximum(m_raw * c, sink2)
    acc = None
    for s, v1 in zip(scores, values):
        e = jnp.exp2(s * c - m2).astype(BF16)
        part = jnp.dot(e, v1, preferred_element_type=F32)
        acc = part if acc is None else acc + part
    den = acc[:, HEAD_DIM:HEAD_DIM + 1] + jnp.exp2(sink2 - m2)
    return acc[:, :HEAD_DIM] * (1.0 / den)


def _rms_norm_store(o_scr, sq, g_ref, o_ref):
    ss = jnp.sum(sq, -1, keepdims=True)
    inv = lax.rsqrt(ss * (1.0 / ATT_DIM) + RMS_EPS)
    o_ref[...] = (o_scr[...] * inv * g_ref[...]).astype(o_ref.dtype)


def _head_cols(g, r):
    h = g * Q_PER_KV + r
    return slice(h * HEAD_DIM, (h + 1) * HEAD_DIM)


def _group_sinks(sink_ref, g, rows):
    return jnp.concatenate([jnp.full((rows, 1), sink_ref[g * Q_PER_KV + r] * LOG2E, F32)
                            for r in range(Q_PER_KV)], axis=0)


def _scatter_heads(og, g, rows, sq, o_scr):
    for r in range(Q_PER_KV):
        oh = og[r * rows:(r + 1) * rows]
        sq = sq + oh * oh
        o_scr[:, _head_cols(g, r)] = oh
    return sq


def _ctx_attn_kernel(sink_ref, q_ref, k_ref, v_ref, g_ref, o_ref, o_scr, *, seq):
    c = ATT_SCALE * LOG2E
    kv_cols = [slice(g * HEAD_DIM, (g + 1) * HEAD_DIM) for g in range(KV_HEADS)]
    heads = [(g, r) for g in range(KV_HEADS) for r in range(Q_PER_KV)]
    for r0 in range(0, q_ref.shape[0], seq):
        rows = slice(r0, r0 + seq)
        sq = jnp.zeros((seq, HEAD_DIM), F32)
        ks = [k_ref[rows, cols].astype(BF16) for cols in kv_cols]
        vs = [v_ref[rows, cols].astype(BF16) for cols in kv_cols]
        scores = [lax.dot_general(q_ref[rows, _head_cols(g, r)].astype(BF16), ks[g], NT_DIMS,
                                  preferred_element_type=F32) for g, r in heads]
        probs = []
        for (g, r), s in zip(heads, scores):
            sink2 = sink_ref[g * Q_PER_KV + r] * LOG2E
            m2 = jnp.maximum(jnp.max(s, -1, keepdims=True) * c, sink2)
            e = jnp.exp2(s * c - m2)
            den = jnp.sum(e, -1, keepdims=True) + jnp.exp2(sink2 - m2)
            probs.append((e * (1.0 / den)).astype(BF16))
        outs = [jnp.dot(p, vs[g], preferred_element_type=F32) for (g, r), p in zip(heads, probs)]
        for (g, r), oh in zip(heads, outs):
            sq = sq + oh * oh
            o_scr[rows, _head_cols(g, r)] = oh
        _rms_norm_store(o_scr.at[rows], sq, g_ref, o_ref.at[rows])


def _ctx_attention(z, sink, norm_g, n_batch, seq):
    kb, vb = K_OFF // KV_DIM, V_OFF // KV_DIM
    rows = CTX_ATT_SEQS * seq
    return pl.pallas_call(
        functools.partial(_ctx_attn_kernel, seq=seq),
        grid=(n_batch // CTX_ATT_SEQS,),
        in_specs=[pl.BlockSpec(memory_space=pltpu.SMEM),
                  pl.BlockSpec((rows, ATT_DIM), lambda b: (b, 0)),
                  pl.BlockSpec((rows, KV_DIM), lambda b: (b, kb)),
                  pl.BlockSpec((rows, KV_DIM), lambda b: (b, vb)),
                  pl.BlockSpec((1, ATT_DIM), lambda b: (0, 0))],
        out_specs=pl.BlockSpec((rows, ATT_DIM), lambda b: (b, 0)),
        out_shape=jax.ShapeDtypeStruct((n_batch * seq, ATT_DIM), BF16),
        scratch_shapes=[pltpu.VMEM((rows, ATT_DIM), F32)],
        compiler_params=_params(1),
        name="ctx_attention",
    )(sink, z, z, z, norm_g.reshape(1, ATT_DIM))


def _rope(x, cos, sin_lo, sin_hi):
    return (x * cos + pltpu.roll(x, HEAD_DIM - ROPE_FREQS, 1) * sin_lo
            + pltpu.roll(x, ROPE_FREQS, 1) * sin_hi)


def _lat_attn_kernel(sink_ref, q_ref, k_ref, v_ref, ck_ref, cv_ref, cq_ref, slq_ref, shq_ref,
                     ca_ref, sla_ref, sha_ref, g_ref, o_ref, kr_scr, v_scr, ckb_scr, cvb_scr, o_scr):
    n = pl.program_id(1)
    seq = k_ref.shape[0]
    win = 3 * BLOCK

    @pl.when(n == 0)
    def _():
        for g in range(KV_HEADS):
            cols = slice(g * HEAD_DIM, (g + 1) * HEAD_DIM)
            kr_scr[:, cols] = _rope(k_ref[:, cols], ca_ref[...], sla_ref[...], sha_ref[...]).astype(BF16)
            v_scr[g] = _with_ones(v_ref[:, cols].astype(BF16))
            cvb_scr[g] = _with_ones(cv_ref[:, cols].astype(BF16))
        ckb_scr[...] = ck_ref[...].astype(BF16)

    k0 = pl.multiple_of(jnp.clip((n - 1) * BLOCK, 0, seq - win), BLOCK)
    rows = Q_PER_KV * BLOCK
    qpos = n * BLOCK + lax.broadcasted_iota(jnp.int32, (rows, win), 0) % BLOCK
    kpos = k0 + lax.broadcasted_iota(jnp.int32, (rows, win), 1)
    valid = jnp.abs(kpos - qpos) <= WINDOW
    sq = jnp.zeros((BLOCK, HEAD_DIM), F32)
    for g in range(KV_HEADS):
        kv_cols = slice(g * HEAD_DIM, (g + 1) * HEAD_DIM)
        k_loc = kr_scr[pl.ds(k0, win), kv_cols]
        v_loc = v_scr[g, pl.ds(k0, win), :]
        k_ctx = ckb_scr[:, kv_cols]
        v_ctx = cvb_scr[g]
        qg = jnp.concatenate(
            [_rope(q_ref[:, _head_cols(g, r)], cq_ref[...], slq_ref[...], shq_ref[...]).astype(BF16)
             for r in range(Q_PER_KV)], axis=0)
        s_loc = lax.dot_general(qg, k_loc, NT_DIMS, preferred_element_type=F32)
        s_loc = jnp.where(valid, s_loc, NEG_INF / ATT_SCALE)
        s_ctx = lax.dot_general(qg, k_ctx, NT_DIMS, preferred_element_type=F32)
        og = _sink_attend([s_loc, s_ctx], [v_loc, v_ctx], _group_sinks(sink_ref, g, BLOCK))
        sq = _scatter_heads(og, g, BLOCK, sq, o_scr)
    _rms_norm_store(o_scr, sq, g_ref, o_ref)


def _lat_attention(z, cache_k, cache_v, layer, sink, norm_g, rope_tabs, row0, n_batch, seq):
    kb, vb = K_OFF // KV_DIM, V_OFF // KV_DIM
    nq = seq // BLOCK
    qrow0, srow0 = row0 // BLOCK, row0 // seq
    past = cache_k.shape[2]
    cos, sin_lo, sin_hi = rope_tabs
    qtab = pl.BlockSpec((BLOCK, HEAD_DIM), lambda b, n: (n, 0))
    atab = pl.BlockSpec((seq, HEAD_DIM), lambda b, n: (0, 0))
    cache_spec = pl.BlockSpec((None, None, past, KV_DIM), lambda b, n: (b, layer, 0, 0))
    return pl.pallas_call(
        _lat_attn_kernel,
        grid=(n_batch, nq),
        in_specs=[pl.BlockSpec(memory_space=pltpu.SMEM),
                  pl.BlockSpec((BLOCK, ATT_DIM), lambda b, n: (qrow0 + b * nq + n, 0)),
                  pl.BlockSpec((seq, KV_DIM), lambda b, n: (srow0 + b, kb)),
                  pl.BlockSpec((seq, KV_DIM), lambda b, n: (srow0 + b, vb)),
                  cache_spec, cache_spec, qtab, qtab, qtab, atab, atab, atab,
                  pl.BlockSpec((1, ATT_DIM), lambda b, n: (0, 0))],
        out_specs=pl.BlockSpec((BLOCK, ATT_DIM), lambda b, n: (b * nq + n, 0)),
        out_shape=jax.ShapeDtypeStruct((n_batch * seq, ATT_DIM), BF16),
        scratch_shapes=[pltpu.VMEM((seq, KV_DIM), BF16), pltpu.VMEM((KV_HEADS, seq, 2 * HEAD_DIM), BF16),
                        pltpu.VMEM((past, KV_DIM), BF16), pltpu.VMEM((KV_HEADS, past, 2 * HEAD_DIM), BF16),
                        pltpu.VMEM((BLOCK, ATT_DIM), F32)],
        compiler_params=_params(2),
        name="lat_attention",
    )(sink, z, z, z, cache_k, cache_v, cos, sin_lo, sin_hi, cos, sin_lo, sin_hi,
      norm_g.reshape(1, ATT_DIM))


def _rope_tables(seq):
    rows = seq // GRID_W
    row = jnp.repeat(jnp.arange(rows, dtype=F32), GRID_W)
    col = jnp.tile(jnp.arange(GRID_W, dtype=F32), rows)
    inv = ROPE_THETA ** (-jnp.arange(ROPE_FREQS, dtype=F32) / ROPE_FREQS)
    ang_r, ang_c = row[:, None] * inv, col[:, None] * inv
    cr, sr, cc, sc = jnp.cos(ang_r), jnp.sin(ang_r), jnp.cos(ang_c), jnp.sin(ang_c)
    zero = jnp.zeros_like(sr)
    cos = jnp.concatenate([cr, cr, cc, cc], -1)
    sin_lo = jnp.concatenate([-sr, zero, -sc, zero], -1)
    sin_hi = jnp.concatenate([zero, sr, zero, sc], -1)
    return cos, sin_lo, sin_hi


def _chunk_cumsum(x, rows, reverse):
    s = 1
    while s < HG_CHUNK:
        if reverse:
            x = x + jnp.where(rows < HG_CHUNK - s, pltpu.roll(x, HG_CHUNK - s, 0), 0.0)
        else:
            x = x + jnp.where(rows >= s, pltpu.roll(x, s, 0), 0.0)
        s *= 2
    return x


def _hgrn_kernel(*refs, seq, n_seq, has_init, want_state):
    hq_ref, hff_ref, hfb_ref, hi_ref, hgt_ref, lbf_ref, lbb_ref, ng_ref = refs[:8]
    pos = 8
    if has_init:
        sf0_ref, sb0_ref = refs[pos:pos + 2]
        pos += 2
    if want_state:
        pos += 2
    o_ref = refs[pos]
    pos += 1
    if want_state:
        sf_ref, sb_ref = refs[pos:pos + 2]
        pos += 2
    q_scr, of_scr, ob_scr, st_scr = refs[pos:pos + 4]

    C = HG_CHUNK
    n_chunks = seq // C
    q_in = hq_ref[...]
    q_scr[...] = q_in * _sigmoid(q_in)
    def state_slot(s, h, d):
        return (s * 2 + d) * HG_GROUP + h

    for s in range(n_seq):
        for h in range(HG_GROUP):
            if has_init:
                st_scr[state_slot(s, h, 0)] = sf0_ref[s, h].T
                st_scr[state_slot(s, h, 1)] = sb0_ref[s, h].T
            else:
                st_scr[state_slot(s, h, 0)] = jnp.zeros((HG_DV, HG_DK), F32)
                st_scr[state_slot(s, h, 1)] = jnp.zeros((HG_DV, HG_DK), F32)

    rows = lax.broadcasted_iota(jnp.int32, (C, HG_DK), 0)
    ti = lax.broadcasted_iota(jnp.int32, (2 * C, 2 * C), 0)
    si = lax.broadcasted_iota(jnp.int32, (2 * C, 2 * C), 1)
    same_chunk = (ti // C) == (si // C)
    p1_sees_p0 = jnp.logical_and(ti >= C, si < C)
    pair_mask_f = jnp.logical_or(jnp.logical_and(same_chunk, si <= ti), p1_sees_p0)
    pair_mask_b = jnp.logical_or(jnp.logical_and(same_chunk, si >= ti), p1_sees_p0)
    dirs = ((hff_ref, lbf_ref, of_scr, False, pair_mask_f, C // 2 - 1, C - 1),
            (hfb_ref, lbb_ref, ob_scr, True, pair_mask_b, C // 2, 0))
    n_pairs = n_chunks // 2
    zeros = jnp.zeros((C, HG_DK), BF16)

    def chunk_prep(z_ref, lb_ref, reverse, mid, last, rs, cols):
        q = q_scr[rs, cols]
        v = hi_ref[rs, cols].astype(BF16)
        lb = lb_ref[:, cols]
        sig = _sigmoid(z_ref[rs, cols])
        log_f = jnp.log(lb + (1.0 - lb) * sig)
        k = (1.0 - lb) * (1.0 - sig)
        b = _chunk_cumsum(log_f, rows, reverse)
        b_mid = b[mid:mid + 1, :]
        b_last = b[last:last + 1, :]
        e_mid = jnp.exp(b - b_mid)
        qe = q * e_mid
        ke = k * (1.0 / e_mid)
        return dict(v=v, qe=qe.astype(BF16), ke=ke.astype(BF16), qb=qe * jnp.exp(b_mid),
                    kl=ke * jnp.exp(b_last - b_mid), decay=jnp.exp(b_last))

    def pair_step(c, carry):
        f0 = c * 2 * C
        b1 = (n_pairs - 1 - c) * 2 * C
        pair_starts = ((f0, f0 + C), (b1 + C, b1))

        def row_start(s, d, p):
            return pl.multiple_of(s * seq + pair_starts[d][p], C)

        streams = [(s, h, d) for s in range(n_seq) for h in range(HG_GROUP) for d in range(2)]
        prep = []
        for s, h, d in streams:
            z_ref, lb_ref, _, reverse, _, mid, last = dirs[d]
            cols = slice(h * HG_DK, (h + 1) * HG_DK)
            p0, p1 = (chunk_prep(z_ref, lb_ref, reverse, mid, last, pl.ds(row_start(s, d, p), C), cols)
                      for p in range(2))
            cat = lambda a, b: jnp.concatenate([a, b], axis=0)
            prep.append(dict(
                v=cat(p0["v"], p1["v"]),
                a_lhs=cat(jnp.concatenate([p0["qe"], zeros, zeros], axis=1),
                          jnp.concatenate([zeros, p1["qb"].astype(BF16), p1["qe"]], axis=1)),
                a_rhs=cat(jnp.concatenate([p0["ke"], p0["kl"].astype(BF16), zeros], axis=1),
                          jnp.concatenate([zeros, zeros, p1["ke"]], axis=1)),
                qb=cat(p0["qb"], p1["qb"] * p0["decay"]).astype(BF16),
                kl=cat(p0["kl"] * p1["decay"], p1["kl"]).astype(BF16),
                decay=p0["decay"] * p1["decay"]))
        a_raw = [lax.dot_general(p["a_lhs"], p["a_rhs"], NT_DIMS, preferred_element_type=F32) for p in prep]
        states = [st_scr[state_slot(s, h, d)] for s, h, d in streams]
        o_inter = [lax.dot_general(p["qb"], st.astype(BF16), NT_DIMS, preferred_element_type=F32)
                   for p, st in zip(prep, states)]
        u_t = [lax.dot_general(p["v"], p["kl"], TN_DIMS, preferred_element_type=F32) for p in prep]
        o_intra = [jnp.dot(jnp.where(dirs[d][4], a, 0.0).astype(BF16), p["v"], preferred_element_type=F32)
                   for (s, h, d), a, p in zip(streams, a_raw, prep)]
        for i, (s, h, d) in enumerate(streams):
            cols = slice(h * HG_DK, (h + 1) * HG_DK)
            o = o_intra[i] + o_inter[i]
            dirs[d][2][pl.ds(row_start(s, d, 0), C), cols] = o[:C]
            dirs[d][2][pl.ds(row_start(s, d, 1), C), cols] = o[C:]
            st_scr[state_slot(s, h, d)] = states[i] * prep[i]["decay"] + u_t[i]
        return carry

    lax.fori_loop(0, n_pairs, pair_step, 0, unroll=2 // n_seq)

    for h in range(HG_GROUP):
        cols = slice(h * HG_DK, (h + 1) * HG_DK)
        o = of_scr[:, cols] + ob_scr[:, cols]
        o = o * lax.rsqrt(jnp.mean(o * o, -1, keepdims=True) + RMS_EPS) * ng_ref[...]
        gt = hgt_ref[:, cols]
        o_ref[:, cols] = (o * (gt * _sigmoid(gt))).astype(o_ref.dtype)
        if want_state:
            for s in range(n_seq):
                sf_ref[s, h] = st_scr[state_slot(s, h, 0)].T
                sb_ref[s, h] = st_scr[state_slot(s, h, 1)].T


def _hgrn(z, lb_f, lb_b, norm_g, row0, n_batch, seq, init_states=None, layer=None, state_bufs=None):
    n_groups = HG_HEADS // HG_GROUP
    n_seq = HG_SEQS if seq * HG_SEQS <= HG_MAX_ROWS else 1
    rows = n_seq * seq
    r0 = row0 // rows
    has_init = init_states is not None
    want_state = state_bufs is not None

    def zspec(off):
        return pl.BlockSpec((rows, HG_GW), lambda b, g, o=off // HG_GW: (r0 + b, o + g))

    lbspec = pl.BlockSpec((1, HG_GW), lambda b, g: (0, g))
    in_specs = [zspec(HQ_OFF), zspec(HFF_OFF), zspec(HFB_OFF), zspec(HI_OFF), zspec(HGT_OFF),
                lbspec, lbspec, pl.BlockSpec((1, HG_DV), lambda b, g: (0, 0))]
    args = [z, z, z, z, z, lb_f.reshape(1, HG_DIM), lb_b.reshape(1, HG_DIM), norm_g.reshape(1, HG_DV)]
    if has_init:
        st_spec = pl.BlockSpec((n_seq, None, HG_GROUP, HG_DK, HG_DV), lambda b, g: (b, layer, g, 0, 0))
        in_specs += [st_spec, st_spec]
        args += list(init_states)
    out_specs = [pl.BlockSpec((rows, HG_GW), lambda b, g: (b, g))]
    out_shape = [jax.ShapeDtypeStruct((n_batch * seq, HG_VDIM), BF16)]
    aliases = {}
    if want_state:
        so = pl.BlockSpec((None, n_seq, HG_GROUP, HG_DK, HG_DV), lambda b, g: (layer, b, g, 0, 0))
        for buf in state_bufs:
            aliases[len(args)] = len(out_specs)
            in_specs.append(pl.BlockSpec(memory_space=pl.ANY))
            args.append(buf)
            out_specs.append(so)
            out_shape.append(jax.ShapeDtypeStruct(buf.shape, buf.dtype))
    return pl.pallas_call(
        functools.partial(_hgrn_kernel, seq=seq, n_seq=n_seq, has_init=has_init, want_state=want_state),
        grid=(n_batch // n_seq, n_groups),
        in_specs=in_specs,
        out_specs=out_specs,
        out_shape=out_shape,
        scratch_shapes=[pltpu.VMEM((rows, HG_GW), F32), pltpu.VMEM((rows, HG_GW), F32),
                        pltpu.VMEM((rows, HG_GW), F32),
                        pltpu.VMEM((n_seq * 2 * HG_GROUP, HG_DV, HG_DK), F32)],
        input_output_aliases=aliases,
        compiler_params=_params(2),
        name=f"hgrn_t{seq}",
    )(*args)


def _oproj_kernel(*refs, n_x, n_ctx_tiles):
    attc_ref, attl_ref, hgc_ref, hgl_ref, w_ref = refs[:5]
    x_refs = refs[5:5 + n_x]
    mod_ref, g_ref, b_ref, o_ref = refs[5 + n_x:]
    is_ctx = pl.program_id(0) < n_ctx_tiles
    for r0 in range(0, OPROJ_TM, OPROJ_SUB):
        rows = slice(r0, r0 + OPROJ_SUB)
        att = jnp.where(is_ctx, attc_ref[rows], attl_ref[rows])
        hg = jnp.where(is_ctx, hgc_ref[rows], hgl_ref[rows])
        mix = (jnp.dot(att, w_ref[0:ATT_DIM, :], preferred_element_type=F32)
               + jnp.dot(hg, w_ref[ATT_DIM:MIX_DIM, :], preferred_element_type=F32))
        y = DEEPNORM_ALPHA * _pick_rows(x_refs, is_ctx, rows) + mod_ref[2:3, :] * mix
        o_ref[rows] = _layer_norm(y, g_ref[0:1, :], b_ref[0:1, :])


def _oproj(att_c, att_l, hg_c, hg_l, w_o, xs, mods, ln_g, ln_b, layer, mod_idx):
    n_tok = sum(x.shape[0] for x in xs)
    n_ctx_tiles = att_c.shape[0] // OPROJ_TM
    row = lambda i: (i, 0)
    ctx_row = lambda i: (jnp.minimum(i, n_ctx_tiles - 1), 0)
    lat_row = lambda i: (jnp.maximum(i - n_ctx_tiles, 0), 0)
    return pl.pallas_call(
        functools.partial(_oproj_kernel, n_x=len(xs), n_ctx_tiles=n_ctx_tiles),
        grid=(n_tok // OPROJ_TM,),
        in_specs=[pl.BlockSpec((OPROJ_TM, ATT_DIM), ctx_row),
                  pl.BlockSpec((OPROJ_TM, ATT_DIM), lat_row),
                  pl.BlockSpec((OPROJ_TM, HG_VDIM), ctx_row),
                  pl.BlockSpec((OPROJ_TM, HG_VDIM), lat_row),
                  pl.BlockSpec((MIX_DIM, D_MODEL), lambda i: (0, 0))]
        + _token_specs(xs, OPROJ_TM, n_ctx_tiles, 1) + [
                  pl.BlockSpec((None, None, 6, D_MODEL), lambda i: (layer, mod_idx(i), 0, 0)),
                  pl.BlockSpec((None, 2, D_MODEL), lambda i: (layer, 0, 0)),
                  pl.BlockSpec((None, 2, D_MODEL), lambda i: (layer, 0, 0))],
        out_specs=pl.BlockSpec((OPROJ_TM, D_MODEL), row),
        out_shape=jax.ShapeDtypeStruct((n_tok, D_MODEL), F32),
        compiler_params=_params(1),
        name="oproj_ln",
    )(att_c, att_l, hg_c, hg_l, w_o, *xs, mods, ln_g, ln_b)


def _mlp_kernel(*refs, n_cast, n_out, n_ctx_tiles):
    x_ref, mod_ref, wu_ref, wd_ref, g_ref, b_ref = refs[:6]
    cast_in = refs[6:6 + n_cast]
    outs = refs[6 + n_cast:6 + n_cast + n_out]
    cast_out = refs[6 + n_cast + n_out:6 + 2 * n_cast + n_out]
    h_ref, acc_ref = refs[6 + 2 * n_cast + n_out:]
    i, f = pl.program_id(0), pl.program_id(1)
    n_f = pl.num_programs(1)

    @pl.when(f == 0)
    def _():
        shift, scale = mod_ref[3:4, :], mod_ref[4:5, :]
        h_ref[...] = (x_ref[...] * (1.0 + scale) + shift).astype(BF16)
        acc_ref[...] = jnp.zeros_like(acc_ref)

    if n_cast:
        @pl.when(i * n_f + f < CAST_STEPS)
        def _():
            for src, dst in zip(cast_in, cast_out):
                dst[...] = src[...].astype(BF16)

    u = jnp.dot(h_ref[...], wu_ref[...], preferred_element_type=F32)
    u = jnp.square(jnp.maximum(u, 0.0)).astype(BF16)
    for n in range(0, D_MODEL, MLP_TN):
        acc_ref[:, n:n + MLP_TN] += jnp.dot(u, wd_ref[:, n:n + MLP_TN], preferred_element_type=F32)

    def finish(o_ref):
        y = DEEPNORM_ALPHA * x_ref[...] + mod_ref[5:6, :] * acc_ref[...]
        o_ref[...] = _layer_norm(y, g_ref[1:2, :], b_ref[1:2, :])

    last = f == n_f - 1
    if n_out == 1:
        pl.when(last)(lambda: finish(outs[0]))
    else:
        pl.when(jnp.logical_and(last, i < n_ctx_tiles))(lambda: finish(outs[0]))
        pl.when(jnp.logical_and(last, i >= n_ctx_tiles))(lambda: finish(outs[1]))


def _cast_specs(weights, layer):
    n_f = D_FF // MLP_TF
    in_specs, out_specs, out_shape = [], [], []
    for w in weights:
        _, rows, cols = w.shape
        slab = rows // CAST_STEPS
        in_specs.append(pl.BlockSpec(
            (None, slab, cols), lambda i, f: (layer, jnp.minimum(i * n_f + f, CAST_STEPS - 1), 0)))
        out_specs.append(pl.BlockSpec(
            (slab, cols), lambda i, f: (jnp.minimum(i * n_f + f, CAST_STEPS - 1), 0)))
        out_shape.append(jax.ShapeDtypeStruct((rows, cols), BF16))
    return in_specs, out_specs, out_shape


def _mlp(x, mods, w_up, w_down, ln_g, ln_b, layer, mod_idx, next_weights=(), split_rows=None):
    n_tok = x.shape[0]
    n_f = D_FF // MLP_TF
    assert (n_tok // MLP_TM) * n_f >= CAST_STEPS
    cast_in_specs, cast_out_specs, cast_shapes = _cast_specs(next_weights, layer + 1)
    if split_rows is None:
        n_ctx_tiles = n_tok // MLP_TM
        y_specs = [pl.BlockSpec((MLP_TM, D_MODEL), lambda i, f: (i, 0))]
        y_shapes = [jax.ShapeDtypeStruct((n_tok, D_MODEL), F32)]
    else:
        n_ctx_tiles = split_rows // MLP_TM
        y_specs = [pl.BlockSpec((MLP_TM, D_MODEL), lambda i, f: (jnp.minimum(i, n_ctx_tiles - 1), 0)),
                   pl.BlockSpec((MLP_TM, D_MODEL), lambda i, f: (jnp.maximum(i - n_ctx_tiles, 0), 0))]
        y_shapes = [jax.ShapeDtypeStruct((split_rows, D_MODEL), F32),
                    jax.ShapeDtypeStruct((n_tok - split_rows, D_MODEL), F32)]
    return pl.pallas_call(
        functools.partial(_mlp_kernel, n_cast=len(next_weights), n_out=len(y_specs),
                          n_ctx_tiles=n_ctx_tiles),
        grid=(n_tok // MLP_TM, n_f),
        in_specs=[pl.BlockSpec((MLP_TM, D_MODEL), lambda i, f: (i, 0)),
                  pl.BlockSpec((None, None, 6, D_MODEL), lambda i, f: (layer, mod_idx(i), 0, 0)),
                  pl.BlockSpec((D_MODEL, MLP_TF), lambda i, f: (0, f)),
                  pl.BlockSpec((MLP_TF, D_MODEL), lambda i, f: (f, 0)),
                  pl.BlockSpec((None, 2, D_MODEL), lambda i, f: (layer, 0, 0)),
                  pl.BlockSpec((None, 2, D_MODEL), lambda i, f: (layer, 0, 0))] + cast_in_specs,
        out_specs=y_specs + cast_out_specs,
        out_shape=y_shapes + cast_shapes,
        scratch_shapes=[pltpu.VMEM((MLP_TM, D_MODEL), BF16), pltpu.VMEM((MLP_TM, D_MODEL), F32)],
        compiler_params=_params(2),
        name="mlp_ln",
    )(x, mods, w_up, w_down, ln_g, ln_b, *next_weights)


def _cache_pack_kernel(*refs, depth):
    k_refs, v_refs = refs[:depth], refs[depth:2 * depth]
    ok_ref, ov_ref = refs[2 * depth:]
    for l in range(depth):
        for h in range(KV_HEADS):
            cols = slice(h * HEAD_DIM, (h + 1) * HEAD_DIM)
            ok_ref[0, l, :, h, :] = k_refs[l][:, cols]
            ov_ref[0, l, :, h, :] = v_refs[l][:, cols]


def _cache_pack(ks, vs, n_batch, seq):
    depth = len(ks)
    in_spec = pl.BlockSpec((seq, KV_DIM), lambda b: (b, 0))
    out_spec = pl.BlockSpec((1, depth, seq, KV_HEADS, HEAD_DIM), lambda b: (b, 0, 0, 0, 0))
    out_shape = jax.ShapeDtypeStruct((n_batch, depth, seq, KV_HEADS, HEAD_DIM), F32)
    return pl.pallas_call(
        functools.partial(_cache_pack_kernel, depth=depth),
        grid=(n_batch,),
        in_specs=[in_spec] * (2 * depth),
        out_specs=[out_spec, out_spec],
        out_shape=[out_shape, out_shape],
        compiler_params=_params(1),
        name="cache_pack",
    )(*ks, *vs)


def _lower_bounds(lb_logits):
    p = jax.nn.softmax(lb_logits.astype(F32), axis=0)
    cs = jnp.cumsum(p, axis=0)
    return cs - cs[0:1]


def kernel(x_prompt, x_sample, cache_k, cache_v, state_hgrn_fwd, state_hgrn_bwd, c, c_ctx, w_mod, b_mod,
           w_in, attn_sink, attn_norm_g, hg_lb_logits, hg_norm_g, w_o, ln_g, ln_b, w_up, w_down):
    bp, seq_p, d = x_prompt.shape
    bs, seq_s, _ = x_sample.shape
    depth = w_in.shape[0]
    past = cache_k.shape[2]
    n_ctx, n_lat = bp * seq_p, bs * seq_s

    xs = (x_prompt.reshape(n_ctx, d), x_sample.reshape(n_lat, d))
    c_rows = jnp.concatenate([c_ctx[None, :], c, jnp.zeros((MOD_ROWS - 1 - bs, d), F32)], 0)
    mods = _modulation(c_rows, w_mod, b_mod)[:, :1 + bs].reshape(depth, 1 + bs, 6, d)

    weights = (w_in, w_o, w_up, w_down)
    w_in_b, w_o_b, w_up_b, w_down_b = (w[0].astype(BF16) for w in weights)
    lb_f, lb_b = _lower_bounds(hg_lb_logits[0]), _lower_bounds(hg_lb_logits[1])
    rope_tabs = _rope_tables(seq_s)
    ck = cache_k.reshape(bs, depth, past, KV_DIM)
    cv = cache_v.reshape(bs, depth, past, KV_DIM)

    new_k, new_v = [], []
    s_f = jnp.zeros((depth, bp, HG_HEADS, HG_DK, HG_DV), F32)
    s_b = jnp.zeros((depth, bp, HG_HEADS, HG_DK, HG_DV), F32)
    for l in range(depth):
        z, k_c, v_c = _inproj(xs, mods, w_in_b, l, _mod_index(_inproj_tile(xs), n_ctx, seq_s), n_ctx)
        new_k.append(k_c)
        new_v.append(v_c)

        att_c = _ctx_attention(z, attn_sink[l], attn_norm_g[l], bp, seq_p)
        att_l = _lat_attention(z, ck, cv, l, attn_sink[l], attn_norm_g[l], rope_tabs, n_ctx, bs, seq_s)

        hg_c, s_f, s_b = _hgrn(z, lb_f[l], lb_b[l], hg_norm_g[l], 0, bp, seq_p, layer=l,
                               state_bufs=(s_f, s_b))
        (hg_l,) = _hgrn(z, lb_f[l], lb_b[l], hg_norm_g[l], n_ctx, bs, seq_s,
                        init_states=(state_hgrn_fwd, state_hgrn_bwd), layer=l)

        x = _oproj(att_c, att_l, hg_c, hg_l, w_o_b, xs, mods, ln_g, ln_b, l,
                   _mod_index(OPROJ_TM, n_ctx, seq_s))
        mlp_mods = _mod_index(MLP_TM, n_ctx, seq_s)
        if l + 1 < depth:
            x, w_in_b, w_o_b, w_up_b, w_down_b = _mlp(x, mods, w_up_b, w_down_b, ln_g, ln_b, l, mlp_mods,
                                                      next_weights=weights)
            xs = (x,)
        else:
            y_p, y_s = _mlp(x, mods, w_up_b, w_down_b, ln_g, ln_b, l, mlp_mods, split_rows=n_ctx)

    new_cache_k, new_cache_v = _cache_pack(new_k, new_v, bp, seq_p)
    return (y_p.reshape(bp, seq_p, d), y_s.reshape(bs, seq_s, d), new_cache_k, new_cache_v,
            jnp.swapaxes(s_f, 0, 1), jnp.swapaxes(s_b, 0, 1))
```

```python
import functools

import jax
import jax.numpy as jnp
from jax import lax
from jax.experimental import pallas as pl
from jax.experimental.pallas import tpu as pltpu

F32 = jnp.float32
BF16 = jnp.bfloat16

D_MODEL = 2048
DEPTH = 4
GRID_W = 64
HEAD_DIM = 128
ATT_HEADS = 8
KV_HEADS = 2
Q_PER_KV = ATT_HEADS // KV_HEADS
ATT_DIM = ATT_HEADS * HEAD_DIM
KV_DIM = KV_HEADS * HEAD_DIM
WINDOW = 128
BLOCK = 128
ATT_SCALE = HEAD_DIM ** -0.5
ROPE_THETA = 10000.0
ROPE_FREQS = HEAD_DIM // 4
HG_HEADS = 8
HG_DK = 128
HG_DV = 128
HG_DIM = HG_HEADS * HG_DK
HG_VDIM = HG_HEADS * HG_DV
HG_CHUNK = 32
MIX_DIM = ATT_DIM + HG_VDIM
IN_DIM = ATT_DIM + 2 * KV_DIM + 3 * HG_DIM + 2 * HG_VDIM
D_FF = 4 * D_MODEL
LN_EPS = 1e-5
RMS_EPS = 1e-6
DEEPNORM_ALPHA = (2 * DEPTH) ** 0.25
NEG_INF = -1e30
LOG2E = 1.4426950408889634

Q_OFF = 0
K_OFF = ATT_DIM
V_OFF = K_OFF + KV_DIM
HQ_OFF = V_OFF + KV_DIM
HFF_OFF = HQ_OFF + HG_DIM
HFB_OFF = HFF_OFF + HG_DIM
HI_OFF = HFB_OFF + HG_DIM
HGT_OFF = HI_OFF + HG_VDIM

V7X_VMEM_LIMIT_BYTES = 56 * 1024 * 1024

MOD_ROWS = 8
MOD_TN = 1536
TOKEN_TILE = 1024
INPROJ_TN = 1664
OPROJ_TM = 512
OPROJ_SUB = 256
MLP_TM = 512
MLP_TF = 1024
MLP_TN = 512
CAST_STEPS = 128
HG_GROUP = 4
CTX_ATT_SEQS = 4
LAT_Q_ROWS = 256
HG_GW = HG_GROUP * HG_DK

NT_DIMS = (((1,), (1,)), ((), ()))
TN_DIMS = (((0,), (0,)), ((), ()))


def _params(n_axes):
    return pltpu.CompilerParams(dimension_semantics=("arbitrary",) * n_axes,
                                vmem_limit_bytes=V7X_VMEM_LIMIT_BYTES)


def _sigmoid(x):
    return 1.0 / (1.0 + jnp.exp(-x))


def _layer_norm(y, g, b):
    mu = jnp.mean(y, -1, keepdims=True)
    yc = y - mu
    var = jnp.mean(yc * yc, -1, keepdims=True)
    return yc * lax.rsqrt(var + LN_EPS) * g + b


def _mod_kernel(c_ref, w_ref, b_ref, o_ref):
    c = c_ref[...]
    a = (c * _sigmoid(c)).astype(BF16)
    o_ref[0] = jnp.dot(a, w_ref[0].astype(BF16), preferred_element_type=F32) + b_ref[0]


def _modulation(c_rows, w_mod, b_mod):
    depth, d, n = w_mod.shape
    return pl.pallas_call(
        _mod_kernel,
        grid=(depth, n // MOD_TN),
        in_specs=[pl.BlockSpec((MOD_ROWS, d), lambda l, j: (0, 0)),
                  pl.BlockSpec((1, d, MOD_TN), lambda l, j: (l, 0, j)),
                  pl.BlockSpec((1, 1, MOD_TN), lambda l, j: (l, 0, j))],
        out_specs=pl.BlockSpec((1, MOD_ROWS, MOD_TN), lambda l, j: (l, 0, j)),
        out_shape=jax.ShapeDtypeStruct((depth, MOD_ROWS, n), F32),
        compiler_params=_params(2),
        name="modulation",
    )(c_rows, w_mod, b_mod.reshape(depth, 1, n))


def _mod_index(tile_rows, n_ctx_rows, lat_rows):
    n_ctx_tiles = n_ctx_rows // tile_rows
    per_lat = lat_rows // tile_rows

    def idx(i):
        return jnp.where(i < n_ctx_tiles, 0, 1 + (i - n_ctx_tiles) // per_lat)
    return idx


def _pick_rows(x_refs, is_ctx, rows=slice(None)):
    if len(x_refs) == 1:
        return x_refs[0][rows]
    return jnp.where(is_ctx, x_refs[0][rows], x_refs[1][rows])


def _inproj_kernel(*refs, n_x, n_ctx_tiles):
    x_refs = refs[:n_x]
    mod_ref, w_ref, z_ref, k_ref, v_ref, h_ref = refs[n_x:]
    i, j = pl.program_id(0), pl.program_id(1)

    @pl.when(j == 0)
    def _():
        shift, scale = mod_ref[0:1, :], mod_ref[1:2, :]
        h_ref[...] = (_pick_rows(x_refs, i < n_ctx_tiles) * (1.0 + scale) + shift).astype(BF16)

    z_ref[...] = jnp.dot(h_ref[...], w_ref[...], preferred_element_type=F32)

    @pl.when(jnp.logical_and(j == 0, i < n_ctx_tiles))
    def _():
        k_ref[...] = z_ref[:, K_OFF:K_OFF + KV_DIM]
        v_ref[...] = z_ref[:, V_OFF:V_OFF + KV_DIM]


def _token_specs(xs, tile, n_ctx_tiles, n_grid_axes):
    def spec(row_of):
        index_map = ((lambda i: (row_of(i), 0)) if n_grid_axes == 1 else (lambda i, j: (row_of(i), 0)))
        return pl.BlockSpec((tile, D_MODEL), index_map)
    if len(xs) == 1:
        return [spec(lambda i: i)]
    return [spec(lambda i: jnp.minimum(i, n_ctx_tiles - 1)), spec(lambda i: jnp.maximum(i - n_ctx_tiles, 0))]


def _inproj_tile(xs):
    return TOKEN_TILE // len(xs)


def _inproj(xs, mods, w_in, layer, mod_idx, n_ctx):
    assert V_OFF + KV_DIM <= INPROJ_TN
    tile = _inproj_tile(xs)
    n_tok = sum(x.shape[0] for x in xs)
    n_ctx_tiles = n_ctx // tile
    kv_spec = pl.BlockSpec((tile, KV_DIM), lambda i, j: (jnp.minimum(i, n_ctx_tiles - 1), 0))
    return pl.pallas_call(
        functools.partial(_inproj_kernel, n_x=len(xs), n_ctx_tiles=n_ctx_tiles),
        grid=(n_tok // tile, IN_DIM // INPROJ_TN),
        in_specs=_token_specs(xs, tile, n_ctx_tiles, 2) + [
                  pl.BlockSpec((None, None, 6, D_MODEL), lambda i, j: (layer, mod_idx(i), 0, 0)),
                  pl.BlockSpec((D_MODEL, INPROJ_TN), lambda i, j: (0, j))],
        out_specs=[pl.BlockSpec((tile, INPROJ_TN), lambda i, j: (i, j)), kv_spec, kv_spec],
        out_shape=[jax.ShapeDtypeStruct((n_tok, IN_DIM), F32),
                   jax.ShapeDtypeStruct((n_ctx, KV_DIM), F32),
                   jax.ShapeDtypeStruct((n_ctx, KV_DIM), F32)],
        scratch_shapes=[pltpu.VMEM((tile, D_MODEL), BF16)],
        compiler_params=_params(2),
        name="inproj",
    )(*xs, mods, w_in)


def _with_ones(v):
    return jnp.concatenate([v, jnp.ones_like(v)], axis=1)


def _sink_attend(scores, values, sink2):
    c = ATT_SCALE * LOG2E
    m_raw = jnp.max(scores[0], -1, keepdims=True)
    for s in scores[1:]:
        m_raw = jnp.maximum(m_raw, jnp.max(s, -1, keepdims=True))
    m2 = jnp.maximum(m_raw * c, sink2)
    acc = None
    for s, v1 in zip(scores, values):
        e = jnp.exp2(s * c - m2).astype(BF16)
        part = jnp.dot(e, v1, preferred_element_type=F32)
        acc = part if acc is None else acc + part
    den = acc[:, HEAD_DIM:HEAD_DIM + 1] + jnp.exp2(sink2 - m2)
    return acc[:, :HEAD_DIM] * (1.0 / den)


def _rms_norm_store(o_scr, sq, g_ref, o_ref):
    ss = jnp.sum(sq, -1, keepdims=True)
    inv = lax.rsqrt(ss * (1.0 / ATT_DIM) + RMS_EPS)
    o_ref[...] = (o_scr[...] * inv * g_ref[...]).astype(o_ref.dtype)


def _head_cols(g, r):
    h = g * Q_PER_KV + r
    return slice(h * HEAD_DIM, (h + 1) * HEAD_DIM)


def _group_sinks(sink_ref, g, rows):
    return jnp.concatenate([jnp.full((rows, 1), sink_ref[g * Q_PER_KV + r] * LOG2E, F32)
                            for r in range(Q_PER_KV)], axis=0)


def _scatter_heads(og, g, rows, sq, o_scr):
    for r in range(Q_PER_KV):
        oh = og[r * rows:(r + 1) * rows]
        sq = sq + oh * oh
        o_scr[:, _head_cols(g, r)] = oh
    return sq


def _ctx_attn_kernel(sink_ref, q_ref, k_ref, v_ref, g_ref, o_ref, o_scr, *, seq):
    c = ATT_SCALE * LOG2E
    kv_cols = [slice(g * HEAD_DIM, (g + 1) * HEAD_DIM) for g in range(KV_HEADS)]
    heads = [(g, r) for g in range(KV_HEADS) for r in range(Q_PER_KV)]
    for r0 in range(0, q_ref.shape[0], seq):
        rows = slice(r0, r0 + seq)
        sq = jnp.zeros((seq, HEAD_DIM), F32)
        ks = [k_ref[rows, cols].astype(BF16) for cols in kv_cols]
        vs = [v_ref[rows, cols].astype(BF16) for cols in kv_cols]
        scores = [lax.dot_general(q_ref[rows, _head_cols(g, r)].astype(BF16), ks[g], NT_DIMS,
                                  preferred_element_type=F32) for g, r in heads]
        probs = []
        for (g, r), s in zip(heads, scores):
            sink2 = sink_ref[g * Q_PER_KV + r] * LOG2E
            m2 = jnp.maximum(jnp.max(s, -1, keepdims=True) * c, sink2)
            e = jnp.exp2(s * c - m2)
            den = jnp.sum(e, -1, keepdims=True) + jnp.exp2(sink2 - m2)
            probs.append((e * (1.0 / den)).astype(BF16))
        outs = [jnp.dot(p, vs[g], preferred_element_type=F32) for (g, r), p in zip(heads, probs)]
        for (g, r), oh in zip(heads, outs):
            sq = sq + oh * oh
            o_scr[rows, _head_cols(g, r)] = oh
        _rms_norm_store(o_scr.at[rows], sq, g_ref, o_ref.at[rows])


def _ctx_attention(z, sink, norm_g, n_batch, seq):
    kb, vb = K_OFF // KV_DIM, V_OFF // KV_DIM
    rows = CTX_ATT_SEQS * seq
    return pl.pallas_call(
        functools.partial(_ctx_attn_kernel, seq=seq),
        grid=(n_batch // CTX_ATT_SEQS,),
        in_specs=[pl.BlockSpec(memory_space=pltpu.SMEM),
                  pl.BlockSpec((rows, ATT_DIM), lambda b: (b, 0)),
                  pl.BlockSpec((rows, KV_DIM), lambda b: (b, kb)),
                  pl.BlockSpec((rows, KV_DIM), lambda b: (b, vb)),
                  pl.BlockSpec((1, ATT_DIM), lambda b: (0, 0))],
        out_specs=pl.BlockSpec((rows, ATT_DIM), lambda b: (b, 0)),
        out_shape=jax.ShapeDtypeStruct((n_batch * seq, ATT_DIM), BF16),
        scratch_shapes=[pltpu.VMEM((rows, ATT_DIM), F32)],
        compiler_params=_params(1),
        name="ctx_attention",
    )(sink, z, z, z, norm_g.reshape(1, ATT_DIM))


def _rope(x, cos, sin_lo, sin_hi):
    return (x * cos + pltpu.roll(x, HEAD_DIM - ROPE_FREQS, 1) * sin_lo
            + pltpu.roll(x, ROPE_FREQS, 1) * sin_hi)


def _lat_attn_kernel(sink_ref, q_ref, k_ref, v_ref, ck_ref, cv_ref, cq_ref, slq_ref, shq_ref,
                     ca_ref, sla_ref, sha_ref, g_ref, o_ref, kr_scr, v_scr, ckb_scr, cvb_scr, o_scr):
    n = pl.program_id(1)
    seq = k_ref.shape[0]
    qb = q_ref.shape[0]
    win = qb + 2 * WINDOW

    @pl.when(n == 0)
    def _():
        for g in range(KV_HEADS):
            cols = slice(g * HEAD_DIM, (g + 1) * HEAD_DIM)
            kr_scr[:, cols] = _rope(k_ref[:, cols], ca_ref[...], sla_ref[...], sha_ref[...]).astype(BF16)
            v_scr[g] = _with_ones(v_ref[:, cols].astype(BF16))
            cvb_scr[g] = _with_ones(cv_ref[:, cols].astype(BF16))
        ckb_scr[...] = ck_ref[...].astype(BF16)

    k0 = pl.multiple_of(jnp.clip(n * qb - WINDOW, 0, seq - win), BLOCK)
    rows = Q_PER_KV * qb
    qpos = n * qb + lax.broadcasted_iota(jnp.int32, (rows, win), 0) % qb
    kpos = k0 + lax.broadcasted_iota(jnp.int32, (rows, win), 1)
    valid = jnp.abs(kpos - qpos) <= WINDOW
    sq = jnp.zeros((qb, HEAD_DIM), F32)
    for g in range(KV_HEADS):
        kv_cols = slice(g * HEAD_DIM, (g + 1) * HEAD_DIM)
        k_loc = kr_scr[pl.ds(k0, win), kv_cols]
        v_loc = v_scr[g, pl.ds(k0, win), :]
        k_ctx = ckb_scr[:, kv_cols]
        v_ctx = cvb_scr[g]
        qg = jnp.concatenate(
            [_rope(q_ref[:, _head_cols(g, r)], cq_ref[...], slq_ref[...], shq_ref[...]).astype(BF16)
             for r in range(Q_PER_KV)], axis=0)
        s_loc = lax.dot_general(qg, k_loc, NT_DIMS, preferred_element_type=F32)
        s_loc = jnp.where(valid, s_loc, NEG_INF / ATT_SCALE)
        s_ctx = lax.dot_general(qg, k_ctx, NT_DIMS, preferred_element_type=F32)
        og = _sink_attend([s_loc, s_ctx], [v_loc, v_ctx], _group_sinks(sink_ref, g, qb))
        sq = _scatter_heads(og, g, qb, sq, o_scr)
    _rms_norm_store(o_scr, sq, g_ref, o_ref)


def _lat_attention(z, cache_k, cache_v, layer, sink, norm_g, rope_tabs, row0, n_batch, seq):
    kb, vb = K_OFF // KV_DIM, V_OFF // KV_DIM
    assert WINDOW == BLOCK and LAT_Q_ROWS % BLOCK == 0
    nq = seq // LAT_Q_ROWS
    qrow0, srow0 = row0 // LAT_Q_ROWS, row0 // seq
    past = cache_k.shape[2]
    cos, sin_lo, sin_hi = rope_tabs
    qtab = pl.BlockSpec((LAT_Q_ROWS, HEAD_DIM), lambda b, n: (n, 0))
    atab = pl.BlockSpec((seq, HEAD_DIM), lambda b, n: (0, 0))
    cache_spec = pl.BlockSpec((None, None, past, KV_DIM), lambda b, n: (b, layer, 0, 0))
    return pl.pallas_call(
        _lat_attn_kernel,
        grid=(n_batch, nq),
        in_specs=[pl.BlockSpec(memory_space=pltpu.SMEM),
                  pl.BlockSpec((LAT_Q_ROWS, ATT_DIM), lambda b, n: (qrow0 + b * nq + n, 0)),
                  pl.BlockSpec((seq, KV_DIM), lambda b, n: (srow0 + b, kb)),
                  pl.BlockSpec((seq, KV_DIM), lambda b, n: (srow0 + b, vb)),
                  cache_spec, cache_spec, qtab, qtab, qtab, atab, atab, atab,
                  pl.BlockSpec((1, ATT_DIM), lambda b, n: (0, 0))],
        out_specs=pl.BlockSpec((LAT_Q_ROWS, ATT_DIM), lambda b, n: (b * nq + n, 0)),
        out_shape=jax.ShapeDtypeStruct((n_batch * seq, ATT_DIM), BF16),
        scratch_shapes=[pltpu.VMEM((seq, KV_DIM), BF16), pltpu.VMEM((KV_HEADS, seq, 2 * HEAD_DIM), BF16),
                        pltpu.VMEM((past, KV_DIM), BF16), pltpu.VMEM((KV_HEADS, past, 2 * HEAD_DIM), BF16),
                        pltpu.VMEM((LAT_Q_ROWS, ATT_DIM), F32)],
        compiler_params=_params(2),
        name="lat_attention",
    )(sink, z, z, z, cache_k, cache_v, cos, sin_lo, sin_hi, cos, sin_lo, sin_hi,
      norm_g.reshape(1, ATT_DIM))


def _rope_tables(seq):
    rows = seq // GRID_W
    row = jnp.repeat(jnp.arange(rows, dtype=F32), GRID_W)
    col = jnp.tile(jnp.arange(GRID_W, dtype=F32), rows)
    inv = ROPE_THETA ** (-jnp.arange(ROPE_FREQS, dtype=F32) / ROPE_FREQS)
    ang_r, ang_c = row[:, None] * inv, col[:, None] * inv
    cr, sr, cc, sc = jnp.cos(ang_r), jnp.sin(ang_r), jnp.cos(ang_c), jnp.sin(ang_c)
    zero = jnp.zeros_like(sr)
    cos = jnp.concatenate([cr, cr, cc, cc], -1)
    sin_lo = jnp.concatenate([-sr, zero, -sc, zero], -1)
    sin_hi = jnp.concatenate([zero, sr, zero, sc], -1)
    return cos, sin_lo, sin_hi


def _chunk_cumsum(x, rows, reverse):
    s = 1
    while s < HG_CHUNK:
        if reverse:
            x = x + jnp.where(rows < HG_CHUNK - s, pltpu.roll(x, HG_CHUNK - s, 0), 0.0)
        else:
            x = x + jnp.where(rows >= s, pltpu.roll(x, s, 0), 0.0)
        s *= 2
    return x


def _hgrn_kernel(*refs, seq, has_init, want_state):
    hq_ref, hff_ref, hfb_ref, hi_ref, hgt_ref, lbf_ref, lbb_ref, ng_ref = refs[:8]
    pos = 8
    if has_init:
        sf0_ref, sb0_ref = refs[pos:pos + 2]
        pos += 2
    if want_state:
        pos += 2
    o_ref = refs[pos]
    pos += 1
    if want_state:
        sf_ref, sb_ref = refs[pos:pos + 2]
        pos += 2
    q_scr, of_scr, ob_scr, st_scr = refs[pos:pos + 4]

    C = HG_CHUNK
    n_chunks = seq // C
    q_in = hq_ref[...]
    q_scr[...] = q_in * _sigmoid(q_in)
    for h in range(HG_GROUP):
        if has_init:
            st_scr[h] = sf0_ref[h].T
            st_scr[HG_GROUP + h] = sb0_ref[h].T
        else:
            st_scr[h] = jnp.zeros((HG_DV, HG_DK), F32)
            st_scr[HG_GROUP + h] = jnp.zeros((HG_DV, HG_DK), F32)

    rows = lax.broadcasted_iota(jnp.int32, (C, HG_DK), 0)
    ti = lax.broadcasted_iota(jnp.int32, (2 * C, 2 * C), 0)
    si = lax.broadcasted_iota(jnp.int32, (2 * C, 2 * C), 1)
    same_chunk = (ti // C) == (si // C)
    p1_sees_p0 = jnp.logical_and(ti >= C, si < C)
    pair_mask_f = jnp.logical_or(jnp.logical_and(same_chunk, si <= ti), p1_sees_p0)
    pair_mask_b = jnp.logical_or(jnp.logical_and(same_chunk, si >= ti), p1_sees_p0)
    dirs = ((hff_ref, lbf_ref, of_scr, False, pair_mask_f, C // 2 - 1, C - 1),
            (hfb_ref, lbb_ref, ob_scr, True, pair_mask_b, C // 2, 0))
    n_pairs = n_chunks // 2
    zeros = jnp.zeros((C, HG_DK), BF16)

    def chunk_prep(z_ref, lb_ref, reverse, mid, last, rs, cols):
        q = q_scr[rs, cols]
        v = hi_ref[rs, cols].astype(BF16)
        lb = lb_ref[:, cols]
        sig = _sigmoid(z_ref[rs, cols])
        log_f = jnp.log(lb + (1.0 - lb) * sig)
        k = (1.0 - lb) * (1.0 - sig)
        b = _chunk_cumsum(log_f, rows, reverse)
        b_mid = b[mid:mid + 1, :]
        b_last = b[last:last + 1, :]
        e_mid = jnp.exp(b - b_mid)
        qe = q * e_mid
        ke = k * (1.0 / e_mid)
        return dict(v=v, qe=qe.astype(BF16), ke=ke.astype(BF16), qb=qe * jnp.exp(b_mid),
                    kl=ke * jnp.exp(b_last - b_mid), decay=jnp.exp(b_last))

    def pair_step(c, carry):
        f0 = pl.multiple_of(c * 2 * C, C)
        b1 = pl.multiple_of((n_pairs - 1 - c) * 2 * C, C)
        starts = ((f0, pl.multiple_of(f0 + C, C)), (pl.multiple_of(b1 + C, C), b1))
        streams = [(h, d) for h in range(HG_GROUP) for d in range(2)]
        prep = []
        for h, d in streams:
            z_ref, lb_ref, _, reverse, _, mid, last = dirs[d]
            cols = slice(h * HG_DK, (h + 1) * HG_DK)
            p0, p1 = (chunk_prep(z_ref, lb_ref, reverse, mid, last, pl.ds(r, C), cols) for r in starts[d])
            cat = lambda a, b: jnp.concatenate([a, b], axis=0)
            prep.append(dict(
                v=cat(p0["v"], p1["v"]),
                a_lhs=cat(jnp.concatenate([p0["qe"], zeros, zeros], axis=1),
                          jnp.concatenate([zeros, p1["qb"].astype(BF16), p1["qe"]], axis=1)),
                a_rhs=cat(jnp.concatenate([p0["ke"], p0["kl"].astype(BF16), zeros], axis=1),
                          jnp.concatenate([zeros, zeros, p1["ke"]], axis=1)),
                qb=cat(p0["qb"], p1["qb"] * p0["decay"]).astype(BF16),
                kl=cat(p0["kl"] * p1["decay"], p1["kl"]).astype(BF16),
                decay=p0["decay"] * p1["decay"]))
        a_raw = [lax.dot_general(p["a_lhs"], p["a_rhs"], NT_DIMS, preferred_element_type=F32) for p in prep]
        states = [st_scr[d * HG_GROUP + h] for h, d in streams]
        o_inter = [lax.dot_general(p["qb"], st.astype(BF16), NT_DIMS, preferred_element_type=F32)
                   for p, st in zip(prep, states)]
        u_t = [lax.dot_general(p["v"], p["kl"], TN_DIMS, preferred_element_type=F32) for p in prep]
        o_intra = [jnp.dot(jnp.where(dirs[d][4], a, 0.0).astype(BF16), p["v"], preferred_element_type=F32)
                   for (h, d), a, p in zip(streams, a_raw, prep)]
        for i, (h, d) in enumerate(streams):
            cols = slice(h * HG_DK, (h + 1) * HG_DK)
            o = o_intra[i] + o_inter[i]
            dirs[d][2][pl.ds(starts[d][0], C), cols] = o[:C]
            dirs[d][2][pl.ds(starts[d][1], C), cols] = o[C:]
            st_scr[d * HG_GROUP + h] = states[i] * prep[i]["decay"] + u_t[i]
        return carry

    lax.fori_loop(0, n_pairs, pair_step, 0, unroll=2)

    for h in range(HG_GROUP):
        cols = slice(h * HG_DK, (h + 1) * HG_DK)
        o = of_scr[:, cols] + ob_scr[:, cols]
        o = o * lax.rsqrt(jnp.mean(o * o, -1, keepdims=True) + RMS_EPS) * ng_ref[...]
        gt = hgt_ref[:, cols]
        o_ref[:, cols] = (o * (gt * _sigmoid(gt))).astype(o_ref.dtype)
        if want_state:
            sf_ref[h] = st_scr[h].T
            sb_ref[h] = st_scr[HG_GROUP + h].T


def _hgrn(z, lb_f, lb_b, norm_g, row0, n_batch, seq, init_states=None, layer=None, state_bufs=None):
    n_groups = HG_HEADS // HG_GROUP
    r0 = row0 // seq
    has_init = init_states is not None
    want_state = state_bufs is not None

    def zspec(off):
        return pl.BlockSpec((seq, HG_GW), lambda b, g, o=off // HG_GW: (r0 + b, o + g))

    lbspec = pl.BlockSpec((1, HG_GW), lambda b, g: (0, g))
    in_specs = [zspec(HQ_OFF), zspec(HFF_OFF), zspec(HFB_OFF), zspec(HI_OFF), zspec(HGT_OFF),
                lbspec, lbspec, pl.BlockSpec((1, HG_DV), lambda b, g: (0, 0))]
    args = [z, z, z, z, z, lb_f.reshape(1, HG_DIM), lb_b.reshape(1, HG_DIM), norm_g.reshape(1, HG_DV)]
    if has_init:
        st_spec = pl.BlockSpec((None, None, HG_GROUP, HG_DK, HG_DV), lambda b, g: (b, layer, g, 0, 0))
        in_specs += [st_spec, st_spec]
        args += list(init_states)
    out_specs = [pl.BlockSpec((seq, HG_GW), lambda b, g: (b, g))]
    out_shape = [jax.ShapeDtypeStruct((n_batch * seq, HG_VDIM), BF16)]
    aliases = {}
    if want_state:
        so = pl.BlockSpec((None, None, HG_GROUP, HG_DK, HG_DV), lambda b, g: (layer, b, g, 0, 0))
        for buf in state_bufs:
            aliases[len(args)] = len(out_specs)
            in_specs.append(pl.BlockSpec(memory_space=pl.ANY))
            args.append(buf)
            out_specs.append(so)
            out_shape.append(jax.ShapeDtypeStruct(buf.shape, buf.dtype))
    return pl.pallas_call(
        functools.partial(_hgrn_kernel, seq=seq, has_init=has_init, want_state=want_state),
        grid=(n_batch, n_groups),
        in_specs=in_specs,
        out_specs=out_specs,
        out_shape=out_shape,
        scratch_shapes=[pltpu.VMEM((seq, HG_GW), F32), pltpu.VMEM((seq, HG_GW), F32),
                        pltpu.VMEM((seq, HG_GW), F32),
                        pltpu.VMEM((2 * HG_GROUP, HG_DV, HG_DK), F32)],
        input_output_aliases=aliases,
        compiler_params=_params(2),
        name=f"hgrn_t{seq}",
    )(*args)


def _oproj_kernel(*refs, n_x, n_ctx_tiles):
    attc_ref, attl_ref, hgc_ref, hgl_ref, w_ref = refs[:5]
    x_refs = refs[5:5 + n_x]
    mod_ref, g_ref, b_ref, o_ref = refs[5 + n_x:]
    is_ctx = pl.program_id(0) < n_ctx_tiles
    for r0 in range(0, OPROJ_TM, OPROJ_SUB):
        rows = slice(r0, r0 + OPROJ_SUB)
        att = jnp.where(is_ctx, attc_ref[rows], attl_ref[rows])
        hg = jnp.where(is_ctx, hgc_ref[rows], hgl_ref[rows])
        mix = (jnp.dot(att, w_ref[0:ATT_DIM, :], preferred_element_type=F32)
               + jnp.dot(hg, w_ref[ATT_DIM:MIX_DIM, :], preferred_element_type=F32))
        y = DEEPNORM_ALPHA * _pick_rows(x_refs, is_ctx, rows) + mod_ref[2:3, :] * mix
        o_ref[rows] = _layer_norm(y, g_ref[0:1, :], b_ref[0:1, :])


def _oproj(att_c, att_l, hg_c, hg_l, w_o, xs, mods, ln_g, ln_b, layer, mod_idx):
    n_tok = sum(x.shape[0] for x in xs)
    n_ctx_tiles = att_c.shape[0] // OPROJ_TM
    row = lambda i: (i, 0)
    ctx_row = lambda i: (jnp.minimum(i, n_ctx_tiles - 1), 0)
    lat_row = lambda i: (jnp.maximum(i - n_ctx_tiles, 0), 0)
    return pl.pallas_call(
        functools.partial(_oproj_kernel, n_x=len(xs), n_ctx_tiles=n_ctx_tiles),
        grid=(n_tok // OPROJ_TM,),
        in_specs=[pl.BlockSpec((OPROJ_TM, ATT_DIM), ctx_row),
                  pl.BlockSpec((OPROJ_TM, ATT_DIM), lat_row),
                  pl.BlockSpec((OPROJ_TM, HG_VDIM), ctx_row),
                  pl.BlockSpec((OPROJ_TM, HG_VDIM), lat_row),
                  pl.BlockSpec((MIX_DIM, D_MODEL), lambda i: (0, 0))]
        + _token_specs(xs, OPROJ_TM, n_ctx_tiles, 1) + [
                  pl.BlockSpec((None, None, 6, D_MODEL), lambda i: (layer, mod_idx(i), 0, 0)),
                  pl.BlockSpec((None, 2, D_MODEL), lambda i: (layer, 0, 0)),
                  pl.BlockSpec((None, 2, D_MODEL), lambda i: (layer, 0, 0))],
        out_specs=pl.BlockSpec((OPROJ_TM, D_MODEL), row),
        out_shape=jax.ShapeDtypeStruct((n_tok, D_MODEL), F32),
        compiler_params=_params(1),
        name="oproj_ln",
    )(att_c, att_l, hg_c, hg_l, w_o, *xs, mods, ln_g, ln_b)


def _mlp_kernel(*refs, n_cast, n_out, n_ctx_tiles):
    x_ref, mod_ref, wu_ref, wd_ref, g_ref, b_ref = refs[:6]
    cast_in = refs[6:6 + n_cast]
    outs = refs[6 + n_cast:6 + n_cast + n_out]
    cast_out = refs[6 + n_cast + n_out:6 + 2 * n_cast + n_out]
    h_ref, acc_ref = refs[6 + 2 * n_cast + n_out:]
    i, f = pl.program_id(0), pl.program_id(1)
    n_f = pl.num_programs(1)

    @pl.when(f == 0)
    def _():
        shift, scale = mod_ref[3:4, :], mod_ref[4:5, :]
        h_ref[...] = (x_ref[...] * (1.0 + scale) + shift).astype(BF16)
        acc_ref[...] = jnp.zeros_like(acc_ref)

    if n_cast:
        @pl.when(i * n_f + f < CAST_STEPS)
        def _():
            for src, dst in zip(cast_in, cast_out):
                dst[...] = src[...].astype(BF16)

    u = jnp.dot(h_ref[...], wu_ref[...], preferred_element_type=F32)
    u = jnp.square(jnp.maximum(u, 0.0)).astype(BF16)
    for n in range(0, D_MODEL, MLP_TN):
        acc_ref[:, n:n + MLP_TN] += jnp.dot(u, wd_ref[:, n:n + MLP_TN], preferred_element_type=F32)

    def finish(o_ref):
        y = DEEPNORM_ALPHA * x_ref[...] + mod_ref[5:6, :] * acc_ref[...]
        o_ref[...] = _layer_norm(y, g_ref[1:2, :], b_ref[1:2, :])

    last = f == n_f - 1
    if n_out == 1:
        pl.when(last)(lambda: finish(outs[0]))
    else:
        pl.when(jnp.logical_and(last, i < n_ctx_tiles))(lambda: finish(outs[0]))
        pl.when(jnp.logical_and(last, i >= n_ctx_tiles))(lambda: finish(outs[1]))


def _cast_specs(weights, layer):
    n_f = D_FF // MLP_TF
    in_specs, out_specs, out_shape = [], [], []
    for w in weights:
        _, rows, cols = w.shape
        slab = rows // CAST_STEPS
        in_specs.append(pl.BlockSpec(
            (None, slab, cols), lambda i, f: (layer, jnp.minimum(i * n_f + f, CAST_STEPS - 1), 0)))
        out_specs.append(pl.BlockSpec(
            (slab, cols), lambda i, f: (jnp.minimum(i * n_f + f, CAST_STEPS - 1), 0)))
        out_shape.append(jax.ShapeDtypeStruct((rows, cols), BF16))
    return in_specs, out_specs, out_shape


def _mlp(x, mods, w_up, w_down, ln_g, ln_b, layer, mod_idx, next_weights=(), split_rows=None):
    n_tok = x.shape[0]
    n_f = D_FF // MLP_TF
    assert (n_tok // MLP_TM) * n_f >= CAST_STEPS
    cast_in_specs, cast_out_specs, cast_shapes = _cast_specs(next_weights, layer + 1)
    if split_rows is None:
        n_ctx_tiles = n_tok // MLP_TM
        y_specs = [pl.BlockSpec((MLP_TM, D_MODEL), lambda i, f: (i, 0))]
        y_shapes = [jax.ShapeDtypeStruct((n_tok, D_MODEL), F32)]
    else:
        n_ctx_tiles = split_rows // MLP_TM
        y_specs = [pl.BlockSpec((MLP_TM, D_MODEL), lambda i, f: (jnp.minimum(i, n_ctx_tiles - 1), 0)),
                   pl.BlockSpec((MLP_TM, D_MODEL), lambda i, f: (jnp.maximum(i - n_ctx_tiles, 0), 0))]
        y_shapes = [jax.ShapeDtypeStruct((split_rows, D_MODEL), F32),
                    jax.ShapeDtypeStruct((n_tok - split_rows, D_MODEL), F32)]
    return pl.pallas_call(
        functools.partial(_mlp_kernel, n_cast=len(next_weights), n_out=len(y_specs),
                          n_ctx_tiles=n_ctx_tiles),
        grid=(n_tok // MLP_TM, n_f),
        in_specs=[pl.BlockSpec((MLP_TM, D_MODEL), lambda i, f: (i, 0)),
                  pl.BlockSpec((None, None, 6, D_MODEL), lambda i, f: (layer, mod_idx(i), 0, 0)),
                  pl.BlockSpec((D_MODEL, MLP_TF), lambda i, f: (0, f)),
                  pl.BlockSpec((MLP_TF, D_MODEL), lambda i, f: (f, 0)),
                  pl.BlockSpec((None, 2, D_MODEL), lambda i, f: (layer, 0, 0)),
                  pl.BlockSpec((None, 2, D_MODEL), lambda i, f: (layer, 0, 0))] + cast_in_specs,
        out_specs=y_specs + cast_out_specs,
        out_shape=y_shapes + cast_shapes,
        scratch_shapes=[pltpu.VMEM((MLP_TM, D_MODEL), BF16), pltpu.VMEM((MLP_TM, D_MODEL), F32)],
        compiler_params=_params(2),
        name="mlp_ln",
    )(x, mods, w_up, w_down, ln_g, ln_b, *next_weights)


def _cache_pack_kernel(*refs, depth):
    k_refs, v_refs = refs[:depth], refs[depth:2 * depth]
    ok_ref, ov_ref = refs[2 * depth:]
    for l in range(depth):
        for h in range(KV_HEADS):
            cols = slice(h * HEAD_DIM, (h + 1) * HEAD_DIM)
            ok_ref[0, l, :, h, :] = k_refs[l][:, cols]
            ov_ref[0, l, :, h, :] = v_refs[l][:, cols]


def _cache_pack(ks, vs, n_batch, seq):
    depth = len(ks)
    in_spec = pl.BlockSpec((seq, KV_DIM), lambda b: (b, 0))
    out_spec = pl.BlockSpec((1, depth, seq, KV_HEADS, HEAD_DIM), lambda b: (b, 0, 0, 0, 0))
    out_shape = jax.ShapeDtypeStruct((n_batch, depth, seq, KV_HEADS, HEAD_DIM), F32)
    return pl.pallas_call(
        functools.partial(_cache_pack_kernel, depth=depth),
        grid=(n_batch,),
        in_specs=[in_spec] * (2 * depth),
        out_specs=[out_spec, out_spec],
        out_shape=[out_shape, out_shape],
        compiler_params=_params(1),
        name="cache_pack",
    )(*ks, *vs)


def _lower_bounds(lb_logits):
    p = jax.nn.softmax(lb_logits.astype(F32), axis=0)
    cs = jnp.cumsum(p, axis=0)
    return cs - cs[0:1]


def kernel(x_prompt, x_sample, cache_k, cache_v, state_hgrn_fwd, state_hgrn_bwd, c, c_ctx, w_mod, b_mod,
           w_in, attn_sink, attn_norm_g, hg_lb_logits, hg_norm_g, w_o, ln_g, ln_b, w_up, w_down):
    bp, seq_p, d = x_prompt.shape
    bs, seq_s, _ = x_sample.shape
    depth = w_in.shape[0]
    past = cache_k.shape[2]
    n_ctx, n_lat = bp * seq_p, bs * seq_s

    xs = (x_prompt.reshape(n_ctx, d), x_sample.reshape(n_lat, d))
    c_rows = jnp.concatenate([c_ctx[None, :], c, jnp.zeros((MOD_ROWS - 1 - bs, d), F32)], 0)
    mods = _modulation(c_rows, w_mod, b_mod)[:, :1 + bs].reshape(depth, 1 + bs, 6, d)

    weights = (w_in, w_o, w_up, w_down)
    w_in_b, w_o_b, w_up_b, w_down_b = (w[0].astype(BF16) for w in weights)
    lb_f, lb_b = _lower_bounds(hg_lb_logits[0]), _lower_bounds(hg_lb_logits[1])
    rope_tabs = _rope_tables(seq_s)
    ck = cache_k.reshape(bs, depth, past, KV_DIM)
    cv = cache_v.reshape(bs, depth, past, KV_DIM)

    new_k, new_v = [], []
    s_f = jnp.zeros((depth, bp, HG_HEADS, HG_DK, HG_DV), F32)
    s_b = jnp.zeros((depth, bp, HG_HEADS, HG_DK, HG_DV), F32)
    for l in range(depth):
        z, k_c, v_c = _inproj(xs, mods, w_in_b, l, _mod_index(_inproj_tile(xs), n_ctx, seq_s), n_ctx)
        new_k.append(k_c)
        new_v.append(v_c)

        att_c = _ctx_attention(z, attn_sink[l], attn_norm_g[l], bp, seq_p)
        att_l = _lat_attention(z, ck, cv, l, attn_sink[l], attn_norm_g[l], rope_tabs, n_ctx, bs, seq_s)

        hg_c, s_f, s_b = _hgrn(z, lb_f[l], lb_b[l], hg_norm_g[l], 0, bp, seq_p, layer=l,
                               state_bufs=(s_f, s_b))
        (hg_l,) = _hgrn(z, lb_f[l], lb_b[l], hg_norm_g[l], n_ctx, bs, seq_s,
                        init_states=(state_hgrn_fwd, state_hgrn_bwd), layer=l)

        x = _oproj(att_c, att_l, hg_c, hg_l, w_o_b, xs, mods, ln_g, ln_b, l,
                   _mod_index(OPROJ_TM, n_ctx, seq_s))
        mlp_mods = _mod_index(MLP_TM, n_ctx, seq_s)
        if l + 1 < depth:
            x, w_in_b, w_o_b, w_up_b, w_down_b = _mlp(x, mods, w_up_b, w_down_b, ln_g, ln_b, l, mlp_mods,
                                                      next_weights=weights)
            xs = (x,)
        else:
            y_p, y_s = _mlp(x, mods, w_up_b, w_down_b, ln_g, ln_b, l, mlp_mods, split_rows=n_ctx)

    new_cache_k, new_cache_v = _cache_pack(new_k, new_v, bp, seq_p)
    return (y_p.reshape(bp, seq_p, d), y_s.reshape(bs, seq_s, d), new_cache_k, new_cache_v,
            jnp.swapaxes(s_f, 0, 1), jnp.swapaxes(s_b, 0, 1))
```

```python
import functools

import jax
import jax.numpy as jnp
from jax import lax
from jax.experimental import pallas as pl
from jax.experimental.pallas import tpu as pltpu

F32 = jnp.float32
BF16 = jnp.bfloat16

D_MODEL = 2048
DEPTH = 4
GRID_W = 64
HEAD_DIM = 128
ATT_HEADS = 8
KV_HEADS = 2
Q_PER_KV = ATT_HEADS // KV_HEADS
ATT_DIM = ATT_HEADS * HEAD_DIM
KV_DIM = KV_HEADS * HEAD_DIM
WINDOW = 128
BLOCK = 128
ATT_SCALE = HEAD_DIM ** -0.5
ROPE_THETA = 10000.0
ROPE_FREQS = HEAD_DIM // 4
HG_HEADS = 8
HG_DK = 128
HG_DV = 128
HG_DIM = HG_HEADS * HG_DK
HG_VDIM = HG_HEADS * HG_DV
HG_CHUNK = 32
MIX_DIM = ATT_DIM + HG_VDIM
IN_DIM = ATT_DIM + 2 * KV_DIM + 3 * HG_DIM + 2 * HG_VDIM
D_FF = 4 * D_MODEL
LN_EPS = 1e-5
RMS_EPS = 1e-6
DEEPNORM_ALPHA = (2 * DEPTH) ** 0.25
NEG_INF = -1e30
LOG2E = 1.4426950408889634

Q_OFF = 0
K_OFF = ATT_DIM
V_OFF = K_OFF + KV_DIM
HQ_OFF = V_OFF + KV_DIM
HFF_OFF = HQ_OFF + HG_DIM
HFB_OFF = HFF_OFF + HG_DIM
HI_OFF = HFB_OFF + HG_DIM
HGT_OFF = HI_OFF + HG_VDIM

V7X_VMEM_LIMIT_BYTES = 56 * 1024 * 1024

MOD_ROWS = 8
MOD_TN = 768
TOKEN_TILE = 1024
INPROJ_TN = 1664
OPROJ_TM = 512
OPROJ_SUB = 256
MLP_TM = 512
MLP_TF = 1024
MLP_TN = 512
CAST_STEPS = 128
HG_GROUP = 4
CTX_ATT_SEQS = 2
HG_GW = HG_GROUP * HG_DK

NT_DIMS = (((1,), (1,)), ((), ()))
TN_DIMS = (((0,), (0,)), ((), ()))


def _params(n_axes):
    return pltpu.CompilerParams(dimension_semantics=("arbitrary",) * n_axes,
                                vmem_limit_bytes=V7X_VMEM_LIMIT_BYTES)


def _sigmoid(x):
    return 1.0 / (1.0 + jnp.exp(-x))


def _layer_norm(y, g, b):
    mu = jnp.mean(y, -1, keepdims=True)
    yc = y - mu
    var = jnp.mean(yc * yc, -1, keepdims=True)
    return yc * lax.rsqrt(var + LN_EPS) * g + b


def _mod_kernel(c_ref, w_ref, b_ref, o_ref):
    c = c_ref[...]
    a = (c * _sigmoid(c)).astype(BF16)
    o_ref[0] = jnp.dot(a, w_ref[0].astype(BF16), preferred_element_type=F32) + b_ref[0]


def _modulation(c_rows, w_mod, b_mod):
    depth, d, n = w_mod.shape
    return pl.pallas_call(
        _mod_kernel,
        grid=(depth, n // MOD_TN),
        in_specs=[pl.BlockSpec((MOD_ROWS, d), lambda l, j: (0, 0)),
                  pl.BlockSpec((1, d, MOD_TN), lambda l, j: (l, 0, j)),
                  pl.BlockSpec((1, 1, MOD_TN), lambda l, j: (l, 0, j))],
        out_specs=pl.BlockSpec((1, MOD_ROWS, MOD_TN), lambda l, j: (l, 0, j)),
        out_shape=jax.ShapeDtypeStruct((depth, MOD_ROWS, n), F32),
        compiler_params=_params(2),
        name="modulation",
    )(c_rows, w_mod, b_mod.reshape(depth, 1, n))


def _mod_index(tile_rows, n_ctx_rows, lat_rows):
    n_ctx_tiles = n_ctx_rows // tile_rows
    per_lat = lat_rows // tile_rows

    def idx(i):
        return jnp.where(i < n_ctx_tiles, 0, 1 + (i - n_ctx_tiles) // per_lat)
    return idx


def _pick_rows(x_refs, is_ctx, rows=slice(None)):
    if len(x_refs) == 1:
        return x_refs[0][rows]
    return jnp.where(is_ctx, x_refs[0][rows], x_refs[1][rows])


def _inproj_kernel(*refs, n_x, n_ctx_tiles):
    x_refs = refs[:n_x]
    mod_ref, w_ref, z_ref, k_ref, v_ref, h_ref = refs[n_x:]
    i, j = pl.program_id(0), pl.program_id(1)

    @pl.when(j == 0)
    def _():
        shift, scale = mod_ref[0:1, :], mod_ref[1:2, :]
        h_ref[...] = (_pick_rows(x_refs, i < n_ctx_tiles) * (1.0 + scale) + shift).astype(BF16)

    z_ref[...] = jnp.dot(h_ref[...], w_ref[...], preferred_element_type=F32)

    @pl.when(jnp.logical_and(j == 0, i < n_ctx_tiles))
    def _():
        k_ref[...] = z_ref[:, K_OFF:K_OFF + KV_DIM]
        v_ref[...] = z_ref[:, V_OFF:V_OFF + KV_DIM]


def _token_specs(xs, tile, n_ctx_tiles, n_grid_axes):
    def spec(row_of):
        index_map = ((lambda i: (row_of(i), 0)) if n_grid_axes == 1 else (lambda i, j: (row_of(i), 0)))
        return pl.BlockSpec((tile, D_MODEL), index_map)
    if len(xs) == 1:
        return [spec(lambda i: i)]
    return [spec(lambda i: jnp.minimum(i, n_ctx_tiles - 1)), spec(lambda i: jnp.maximum(i - n_ctx_tiles, 0))]


def _inproj_tile(xs):
    return TOKEN_TILE // len(xs)


def _inproj(xs, mods, w_in, layer, mod_idx, n_ctx):
    assert V_OFF + KV_DIM <= INPROJ_TN
    tile = _inproj_tile(xs)
    n_tok = sum(x.shape[0] for x in xs)
    n_ctx_tiles = n_ctx // tile
    kv_spec = pl.BlockSpec((tile, KV_DIM), lambda i, j: (jnp.minimum(i, n_ctx_tiles - 1), 0))
    return pl.pallas_call(
        functools.partial(_inproj_kernel, n_x=len(xs), n_ctx_tiles=n_ctx_tiles),
        grid=(n_tok // tile, IN_DIM // INPROJ_TN),
        in_specs=_token_specs(xs, tile, n_ctx_tiles, 2) + [
                  pl.BlockSpec((None, None, 6, D_MODEL), lambda i, j: (layer, mod_idx(i), 0, 0)),
                  pl.BlockSpec((D_MODEL, INPROJ_TN), lambda i, j: (0, j))],
        out_specs=[pl.BlockSpec((tile, INPROJ_TN), lambda i, j: (i, j)), kv_spec, kv_spec],
        out_shape=[jax.ShapeDtypeStruct((n_tok, IN_DIM), F32),
                   jax.ShapeDtypeStruct((n_ctx, KV_DIM), F32),
                   jax.ShapeDtypeStruct((n_ctx, KV_DIM), F32)],
        scratch_shapes=[pltpu.VMEM((tile, D_MODEL), BF16)],
        compiler_params=_params(2),
        name="inproj",
    )(*xs, mods, w_in)


def _with_ones(v):
    return jnp.concatenate([v, jnp.ones_like(v)], axis=1)


def _sink_attend(scores, values, sink2):
    c = ATT_SCALE * LOG2E
    m_raw = jnp.max(scores[0], -1, keepdims=True)
    for s in scores[1:]:
        m_raw = jnp.maximum(m_raw, jnp.max(s, -1, keepdims=True))
    m2 = jnp.maximum(m_raw * c, sink2)
    acc = None
    for s, v1 in zip(scores, values):
        e = jnp.exp2(s * c - m2).astype(BF16)
        part = jnp.dot(e, v1, preferred_element_type=F32)
        acc = part if acc is None else acc + part
    den = acc[:, HEAD_DIM:HEAD_DIM + 1] + jnp.exp2(sink2 - m2)
    return acc[:, :HEAD_DIM] * (1.0 / den)


def _rms_norm_store(o_scr, sq, g_ref, o_ref):
    ss = jnp.sum(sq, -1, keepdims=True)
    inv = lax.rsqrt(ss * (1.0 / ATT_DIM) + RMS_EPS)
    o_ref[...] = (o_scr[...] * inv * g_ref[...]).astype(o_ref.dtype)


def _head_cols(g, r):
    h = g * Q_PER_KV + r
    return slice(h * HEAD_DIM, (h + 1) * HEAD_DIM)


def _group_sinks(sink_ref, g, rows):
    return jnp.concatenate([jnp.full((rows, 1), sink_ref[g * Q_PER_KV + r] * LOG2E, F32)
                            for r in range(Q_PER_KV)], axis=0)


def _scatter_heads(og, g, rows, sq, o_scr):
    for r in range(Q_PER_KV):
        oh = og[r * rows:(r + 1) * rows]
        sq = sq + oh * oh
        o_scr[:, _head_cols(g, r)] = oh
    return sq


def _ctx_attn_kernel(sink_ref, q_ref, k_ref, v_ref, g_ref, o_ref, o_scr, *, seq):
    c = ATT_SCALE * LOG2E
    kv_cols = [slice(g * HEAD_DIM, (g + 1) * HEAD_DIM) for g in range(KV_HEADS)]
    heads = [(g, r) for g in range(KV_HEADS) for r in range(Q_PER_KV)]
    for r0 in range(0, q_ref.shape[0], seq):
        rows = slice(r0, r0 + seq)
        sq = jnp.zeros((seq, HEAD_DIM), F32)
        ks = [k_ref[rows, cols].astype(BF16) for cols in kv_cols]
        vs = [v_ref[rows, cols].astype(BF16) for cols in kv_cols]
        scores = [lax.dot_general(q_ref[rows, _head_cols(g, r)].astype(BF16), ks[g], NT_DIMS,
                                  preferred_element_type=F32) for g, r in heads]
        probs = []
        for (g, r), s in zip(heads, scores):
            sink2 = sink_ref[g * Q_PER_KV + r] * LOG2E
            m2 = jnp.maximum(jnp.max(s, -1, keepdims=True) * c, sink2)
            e = jnp.exp2(s * c - m2)
            den = jnp.sum(e, -1, keepdims=True) + jnp.exp2(sink2 - m2)
            probs.append((e * (1.0 / den)).astype(BF16))
        outs = [jnp.dot(p, vs[g], preferred_element_type=F32) for (g, r), p in zip(heads, probs)]
        for (g, r), oh in zip(heads, outs):
            sq = sq + oh * oh
            o_scr[rows, _head_cols(g, r)] = oh
        _rms_norm_store(o_scr.at[rows], sq, g_ref, o_ref.at[rows])


def _ctx_attention(z, sink, norm_g, n_batch, seq):
    kb, vb = K_OFF // KV_DIM, V_OFF // KV_DIM
    rows = CTX_ATT_SEQS * seq
    return pl.pallas_call(
        functools.partial(_ctx_attn_kernel, seq=seq),
        grid=(n_batch // CTX_ATT_SEQS,),
        in_specs=[pl.BlockSpec(memory_space=pltpu.SMEM),
                  pl.BlockSpec((rows, ATT_DIM), lambda b: (b, 0)),
                  pl.BlockSpec((rows, KV_DIM), lambda b: (b, kb)),
                  pl.BlockSpec((rows, KV_DIM), lambda b: (b, vb)),
                  pl.BlockSpec((1, ATT_DIM), lambda b: (0, 0))],
        out_specs=pl.BlockSpec((rows, ATT_DIM), lambda b: (b, 0)),
        out_shape=jax.ShapeDtypeStruct((n_batch * seq, ATT_DIM), BF16),
        scratch_shapes=[pltpu.VMEM((rows, ATT_DIM), F32)],
        compiler_params=_params(1),
        name="ctx_attention",
    )(sink, z, z, z, norm_g.reshape(1, ATT_DIM))


def _rope(x, cos, sin_lo, sin_hi):
    return (x * cos + pltpu.roll(x, HEAD_DIM - ROPE_FREQS, 1) * sin_lo
            + pltpu.roll(x, ROPE_FREQS, 1) * sin_hi)


def _lat_attn_kernel(sink_ref, q_ref, k_ref, v_ref, ck_ref, cv_ref, cq_ref, slq_ref, shq_ref,
                     ca_ref, sla_ref, sha_ref, g_ref, o_ref, kr_scr, v_scr, ckb_scr, cvb_scr, o_scr):
    n = pl.program_id(1)
    seq = k_ref.shape[0]
    win = 3 * BLOCK

    @pl.when(n == 0)
    def _():
        for g in range(KV_HEADS):
            cols = slice(g * HEAD_DIM, (g + 1) * HEAD_DIM)
            kr_scr[:, cols] = _rope(k_ref[:, cols], ca_ref[...], sla_ref[...], sha_ref[...]).astype(BF16)
            v_scr[g] = _with_ones(v_ref[:, cols].astype(BF16))
            cvb_scr[g] = _with_ones(cv_ref[:, cols].astype(BF16))
        ckb_scr[...] = ck_ref[...].astype(BF16)

    k0 = pl.multiple_of(jnp.clip((n - 1) * BLOCK, 0, seq - win), BLOCK)
    rows = Q_PER_KV * BLOCK
    qpos = n * BLOCK + lax.broadcasted_iota(jnp.int32, (rows, win), 0) % BLOCK
    kpos = k0 + lax.broadcasted_iota(jnp.int32, (rows, win), 1)
    valid = jnp.abs(kpos - qpos) <= WINDOW
    sq = jnp.zeros((BLOCK, HEAD_DIM), F32)
    for g in range(KV_HEADS):
        kv_cols = slice(g * HEAD_DIM, (g + 1) * HEAD_DIM)
        k_loc = kr_scr[pl.ds(k0, win), kv_cols]
        v_loc = v_scr[g, pl.ds(k0, win), :]
        k_ctx = ckb_scr[:, kv_cols]
        v_ctx = cvb_scr[g]
        qg = jnp.concatenate(
            [_rope(q_ref[:, _head_cols(g, r)], cq_ref[...], slq_ref[...], shq_ref[...]).astype(BF16)
             for r in range(Q_PER_KV)], axis=0)
        s_loc = lax.dot_general(qg, k_loc, NT_DIMS, preferred_element_type=F32)
        s_loc = jnp.where(valid, s_loc, NEG_INF / ATT_SCALE)
        s_ctx = lax.dot_general(qg, k_ctx, NT_DIMS, preferred_element_type=F32)
        og = _sink_attend([s_loc, s_ctx], [v_loc, v_ctx], _group_sinks(sink_ref, g, BLOCK))
        sq = _scatter_heads(og, g, BLOCK, sq, o_scr)
    _rms_norm_store(o_scr, sq, g_ref, o_ref)


def _lat_attention(z, cache_k, cache_v, layer, sink, norm_g, rope_tabs, row0, n_batch, seq):
    kb, vb = K_OFF // KV_DIM, V_OFF // KV_DIM
    nq = seq // BLOCK
    qrow0, srow0 = row0 // BLOCK, row0 // seq
    past = cache_k.shape[2]
    cos, sin_lo, sin_hi = rope_tabs
    qtab = pl.BlockSpec((BLOCK, HEAD_DIM), lambda b, n: (n, 0))
    atab = pl.BlockSpec((seq, HEAD_DIM), lambda b, n: (0, 0))
    cache_spec = pl.BlockSpec((None, None, past, KV_DIM), lambda b, n: (b, layer, 0, 0))
    return pl.pallas_call(
        _lat_attn_kernel,
        grid=(n_batch, nq),
        in_specs=[pl.BlockSpec(memory_space=pltpu.SMEM),
                  pl.BlockSpec((BLOCK, ATT_DIM), lambda b, n: (qrow0 + b * nq + n, 0)),
                  pl.BlockSpec((seq, KV_DIM), lambda b, n: (srow0 + b, kb)),
                  pl.BlockSpec((seq, KV_DIM), lambda b, n: (srow0 + b, vb)),
                  cache_spec, cache_spec, qtab, qtab, qtab, atab, atab, atab,
                  pl.BlockSpec((1, ATT_DIM), lambda b, n: (0, 0))],
        out_specs=pl.BlockSpec((BLOCK, ATT_DIM), lambda b, n: (b * nq + n, 0)),
        out_shape=jax.ShapeDtypeStruct((n_batch * seq, ATT_DIM), BF16),
        scratch_shapes=[pltpu.VMEM((seq, KV_DIM), BF16), pltpu.VMEM((KV_HEADS, seq, 2 * HEAD_DIM), BF16),
                        pltpu.VMEM((past, KV_DIM), BF16), pltpu.VMEM((KV_HEADS, past, 2 * HEAD_DIM), BF16),
                        pltpu.VMEM((BLOCK, ATT_DIM), F32)],
        compiler_params=_params(2),
        name="lat_attention",
    )(sink, z, z, z, cache_k, cache_v, cos, sin_lo, sin_hi, cos, sin_lo, sin_hi,
      norm_g.reshape(1, ATT_DIM))


def _rope_tables(seq):
    rows = seq // GRID_W
    row = jnp.repeat(jnp.arange(rows, dtype=F32), GRID_W)
    col = jnp.tile(jnp.arange(GRID_W, dtype=F32), rows)
    inv = ROPE_THETA ** (-jnp.arange(ROPE_FREQS, dtype=F32) / ROPE_FREQS)
    ang_r, ang_c = row[:, None] * inv, col[:, None] * inv
    cr, sr, cc, sc = jnp.cos(ang_r), jnp.sin(ang_r), jnp.cos(ang_c), jnp.sin(ang_c)
    zero = jnp.zeros_like(sr)
    cos = jnp.concatenate([cr, cr, cc, cc], -1)
    sin_lo = jnp.concatenate([-sr, zero, -sc, zero], -1)
    sin_hi = jnp.concatenate([zero, sr, zero, sc], -1)
    return cos, sin_lo, sin_hi


def _chunk_cumsum(x, rows, reverse):
    s = 1
    while s < HG_CHUNK:
        if reverse:
            x = x + jnp.where(rows < HG_CHUNK - s, pltpu.roll(x, HG_CHUNK - s, 0), 0.0)
        else:
            x = x + jnp.where(rows >= s, pltpu.roll(x, s, 0), 0.0)
        s *= 2
    return x


def _hgrn_kernel(*refs, seq, has_init, want_state):
    hq_ref, hff_ref, hfb_ref, hi_ref, hgt_ref, lbf_ref, lbb_ref, ng_ref = refs[:8]
    pos = 8
    if has_init:
        sf0_ref, sb0_ref = refs[pos:pos + 2]
        pos += 2
    if want_state:
        pos += 2
    o_ref = refs[pos]
    pos += 1
    if want_state:
        sf_ref, sb_ref = refs[pos:pos + 2]
        pos += 2
    q_scr, of_scr, ob_scr, st_scr = refs[pos:pos + 4]

    C = HG_CHUNK
    n_chunks = seq // C
    q_in = hq_ref[...]
    q_scr[...] = q_in * _sigmoid(q_in)
    for h in range(HG_GROUP):
        if has_init:
            st_scr[h] = sf0_ref[h].T
            st_scr[HG_GROUP + h] = sb0_ref[h].T
        else:
            st_scr[h] = jnp.zeros((HG_DV, HG_DK), F32)
            st_scr[HG_GROUP + h] = jnp.zeros((HG_DV, HG_DK), F32)

    rows = lax.broadcasted_iota(jnp.int32, (C, HG_DK), 0)
    ti = lax.broadcasted_iota(jnp.int32, (2 * C, 2 * C), 0)
    si = lax.broadcasted_iota(jnp.int32, (2 * C, 2 * C), 1)
    same_chunk = (ti // C) == (si // C)
    p1_sees_p0 = jnp.logical_and(ti >= C, si < C)
    pair_mask_f = jnp.logical_or(jnp.logical_and(same_chunk, si <= ti), p1_sees_p0)
    pair_mask_b = jnp.logical_or(jnp.logical_and(same_chunk, si >= ti), p1_sees_p0)
    dirs = ((hff_ref, lbf_ref, of_scr, False, pair_mask_f, C // 2 - 1, C - 1),
            (hfb_ref, lbb_ref, ob_scr, True, pair_mask_b, C // 2, 0))
    n_pairs = n_chunks // 2
    zeros = jnp.zeros((C, HG_DK), BF16)

    def chunk_prep(z_ref, lb_ref, reverse, mid, last, rs, cols):
        q = q_scr[rs, cols]
        v = hi_ref[rs, cols].astype(BF16)
        lb = lb_ref[:, cols]
        sig = _sigmoid(z_ref[rs, cols])
        log_f = jnp.log(lb + (1.0 - lb) * sig)
        k = (1.0 - lb) * (1.0 - sig)
        b = _chunk_cumsum(log_f, rows, reverse)
        b_mid = b[mid:mid + 1, :]
        b_last = b[last:last + 1, :]
        e_mid = jnp.exp(b - b_mid)
        qe = q * e_mid
        ke = k * (1.0 / e_mid)
        return dict(v=v, qe=qe.astype(BF16), ke=ke.astype(BF16), qb=qe * jnp.exp(b_mid),
                    kl=ke * jnp.exp(b_last - b_mid), decay=jnp.exp(b_last))

    def pair_step(c, carry):
        f0 = pl.multiple_of(c * 2 * C, C)
        b1 = pl.multiple_of((n_pairs - 1 - c) * 2 * C, C)
        starts = ((f0, pl.multiple_of(f0 + C, C)), (pl.multiple_of(b1 + C, C), b1))
        streams = [(h, d) for h in range(HG_GROUP) for d in range(2)]
        prep = []
        for h, d in streams:
            z_ref, lb_ref, _, reverse, _, mid, last = dirs[d]
            cols = slice(h * HG_DK, (h + 1) * HG_DK)
            p0, p1 = (chunk_prep(z_ref, lb_ref, reverse, mid, last, pl.ds(r, C), cols) for r in starts[d])
            cat = lambda a, b: jnp.concatenate([a, b], axis=0)
            prep.append(dict(
                v=cat(p0["v"], p1["v"]),
                a_lhs=cat(jnp.concatenate([p0["qe"], zeros, zeros], axis=1),
                          jnp.concatenate([zeros, p1["qb"].astype(BF16), p1["qe"]], axis=1)),
                a_rhs=cat(jnp.concatenate([p0["ke"], p0["kl"].astype(BF16), zeros], axis=1),
                          jnp.concatenate([zeros, zeros, p1["ke"]], axis=1)),
                qb=cat(p0["qb"], p1["qb"] * p0["decay"]).astype(BF16),
                kl=cat(p0["kl"] * p1["decay"], p1["kl"]).astype(BF16),
                decay=p0["decay"] * p1["decay"]))
        a_raw = [lax.dot_general(p["a_lhs"], p["a_rhs"], NT_DIMS, preferred_element_type=F32) for p in prep]
        states = [st_scr[d * HG_GROUP + h] for h, d in streams]
        o_inter = [lax.dot_general(p["qb"], st.astype(BF16), NT_DIMS, preferred_element_type=F32)
                   for p, st in zip(prep, states)]
        u_t = [lax.dot_general(p["v"], p["kl"], TN_DIMS, preferred_element_type=F32) for p in prep]
        o_intra = [jnp.dot(jnp.where(dirs[d][4], a, 0.0).astype(BF16), p["v"], preferred_element_type=F32)
                   for (h, d), a, p in zip(streams, a_raw, prep)]
        for i, (h, d) in enumerate(streams):
            cols = slice(h * HG_DK, (h + 1) * HG_DK)
            o = o_intra[i] + o_inter[i]
            dirs[d][2][pl.ds(starts[d][0], C), cols] = o[:C]
            dirs[d][2][pl.ds(starts[d][1], C), cols] = o[C:]
            st_scr[d * HG_GROUP + h] = states[i] * prep[i]["decay"] + u_t[i]
        return carry

    lax.fori_loop(0, n_pairs, pair_step, 0, unroll=4)

    for h in range(HG_GROUP):
        cols = slice(h * HG_DK, (h + 1) * HG_DK)
        o = of_scr[:, cols] + ob_scr[:, cols]
        o = o * lax.rsqrt(jnp.mean(o * o, -1, keepdims=True) + RMS_EPS) * ng_ref[...]
        gt = hgt_ref[:, cols]
        o_ref[:, cols] = (o * (gt * _sigmoid(gt))).astype(o_ref.dtype)
        if want_state:
            sf_ref[h] = st_scr[h].T
            sb_ref[h] = st_scr[HG_GROUP + h].T


def _hgrn(z, lb_f, lb_b, norm_g, row0, n_batch, seq, init_states=None, layer=None, state_bufs=None):
    n_groups = HG_HEADS // HG_GROUP
    r0 = row0 // seq
    has_init = init_states is not None
    want_state = state_bufs is not None

    def zspec(off):
        return pl.BlockSpec((seq, HG_GW), lambda b, g, o=off // HG_GW: (r0 + b, o + g))

    lbspec = pl.BlockSpec((1, HG_GW), lambda b, g: (0, g))
    in_specs = [zspec(HQ_OFF), zspec(HFF_OFF), zspec(HFB_OFF), zspec(HI_OFF), zspec(HGT_OFF),
                lbspec, lbspec, pl.BlockSpec((1, HG_DV), lambda b, g: (0, 0))]
    args = [z, z, z, z, z, lb_f.reshape(1, HG_DIM), lb_b.reshape(1, HG_DIM), norm_g.reshape(1, HG_DV)]
    if has_init:
        st_spec = pl.BlockSpec((None, None, HG_GROUP, HG_DK, HG_DV), lambda b, g: (b, layer, g, 0, 0))
        in_specs += [st_spec, st_spec]
        args += list(init_states)
    out_specs = [pl.BlockSpec((seq, HG_GW), lambda b, g: (b, g))]
    out_shape = [jax.ShapeDtypeStruct((n_batch * seq, HG_VDIM), BF16)]
    aliases = {}
    if want_state:
        so = pl.BlockSpec((None, None, HG_GROUP, HG_DK, HG_DV), lambda b, g: (layer, b, g, 0, 0))
        for buf in state_bufs:
            aliases[len(args)] = len(out_specs)
            in_specs.append(pl.BlockSpec(memory_space=pl.ANY))
            args.append(buf)
            out_specs.append(so)
            out_shape.append(jax.ShapeDtypeStruct(buf.shape, buf.dtype))
    return pl.pallas_call(
        functools.partial(_hgrn_kernel, seq=seq, has_init=has_init, want_state=want_state),
        grid=(n_batch, n_groups),
        in_specs=in_specs,
        out_specs=out_specs,
        out_shape=out_shape,
        scratch_shapes=[pltpu.VMEM((seq, HG_GW), F32), pltpu.VMEM((seq, HG_GW), F32),
                        pltpu.VMEM((seq, HG_GW), F32),
                        pltpu.VMEM((2 * HG_GROUP, HG_DV, HG_DK), F32)],
        input_output_aliases=aliases,
        compiler_params=_params(2),
        name=f"hgrn_t{seq}",
    )(*args)


def _oproj_kernel(*refs, n_x, n_ctx_tiles):
    attc_ref, attl_ref, hgc_ref, hgl_ref, w_ref = refs[:5]
    x_refs = refs[5:5 + n_x]
    mod_ref, g_ref, b_ref, o_ref = refs[5 + n_x:]
    is_ctx = pl.program_id(0) < n_ctx_tiles
    for r0 in range(0, OPROJ_TM, OPROJ_SUB):
        rows = slice(r0, r0 + OPROJ_SUB)
        att = jnp.where(is_ctx, attc_ref[rows], attl_ref[rows])
        hg = jnp.where(is_ctx, hgc_ref[rows], hgl_ref[rows])
        mix = (jnp.dot(att, w_ref[0:ATT_DIM, :], preferred_element_type=F32)
               + jnp.dot(hg, w_ref[ATT_DIM:MIX_DIM, :], preferred_element_type=F32))
        y = DEEPNORM_ALPHA * _pick_rows(x_refs, is_ctx, rows) + mod_ref[2:3, :] * mix
        o_ref[rows] = _layer_norm(y, g_ref[0:1, :], b_ref[0:1, :])


def _oproj(att_c, att_l, hg_c, hg_l, w_o, xs, mods, ln_g, ln_b, layer, mod_idx):
    n_tok = sum(x.shape[0] for x in xs)
    n_ctx_tiles = att_c.shape[0] // OPROJ_TM
    row = lambda i: (i, 0)
    ctx_row = lambda i: (jnp.minimum(i, n_ctx_tiles - 1), 0)
    lat_row = lambda i: (jnp.maximum(i - n_ctx_tiles, 0), 0)
    return pl.pallas_call(
        functools.partial(_oproj_kernel, n_x=len(xs), n_ctx_tiles=n_ctx_tiles),
        grid=(n_tok // OPROJ_TM,),
        in_specs=[pl.BlockSpec((OPROJ_TM, ATT_DIM), ctx_row),
                  pl.BlockSpec((OPROJ_TM, ATT_DIM), lat_row),
                  pl.BlockSpec((OPROJ_TM, HG_VDIM), ctx_row),
                  pl.BlockSpec((OPROJ_TM, HG_VDIM), lat_row),
                  pl.BlockSpec((MIX_DIM, D_MODEL), lambda i: (0, 0))]
        + _token_specs(xs, OPROJ_TM, n_ctx_tiles, 1) + [
                  pl.BlockSpec((None, None, 6, D_MODEL), lambda i: (layer, mod_idx(i), 0, 0)),
                  pl.BlockSpec((None, 2, D_MODEL), lambda i: (layer, 0, 0)),
                  pl.BlockSpec((None, 2, D_MODEL), lambda i: (layer, 0, 0))],
        out_specs=pl.BlockSpec((OPROJ_TM, D_MODEL), row),
        out_shape=jax.ShapeDtypeStruct((n_tok, D_MODEL), F32),
        compiler_params=_params(1),
        name="oproj_ln",
    )(att_c, att_l, hg_c, hg_l, w_o, *xs, mods, ln_g, ln_b)


def _mlp_kernel(*refs, n_cast, n_out, n_ctx_tiles):
    x_ref, mod_ref, wu_ref, wd_ref, g_ref, b_ref = refs[:6]
    cast_in = refs[6:6 + n_cast]
    outs = refs[6 + n_cast:6 + n_cast + n_out]
    cast_out = refs[6 + n_cast + n_out:6 + 2 * n_cast + n_out]
    h_ref, acc_ref = refs[6 + 2 * n_cast + n_out:]
    i, f = pl.program_id(0), pl.program_id(1)
    n_f = pl.num_programs(1)

    @pl.when(f == 0)
    def _():
        shift, scale = mod_ref[3:4, :], mod_ref[4:5, :]
        h_ref[...] = (x_ref[...] * (1.0 + scale) + shift).astype(BF16)
        acc_ref[...] = jnp.zeros_like(acc_ref)

    if n_cast:
        @pl.when(i * n_f + f < CAST_STEPS)
        def _():
            for src, dst in zip(cast_in, cast_out):
                dst[...] = src[...].astype(BF16)

    u = jnp.dot(h_ref[...], wu_ref[...], preferred_element_type=F32)
    u = jnp.square(jnp.maximum(u, 0.0)).astype(BF16)
    for n in range(0, D_MODEL, MLP_TN):
        acc_ref[:, n:n + MLP_TN] += jnp.dot(u, wd_ref[:, n:n + MLP_TN], preferred_element_type=F32)

    def finish(o_ref):
        y = DEEPNORM_ALPHA * x_ref[...] + mod_ref[5:6, :] * acc_ref[...]
        o_ref[...] = _layer_norm(y, g_ref[1:2, :], b_ref[1:2, :])

    last = f == n_f - 1
    if n_out == 1:
        pl.when(last)(lambda: finish(outs[0]))
    else:
        pl.when(jnp.logical_and(last, i < n_ctx_tiles))(lambda: finish(outs[0]))
        pl.when(jnp.logical_and(last, i >= n_ctx_tiles))(lambda: finish(outs[1]))


def _cast_specs(weights, layer):
    n_f = D_FF // MLP_TF
    in_specs, out_specs, out_shape = [], [], []
    for w in weights:
        _, rows, cols = w.shape
        slab = rows // CAST_STEPS
        in_specs.append(pl.BlockSpec(
            (None, slab, cols), lambda i, f: (layer, jnp.minimum(i * n_f + f, CAST_STEPS - 1), 0)))
        out_specs.append(pl.BlockSpec(
            (slab, cols), lambda i, f: (jnp.minimum(i * n_f + f, CAST_STEPS - 1), 0)))
        out_shape.append(jax.ShapeDtypeStruct((rows, cols), BF16))
    return in_specs, out_specs, out_shape


def _mlp(x, mods, w_up, w_down, ln_g, ln_b, layer, mod_idx, next_weights=(), split_rows=None):
    n_tok = x.shape[0]
    n_f = D_FF // MLP_TF
    assert (n_tok // MLP_TM) * n_f >= CAST_STEPS
    cast_in_specs, cast_out_specs, cast_shapes = _cast_specs(next_weights, layer + 1)
    if split_rows is None:
        n_ctx_tiles = n_tok // MLP_TM
        y_specs = [pl.BlockSpec((MLP_TM, D_MODEL), lambda i, f: (i, 0))]
        y_shapes = [jax.ShapeDtypeStruct((n_tok, D_MODEL), F32)]
    else:
        n_ctx_tiles = split_rows // MLP_TM
        y_specs = [pl.BlockSpec((MLP_TM, D_MODEL), lambda i, f: (jnp.minimum(i, n_ctx_tiles - 1), 0)),
                   pl.BlockSpec((MLP_TM, D_MODEL), lambda i, f: (jnp.maximum(i - n_ctx_tiles, 0), 0))]
        y_shapes = [jax.ShapeDtypeStruct((split_rows, D_MODEL), F32),
                    jax.ShapeDtypeStruct((n_tok - split_rows, D_MODEL), F32)]
    return pl.pallas_call(
        functools.partial(_mlp_kernel, n_cast=len(next_weights), n_out=len(y_specs),
                          n_ctx_tiles=n_ctx_tiles),
        grid=(n_tok // MLP_TM, n_f),
        in_specs=[pl.BlockSpec((MLP_TM, D_MODEL), lambda i, f: (i, 0)),
                  pl.BlockSpec((None, None, 6, D_MODEL), lambda i, f: (layer, mod_idx(i), 0, 0)),
                  pl.BlockSpec((D_MODEL, MLP_TF), lambda i, f: (0, f)),
                  pl.BlockSpec((MLP_TF, D_MODEL), lambda i, f: (f, 0)),
                  pl.BlockSpec((None, 2, D_MODEL), lambda i, f: (layer, 0, 0)),
                  pl.BlockSpec((None, 2, D_MODEL), lambda i, f: (layer, 0, 0))] + cast_in_specs,
        out_specs=y_specs + cast_out_specs,
        out_shape=y_shapes + cast_shapes,
        scratch_shapes=[pltpu.VMEM((MLP_TM, D_MODEL), BF16), pltpu.VMEM((MLP_TM, D_MODEL), F32)],
        compiler_params=_params(2),
        name="mlp_ln",
    )(x, mods, w_up, w_down, ln_g, ln_b, *next_weights)


def _cache_pack_kernel(*refs, depth):
    k_refs, v_refs = refs[:depth], refs[depth:2 * depth]
    ok_ref, ov_ref = refs[2 * depth:]
    for l in range(depth):
        for h in range(KV_HEADS):
            cols = slice(h * HEAD_DIM, (h + 1) * HEAD_DIM)
            ok_ref[0, l, :, h, :] = k_refs[l][:, cols]
            ov_ref[0, l, :, h, :] = v_refs[l][:, cols]


def _cache_pack(ks, vs, n_batch, seq):
    depth = len(ks)
    in_spec = pl.BlockSpec((seq, KV_DIM), lambda b: (b, 0))
    out_spec = pl.BlockSpec((1, depth, seq, KV_HEADS, HEAD_DIM), lambda b: (b, 0, 0, 0, 0))
    out_shape = jax.ShapeDtypeStruct((n_batch, depth, seq, KV_HEADS, HEAD_DIM), F32)
    return pl.pallas_call(
        functools.partial(_cache_pack_kernel, depth=depth),
        grid=(n_batch,),
        in_specs=[in_spec] * (2 * depth),
        out_specs=[out_spec, out_spec],
        out_shape=[out_shape, out_shape],
        compiler_params=_params(1),
        name="cache_pack",
    )(*ks, *vs)


def _lower_bounds(lb_logits):
    p = jax.nn.softmax(lb_logits.astype(F32), axis=0)
    cs = jnp.cumsum(p, axis=0)
    return cs - cs[0:1]


def kernel(x_prompt, x_sample, cache_k, cache_v, state_hgrn_fwd, state_hgrn_bwd, c, c_ctx, w_mod, b_mod,
           w_in, attn_sink, attn_norm_g, hg_lb_logits, hg_norm_g, w_o, ln_g, ln_b, w_up, w_down):
    bp, seq_p, d = x_prompt.shape
    bs, seq_s, _ = x_sample.shape
    depth = w_in.shape[0]
    past = cache_k.shape[2]
    n_ctx, n_lat = bp * seq_p, bs * seq_s

    xs = (x_prompt.reshape(n_ctx, d), x_sample.reshape(n_lat, d))
    c_rows = jnp.concatenate([c_ctx[None, :], c, jnp.zeros((MOD_ROWS - 1 - bs, d), F32)], 0)
    mods = _modulation(c_rows, w_mod, b_mod)[:, :1 + bs].reshape(depth, 1 + bs, 6, d)

    weights = (w_in, w_o, w_up, w_down)
    w_in_b, w_o_b, w_up_b, w_down_b = (w[0].astype(BF16) for w in weights)
    lb_f, lb_b = _lower_bounds(hg_lb_logits[0]), _lower_bounds(hg_lb_logits[1])
    rope_tabs = _rope_tables(seq_s)
    ck = cache_k.reshape(bs, depth, past, KV_DIM)
    cv = cache_v.reshape(bs, depth, past, KV_DIM)

    new_k, new_v = [], []
    s_f = jnp.zeros((depth, bp, HG_HEADS, HG_DK, HG_DV), F32)
    s_b = jnp.zeros((depth, bp, HG_HEADS, HG_DK, HG_DV), F32)
    for l in range(depth):
        z, k_c, v_c = _inproj(xs, mods, w_in_b, l, _mod_index(_inproj_tile(xs), n_ctx, seq_s), n_ctx)
        new_k.append(k_c)
        new_v.append(v_c)

        att_c = _ctx_attention(z, attn_sink[l], attn_norm_g[l], bp, seq_p)
        att_l = _lat_attention(z, ck, cv, l, attn_sink[l], attn_norm_g[l], rope_tabs, n_ctx, bs, seq_s)

        hg_c, s_f, s_b = _hgrn(z, lb_f[l], lb_b[l], hg_norm_g[l], 0, bp, seq_p, layer=l,
                               state_bufs=(s_f, s_b))
        (hg_l,) = _hgrn(z, lb_f[l], lb_b[l], hg_norm_g[l], n_ctx, bs, seq_s,
                        init_states=(state_hgrn_fwd, state_hgrn_bwd), layer=l)

        x = _oproj(att_c, att_l, hg_c, hg_l, w_o_b, xs, mods, ln_g, ln_b, l,
                   _mod_index(OPROJ_TM, n_ctx, seq_s))
        mlp_mods = _mod_index(MLP_TM, n_ctx, seq_s)
        if l + 1 < depth:
            x, w_in_b, w_o_b, w_up_b, w_down_b = _mlp(x, mods, w_up_b, w_down_b, ln_g, ln_b, l, mlp_mods,
                                                      next_weights=weights)
            xs = (x,)
        else:
            y_p, y_s = _mlp(x, mods, w_up_b, w_down_b, ln_g, ln_b, l, mlp_mods, split_rows=n_ctx)

    new_cache_k, new_cache_v = _cache_pack(new_k, new_v, bp, seq_p)
    return (y_p.reshape(bp, seq_p, d), y_s.reshape(bs, seq_s, d), new_cache_k, new_cache_v,
            jnp.swapaxes(s_f, 0, 1), jnp.swapaxes(s_b, 0, 1))
```

```python
import functools

import jax
import jax.numpy as jnp
from jax import lax
from jax.experimental import pallas as pl
from jax.experimental.pallas import tpu as pltpu

F32 = jnp.float32
BF16 = jnp.bfloat16

D_MODEL = 2048
DEPTH = 4
GRID_W = 64
HEAD_DIM = 128
ATT_HEADS = 8
KV_HEADS = 2
Q_PER_KV = ATT_HEADS // KV_HEADS
ATT_DIM = ATT_HEADS * HEAD_DIM
KV_DIM = KV_HEADS * HEAD_DIM
WINDOW = 128
BLOCK = 128
ATT_SCALE = HEAD_DIM ** -0.5
ROPE_THETA = 10000.0
ROPE_FREQS = HEAD_DIM // 4
HG_HEADS = 8
HG_DK = 128
HG_DV = 128
HG_DIM = HG_HEADS * HG_DK
HG_VDIM = HG_HEADS * HG_DV
HG_CHUNK = 32
MIX_DIM = ATT_DIM + HG_VDIM
IN_DIM = ATT_DIM + 2 * KV_DIM + 3 * HG_DIM + 2 * HG_VDIM
D_FF = 4 * D_MODEL
LN_EPS = 1e-5
RMS_EPS = 1e-6
DEEPNORM_ALPHA = (2 * DEPTH) ** 0.25
NEG_INF = -1e30
LOG2E = 1.4426950408889634

Q_OFF = 0
K_OFF = ATT_DIM
V_OFF = K_OFF + KV_DIM
HQ_OFF = V_OFF + KV_DIM
HFF_OFF = HQ_OFF + HG_DIM
HFB_OFF = HFF_OFF + HG_DIM
HI_OFF = HFB_OFF + HG_DIM
HGT_OFF = HI_OFF + HG_VDIM

V7X_VMEM_LIMIT_BYTES = 56 * 1024 * 1024

MOD_ROWS = 8
MOD_TN = 768
TOKEN_TILE = 1024
INPROJ_TN = 1664
OPROJ_TM = 512
OPROJ_SUB = 256
MLP_TM = 512
MLP_TF = 1024
MLP_TN = 512
CAST_STEPS = 128
HG_GROUP = 4
CTX_ATT_SEQS = 2
HG_SEQS = 2
HG_MAX_ROWS = 512
HG_GW = HG_GROUP * HG_DK

NT_DIMS = (((1,), (1,)), ((), ()))
TN_DIMS = (((0,), (0,)), ((), ()))


def _params(n_axes):
    return pltpu.CompilerParams(dimension_semantics=("arbitrary",) * n_axes,
                                vmem_limit_bytes=V7X_VMEM_LIMIT_BYTES)


def _sigmoid(x):
    return 1.0 / (1.0 + jnp.exp(-x))


def _layer_norm(y, g, b):
    mu = jnp.mean(y, -1, keepdims=True)
    yc = y - mu
    var = jnp.mean(yc * yc, -1, keepdims=True)
    return yc * lax.rsqrt(var + LN_EPS) * g + b


def _mod_kernel(c_ref, w_ref, b_ref, o_ref):
    c = c_ref[...]
    a = (c * _sigmoid(c)).astype(BF16)
    o_ref[0] = jnp.dot(a, w_ref[0].astype(BF16), preferred_element_type=F32) + b_ref[0]


def _modulation(c_rows, w_mod, b_mod):
    depth, d, n = w_mod.shape
    return pl.pallas_call(
        _mod_kernel,
        grid=(depth, n // MOD_TN),
        in_specs=[pl.BlockSpec((MOD_ROWS, d), lambda l, j: (0, 0)),
                  pl.BlockSpec((1, d, MOD_TN), lambda l, j: (l, 0, j)),
                  pl.BlockSpec((1, 1, MOD_TN), lambda l, j: (l, 0, j))],
        out_specs=pl.BlockSpec((1, MOD_ROWS, MOD_TN), lambda l, j: (l, 0, j)),
        out_shape=jax.ShapeDtypeStruct((depth, MOD_ROWS, n), F32),
        compiler_params=_params(2),
        name="modulation",
    )(c_rows, w_mod, b_mod.reshape(depth, 1, n))


def _mod_index(tile_rows, n_ctx_rows, lat_rows):
    n_ctx_tiles = n_ctx_rows // tile_rows
    per_lat = lat_rows // tile_rows

    def idx(i):
        return jnp.where(i < n_ctx_tiles, 0, 1 + (i - n_ctx_tiles) // per_lat)
    return idx


def _pick_rows(x_refs, is_ctx, rows=slice(None)):
    if len(x_refs) == 1:
        return x_refs[0][rows]
    return jnp.where(is_ctx, x_refs[0][rows], x_refs[1][rows])


def _inproj_kernel(*refs, n_x, n_ctx_tiles):
    x_refs = refs[:n_x]
    mod_ref, w_ref, z_ref, k_ref, v_ref, h_ref = refs[n_x:]
    i, j = pl.program_id(0), pl.program_id(1)

    @pl.when(j == 0)
    def _():
        shift, scale = mod_ref[0:1, :], mod_ref[1:2, :]
        h_ref[...] = (_pick_rows(x_refs, i < n_ctx_tiles) * (1.0 + scale) + shift).astype(BF16)

    z_ref[...] = jnp.dot(h_ref[...], w_ref[...], preferred_element_type=F32)

    @pl.when(jnp.logical_and(j == 0, i < n_ctx_tiles))
    def _():
        k_ref[...] = z_ref[:, K_OFF:K_OFF + KV_DIM]
        v_ref[...] = z_ref[:, V_OFF:V_OFF + KV_DIM]


def _token_specs(xs, tile, n_ctx_tiles, n_grid_axes):
    def spec(row_of):
        index_map = ((lambda i: (row_of(i), 0)) if n_grid_axes == 1 else (lambda i, j: (row_of(i), 0)))
        return pl.BlockSpec((tile, D_MODEL), index_map)
    if len(xs) == 1:
        return [spec(lambda i: i)]
    return [spec(lambda i: jnp.minimum(i, n_ctx_tiles - 1)), spec(lambda i: jnp.maximum(i - n_ctx_tiles, 0))]


def _inproj_tile(xs):
    return TOKEN_TILE // len(xs)


def _inproj(xs, mods, w_in, layer, mod_idx, n_ctx):
    assert V_OFF + KV_DIM <= INPROJ_TN
    tile = _inproj_tile(xs)
    n_tok = sum(x.shape[0] for x in xs)
    n_ctx_tiles = n_ctx // tile
    kv_spec = pl.BlockSpec((tile, KV_DIM), lambda i, j: (jnp.minimum(i, n_ctx_tiles - 1), 0))
    return pl.pallas_call(
        functools.partial(_inproj_kernel, n_x=len(xs), n_ctx_tiles=n_ctx_tiles),
        grid=(n_tok // tile, IN_DIM // INPROJ_TN),
        in_specs=_token_specs(xs, tile, n_ctx_tiles, 2) + [
                  pl.BlockSpec((None, None, 6, D_MODEL), lambda i, j: (layer, mod_idx(i), 0, 0)),
                  pl.BlockSpec((D_MODEL, INPROJ_TN), lambda i, j: (0, j))],
        out_specs=[pl.BlockSpec((tile, INPROJ_TN), lambda i, j: (i, j)), kv_spec, kv_spec],
        out_shape=[jax.ShapeDtypeStruct((n_tok, IN_DIM), F32),
                   jax.ShapeDtypeStruct((n_ctx, KV_DIM), F32),
                   jax.ShapeDtypeStruct((n_ctx, KV_DIM), F32)],
        scratch_shapes=[pltpu.VMEM((tile, D_MODEL), BF16)],
        compiler_params=_params(2),
        name="inproj",
    )(*xs, mods, w_in)


def _with_ones(v):
    return jnp.concatenate([v, jnp.ones_like(v)], axis=1)


def _sink_attend(scores, values, sink2):
    c = ATT_SCALE * LOG2E
    m_raw = jnp.max(scores[0], -1, keepdims=True)
    for s in scores[1:]:
        m_raw = jnp.maximum(m_raw, jnp.max(s, -1, keepdims=True))
    m2 = jnp.maximum(m_raw * c, sink2)
    acc = None
    for s, v1 in zip(scores, values):
        e = jnp.exp2(s * c - m2).astype(BF16)
        part = jnp.dot(e, v1, preferred_element_type=F32)
        acc = part if acc is None else acc + part
    den = acc[:, HEAD_DIM:HEAD_DIM + 1] + jnp.exp2(sink2 - m2)
    return acc[:, :HEAD_DIM] * (1.0 / den)


def _rms_norm_store(o_scr, sq, g_ref, o_ref):
    ss = jnp.sum(sq, -1, keepdims=True)
    inv = lax.rsqrt(ss * (1.0 / ATT_DIM) + RMS_EPS)
    o_ref[...] = (o_scr[...] * inv * g_ref[...]).astype(o_ref.dtype)


def _head_cols(g, r):
    h = g * Q_PER_KV + r
    return slice(h * HEAD_DIM, (h + 1) * HEAD_DIM)


def _group_sinks(sink_ref, g, rows):
    return jnp.concatenate([jnp.full((rows, 1), sink_ref[g * Q_PER_KV + r] * LOG2E, F32)
                            for r in range(Q_PER_KV)], axis=0)


def _scatter_heads(og, g, rows, sq, o_scr):
    for r in range(Q_PER_KV):
        oh = og[r * rows:(r + 1) * rows]
        sq = sq + oh * oh
        o_scr[:, _head_cols(g, r)] = oh
    return sq


def _ctx_attn_kernel(sink_ref, q_ref, k_ref, v_ref, g_ref, o_ref, o_scr, *, seq):
    c = ATT_SCALE * LOG2E
    kv_cols = [slice(g * HEAD_DIM, (g + 1) * HEAD_DIM) for g in range(KV_HEADS)]
    heads = [(g, r) for g in range(KV_HEADS) for r in range(Q_PER_KV)]
    for r0 in range(0, q_ref.shape[0], seq):
        rows = slice(r0, r0 + seq)
        sq = jnp.zeros((seq, HEAD_DIM), F32)
        ks = [k_ref[rows, cols].astype(BF16) for cols in kv_cols]
        vs = [v_ref[rows, cols].astype(BF16) for cols in kv_cols]
        scores = [lax.dot_general(q_ref[rows, _head_cols(g, r)].astype(BF16), ks[g], NT_DIMS,
                                  preferred_element_type=F32) for g, r in heads]
        probs = []
        for (g, r), s in zip(heads, scores):
            sink2 = sink_ref[g * Q_PER_KV + r] * LOG2E
            m2 = jnp.maximum(jnp.max(s, -1, keepdims=True) * c, sink2)
            e = jnp.exp2(s * c - m2)
            den = jnp.sum(e, -1, keepdims=True) + jnp.exp2(sink2 - m2)
            probs.append((e * (1.0 / den)).astype(BF16))
        outs = [jnp.dot(p, vs[g], preferred_element_type=F32) for (g, r), p in zip(heads, probs)]
        for (g, r), oh in zip(heads, outs):
            sq = sq + oh * oh
            o_scr[rows, _head_cols(g, r)] = oh
        _rms_norm_store(o_scr.at[rows], sq, g_ref, o_ref.at[rows])


def _ctx_attention(z, sink, norm_g, n_batch, seq):
    kb, vb = K_OFF // KV_DIM, V_OFF // KV_DIM
    rows = CTX_ATT_SEQS * seq
    return pl.pallas_call(
        functools.partial(_ctx_attn_kernel, seq=seq),
        grid=(n_batch // CTX_ATT_SEQS,),
        in_specs=[pl.BlockSpec(memory_space=pltpu.SMEM),
                  pl.BlockSpec((rows, ATT_DIM), lambda b: (b, 0)),
                  pl.BlockSpec((rows, KV_DIM), lambda b: (b, kb)),
                  pl.BlockSpec((rows, KV_DIM), lambda b: (b, vb)),
                  pl.BlockSpec((1, ATT_DIM), lambda b: (0, 0))],
        out_specs=pl.BlockSpec((rows, ATT_DIM), lambda b: (b, 0)),
        out_shape=jax.ShapeDtypeStruct((n_batch * seq, ATT_DIM), BF16),
        scratch_shapes=[pltpu.VMEM((rows, ATT_DIM), F32)],
        compiler_params=_params(1),
        name="ctx_attention",
    )(sink, z, z, z, norm_g.reshape(1, ATT_DIM))


def _rope(x, cos, sin_lo, sin_hi):
    return (x * cos + pltpu.roll(x, HEAD_DIM - ROPE_FREQS, 1) * sin_lo
            + pltpu.roll(x, ROPE_FREQS, 1) * sin_hi)


def _lat_attn_kernel(sink_ref, q_ref, k_ref, v_ref, ck_ref, cv_ref, cq_ref, slq_ref, shq_ref,
                     ca_ref, sla_ref, sha_ref, g_ref, o_ref, kr_scr, v_scr, ckb_scr, cvb_scr, o_scr):
    n = pl.program_id(1)
    seq = k_ref.shape[0]
    win = 3 * BLOCK

    @pl.when(n == 0)
    def _():
        for g in range(KV_HEADS):
            cols = slice(g * HEAD_DIM, (g + 1) * HEAD_DIM)
            kr_scr[:, cols] = _rope(k_ref[:, cols], ca_ref[...], sla_ref[...], sha_ref[...]).astype(BF16)
            v_scr[g] = _with_ones(v_ref[:, cols].astype(BF16))
            cvb_scr[g] = _with_ones(cv_ref[:, cols].astype(BF16))
        ckb_scr[...] = ck_ref[...].astype(BF16)

    k0 = pl.multiple_of(jnp.clip((n - 1) * BLOCK, 0, seq - win), BLOCK)
    rows = Q_PER_KV * BLOCK
    qpos = n * BLOCK + lax.broadcasted_iota(jnp.int32, (rows, win), 0) % BLOCK
    kpos = k0 + lax.broadcasted_iota(jnp.int32, (rows, win), 1)
    valid = jnp.abs(kpos - qpos) <= WINDOW
    sq = jnp.zeros((BLOCK, HEAD_DIM), F32)
    for g in range(KV_HEADS):
        kv_cols = slice(g * HEAD_DIM, (g + 1) * HEAD_DIM)
        k_loc = kr_scr[pl.ds(k0, win), kv_cols]
        v_loc = v_scr[g, pl.ds(k0, win), :]
        k_ctx = ckb_scr[:, kv_cols]
        v_ctx = cvb_scr[g]
        qg = jnp.concatenate(
            [_rope(q_ref[:, _head_cols(g, r)], cq_ref[...], slq_ref[...], shq_ref[...]).astype(BF16)
             for r in range(Q_PER_KV)], axis=0)
        s_loc = lax.dot_general(qg, k_loc, NT_DIMS, preferred_element_type=F32)
        s_loc = jnp.where(valid, s_loc, NEG_INF / ATT_SCALE)
        s_ctx = lax.dot_general(qg, k_ctx, NT_DIMS, preferred_element_type=F32)
        og = _sink_attend([s_loc, s_ctx], [v_loc, v_ctx], _group_sinks(sink_ref, g, BLOCK))
        sq = _scatter_heads(og, g, BLOCK, sq, o_scr)
    _rms_norm_store(o_scr, sq, g_ref, o_ref)


def _lat_attention(z, cache_k, cache_v, layer, sink, norm_g, rope_tabs, row0, n_batch, seq):
    kb, vb = K_OFF // KV_DIM, V_OFF // KV_DIM
    nq = seq // BLOCK
    qrow0, srow0 = row0 // BLOCK, row0 // seq
    past = cache_k.shape[2]
    cos, sin_lo, sin_hi = rope_tabs
    qtab = pl.BlockSpec((BLOCK, HEAD_DIM), lambda b, n: (n, 0))
    atab = pl.BlockSpec((seq, HEAD_DIM), lambda b, n: (0, 0))
    cache_spec = pl.BlockSpec((None, None, past, KV_DIM), lambda b, n: (b, layer, 0, 0))
    return pl.pallas_call(
        _lat_attn_kernel,
        grid=(n_batch, nq),
        in_specs=[pl.BlockSpec(memory_space=pltpu.SMEM),
                  pl.BlockSpec((BLOCK, ATT_DIM), lambda b, n: (qrow0 + b * nq + n, 0)),
                  pl.BlockSpec((seq, KV_DIM), lambda b, n: (srow0 + b, kb)),
                  pl.BlockSpec((seq, KV_DIM), lambda b, n: (srow0 + b, vb)),
                  cache_spec, cache_spec, qtab, qtab, qtab, atab, atab, atab,
                  pl.BlockSpec((1, ATT_DIM), lambda b, n: (0, 0))],
        out_specs=pl.BlockSpec((BLOCK, ATT_DIM), lambda b, n: (b * nq + n, 0)),
        out_shape=jax.ShapeDtypeStruct((n_batch * seq, ATT_DIM), BF16),
        scratch_shapes=[pltpu.VMEM((seq, KV_DIM), BF16), pltpu.VMEM((KV_HEADS, seq, 2 * HEAD_DIM), BF16),
                        pltpu.VMEM((past, KV_DIM), BF16), pltpu.VMEM((KV_HEADS, past, 2 * HEAD_DIM), BF16),
                        pltpu.VMEM((BLOCK, ATT_DIM), F32)],
        compiler_params=_params(2),
        name="lat_attention",
    )(sink, z, z, z, cache_k, cache_v, cos, sin_lo, sin_hi, cos, sin_lo, sin_hi,
      norm_g.reshape(1, ATT_DIM))


def _rope_tables(seq):
    rows = seq // GRID_W
    row = jnp.repeat(jnp.arange(rows, dtype=F32), GRID_W)
    col = jnp.tile(jnp.arange(GRID_W, dtype=F32), rows)
    inv = ROPE_THETA ** (-jnp.arange(ROPE_FREQS, dtype=F32) / ROPE_FREQS)
    ang_r, ang_c = row[:, None] * inv, col[:, None] * inv
    cr, sr, cc, sc = jnp.cos(ang_r), jnp.sin(ang_r), jnp.cos(ang_c), jnp.sin(ang_c)
    zero = jnp.zeros_like(sr)
    cos = jnp.concatenate([cr, cr, cc, cc], -1)
    sin_lo = jnp.concatenate([-sr, zero, -sc, zero], -1)
    sin_hi = jnp.concatenate([zero, sr, zero, sc], -1)
    return cos, sin_lo, sin_hi


def _chunk_cumsum(x, rows, reverse):
    s = 1
    while s < HG_CHUNK:
        if reverse:
            x = x + jnp.where(rows < HG_CHUNK - s, pltpu.roll(x, HG_CHUNK - s, 0), 0.0)
        else:
            x = x + jnp.where(rows >= s, pltpu.roll(x, s, 0), 0.0)
        s *= 2
    return x


def _hgrn_kernel(*refs, seq, n_seq, has_init, want_state):
    hq_ref, hff_ref, hfb_ref, hi_ref, hgt_ref, lbf_ref, lbb_ref, ng_ref = refs[:8]
    pos = 8
    if has_init:
        sf0_ref, sb0_ref = refs[pos:pos + 2]
        pos += 2
    if want_state:
        pos += 2
    o_ref = refs[pos]
    pos += 1
    if want_state:
        sf_ref, sb_ref = refs[pos:pos + 2]
        pos += 2
    q_scr, of_scr, ob_scr, st_scr = refs[pos:pos + 4]

    C = HG_CHUNK
    n_chunks = seq // C
    q_in = hq_ref[...]
    q_scr[...] = q_in * _sigmoid(q_in)
    def state_slot(s, h, d):
        return (s * 2 + d) * HG_GROUP + h

    for s in range(n_seq):
        for h in range(HG_GROUP):
            if has_init:
                st_scr[state_slot(s, h, 0)] = sf0_ref[s, h].T
                st_scr[state_slot(s, h, 1)] = sb0_ref[s, h].T
            else:
                st_scr[state_slot(s, h, 0)] = jnp.zeros((HG_DV, HG_DK), F32)
                st_scr[state_slot(s, h, 1)] = jnp.zeros((HG_DV, HG_DK), F32)

    rows = lax.broadcasted_iota(jnp.int32, (C, HG_DK), 0)
    ti = lax.broadcasted_iota(jnp.int32, (2 * C, 2 * C), 0)
    si = lax.broadcasted_iota(jnp.int32, (2 * C, 2 * C), 1)
    same_chunk = (ti // C) == (si // C)
    p1_sees_p0 = jnp.logical_and(ti >= C, si < C)
    pair_mask_f = jnp.logical_or(jnp.logical_and(same_chunk, si <= ti), p1_sees_p0)
    pair_mask_b = jnp.logical_or(jnp.logical_and(same_chunk, si >= ti), p1_sees_p0)
    dirs = ((hff_ref, lbf_ref, of_scr, False, pair_mask_f, C // 2 - 1, C - 1),
            (hfb_ref, lbb_ref, ob_scr, True, pair_mask_b, C // 2, 0))
    n_pairs = n_chunks // 2
    zeros = jnp.zeros((C, HG_DK), BF16)

    def chunk_prep(z_ref, lb_ref, reverse, mid, last, rs, cols):
        q = q_scr[rs, cols]
        v = hi_ref[rs, cols].astype(BF16)
        lb = lb_ref[:, cols]
        sig = _sigmoid(z_ref[rs, cols])
        log_f = jnp.log(lb + (1.0 - lb) * sig)
        k = (1.0 - lb) * (1.0 - sig)
        b = _chunk_cumsum(log_f, rows, reverse)
        b_mid = b[mid:mid + 1, :]
        b_last = b[last:last + 1, :]
        e_mid = jnp.exp(b - b_mid)
        qe = q * e_mid
        ke = k * (1.0 / e_mid)
        return dict(v=v, qe=qe.astype(BF16), ke=ke.astype(BF16), qb=qe * jnp.exp(b_mid),
                    kl=ke * jnp.exp(b_last - b_mid), decay=jnp.exp(b_last))

    def pair_step(c, carry):
        f0 = c * 2 * C
        b1 = (n_pairs - 1 - c) * 2 * C
        pair_starts = ((f0, f0 + C), (b1 + C, b1))

        def row_start(s, d, p):
            return pl.multiple_of(s * seq + pair_starts[d][p], C)

        streams = [(s, h, d) for s in range(n_seq) for h in range(HG_GROUP) for d in range(2)]
        prep = []
        for s, h, d in streams:
            z_ref, lb_ref, _, reverse, _, mid, last = dirs[d]
            cols = slice(h * HG_DK, (h + 1) * HG_DK)
            p0, p1 = (chunk_prep(z_ref, lb_ref, reverse, mid, last, pl.ds(row_start(s, d, p), C), cols)
                      for p in range(2))
            cat = lambda a, b: jnp.concatenate([a, b], axis=0)
            prep.append(dict(
                v=cat(p0["v"], p1["v"]),
                a_lhs=cat(jnp.concatenate([p0["qe"], zeros, zeros], axis=1),
                          jnp.concatenate([zeros, p1["qb"].astype(BF16), p1["qe"]], axis=1)),
                a_rhs=cat(jnp.concatenate([p0["ke"], p0["kl"].astype(BF16), zeros], axis=1),
                          jnp.concatenate([zeros, zeros, p1["ke"]], axis=1)),
                qb=cat(p0["qb"], p1["qb"] * p0["decay"]).astype(BF16),
                kl=cat(p0["kl"] * p1["decay"], p1["kl"]).astype(BF16),
                decay=p0["decay"] * p1["decay"]))
        a_raw = [lax.dot_general(p["a_lhs"], p["a_rhs"], NT_DIMS, preferred_element_type=F32) for p in prep]
        states = [st_scr[state_slot(s, h, d)] for s, h, d in streams]
        o_inter = [lax.dot_general(p["qb"], st.astype(BF16), NT_DIMS, preferred_element_type=F32)
                   for p, st in zip(prep, states)]
        u_t = [lax.dot_general(p["v"], p["kl"], TN_DIMS, preferred_element_type=F32) for p in prep]
        o_intra = [jnp.dot(jnp.where(dirs[d][4], a, 0.0).astype(BF16), p["v"], preferred_element_type=F32)
                   for (s, h, d), a, p in zip(streams, a_raw, prep)]
        for i, (s, h, d) in enumerate(streams):
            cols = slice(h * HG_DK, (h + 1) * HG_DK)
            o = o_intra[i] + o_inter[i]
            dirs[d][2][pl.ds(row_start(s, d, 0), C), cols] = o[:C]
            dirs[d][2][pl.ds(row_start(s, d, 1), C), cols] = o[C:]
            st_scr[state_slot(s, h, d)] = states[i] * prep[i]["decay"] + u_t[i]
        return carry

    lax.fori_loop(0, n_pairs, pair_step, 0, unroll=4)

    for h in range(HG_GROUP):
        cols = slice(h * HG_DK, (h + 1) * HG_DK)
        o = of_scr[:, cols] + ob_scr[:, cols]
        o = o * lax.rsqrt(jnp.mean(o * o, -1, keepdims=True) + RMS_EPS) * ng_ref[...]
        gt = hgt_ref[:, cols]
        o_ref[:, cols] = (o * (gt * _sigmoid(gt))).astype(o_ref.dtype)
        if want_state:
            for s in range(n_seq):
                sf_ref[s, h] = st_scr[state_slot(s, h, 0)].T
                sb_ref[s, h] = st_scr[state_slot(s, h, 1)].T


def _hgrn(z, lb_f, lb_b, norm_g, row0, n_batch, seq, init_states=None, layer=None, state_bufs=None):
    n_groups = HG_HEADS // HG_GROUP
    n_seq = HG_SEQS if seq * HG_SEQS <= HG_MAX_ROWS else 1
    rows = n_seq * seq
    r0 = row0 // rows
    has_init = init_states is not None
    want_state = state_bufs is not None

    def zspec(off):
        return pl.BlockSpec((rows, HG_GW), lambda b, g, o=off // HG_GW: (r0 + b, o + g))

    lbspec = pl.BlockSpec((1, HG_GW), lambda b, g: (0, g))
    in_specs = [zspec(HQ_OFF), zspec(HFF_OFF), zspec(HFB_OFF), zspec(HI_OFF), zspec(HGT_OFF),
                lbspec, lbspec, pl.BlockSpec((1, HG_DV), lambda b, g: (0, 0))]
    args = [z, z, z, z, z, lb_f.reshape(1, HG_DIM), lb_b.reshape(1, HG_DIM), norm_g.reshape(1, HG_DV)]
    if has_init:
        st_spec = pl.BlockSpec((n_seq, None, HG_GROUP, HG_DK, HG_DV), lambda b, g: (b, layer, g, 0, 0))
        in_specs += [st_spec, st_spec]
        args += list(init_states)
    out_specs = [pl.BlockSpec((rows, HG_GW), lambda b, g: (b, g))]
    out_shape = [jax.ShapeDtypeStruct((n_batch * seq, HG_VDIM), BF16)]
    aliases = {}
    if want_state:
        so = pl.BlockSpec((None, n_seq, HG_GROUP, HG_DK, HG_DV), lambda b, g: (layer, b, g, 0, 0))
        for buf in state_bufs:
            aliases[len(args)] = len(out_specs)
            in_specs.append(pl.BlockSpec(memory_space=pl.ANY))
            args.append(buf)
            out_specs.append(so)
            out_shape.append(jax.ShapeDtypeStruct(buf.shape, buf.dtype))
    return pl.pallas_call(
        functools.partial(_hgrn_kernel, seq=seq, n_seq=n_seq, has_init=has_init, want_state=want_state),
        grid=(n_batch // n_seq, n_groups),
        in_specs=in_specs,
        out_specs=out_specs,
        out_shape=out_shape,
        scratch_shapes=[pltpu.VMEM((rows, HG_GW), F32), pltpu.VMEM((rows, HG_GW), F32),
                        pltpu.VMEM((rows, HG_GW), F32),
                        pltpu.VMEM((n_seq * 2 * HG_GROUP, HG_DV, HG_DK), F32)],
        input_output_aliases=aliases,
        compiler_params=_params(2),
        name=f"hgrn_t{seq}",
    )(*args)


def _oproj_kernel(*refs, n_x, n_ctx_tiles):
    attc_ref, attl_ref, hgc_ref, hgl_ref, w_ref = refs[:5]
    x_refs = refs[5:5 + n_x]
    mod_ref, g_ref, b_ref, o_ref = refs[5 + n_x:]
    is_ctx = pl.program_id(0) < n_ctx_tiles
    for r0 in range(0, OPROJ_TM, OPROJ_SUB):
        rows = slice(r0, r0 + OPROJ_SUB)
        att = jnp.where(is_ctx, attc_ref[rows], attl_ref[rows])
        hg = jnp.where(is_ctx, hgc_ref[rows], hgl_ref[rows])
        mix = (jnp.dot(att, w_ref[0:ATT_DIM, :], preferred_element_type=F32)
               + jnp.dot(hg, w_ref[ATT_DIM:MIX_DIM, :], preferred_element_type=F32))
        y = DEEPNORM_ALPHA * _pick_rows(x_refs, is_ctx, rows) + mod_ref[2:3, :] * mix
        o_ref[rows] = _layer_norm(y, g_ref[0:1, :], b_ref[0:1, :])


def _oproj(att_c, att_l, hg_c, hg_l, w_o, xs, mods, ln_g, ln_b, layer, mod_idx):
    n_tok = sum(x.shape[0] for x in xs)
    n_ctx_tiles = att_c.shape[0] // OPROJ_TM
    row = lambda i: (i, 0)
    ctx_row = lambda i: (jnp.minimum(i, n_ctx_tiles - 1), 0)
    lat_row = lambda i: (jnp.maximum(i - n_ctx_tiles, 0), 0)
    return pl.pallas_call(
        functools.partial(_oproj_kernel, n_x=len(xs), n_ctx_tiles=n_ctx_tiles),
        grid=(n_tok // OPROJ_TM,),
        in_specs=[pl.BlockSpec((OPROJ_TM, ATT_DIM), ctx_row),
                  pl.BlockSpec((OPROJ_TM, ATT_DIM), lat_row),
                  pl.BlockSpec((OPROJ_TM, HG_VDIM), ctx_row),
                  pl.BlockSpec((OPROJ_TM, HG_VDIM), lat_row),
                  pl.BlockSpec((MIX_DIM, D_MODEL), lambda i: (0, 0))]
        + _token_specs(xs, OPROJ_TM, n_ctx_tiles, 1) + [
                  pl.BlockSpec((None, None, 6, D_MODEL), lambda i: (layer, mod_idx(i), 0, 0)),
                  pl.BlockSpec((None, 2, D_MODEL), lambda i: (layer, 0, 0)),
                  pl.BlockSpec((None, 2, D_MODEL), lambda i: (layer, 0, 0))],
        out_specs=pl.BlockSpec((OPROJ_TM, D_MODEL), row),
        out_shape=jax.ShapeDtypeStruct((n_tok, D_MODEL), F32),
        compiler_params=_params(1),
        name="oproj_ln",
    )(att_c, att_l, hg_c, hg_l, w_o, *xs, mods, ln_g, ln_b)


def _mlp_kernel(*refs, n_cast, n_out, n_ctx_tiles):
    x_ref, mod_ref, wu_ref, wd_ref, g_ref, b_ref = refs[:6]
    cast_in = refs[6:6 + n_cast]
    outs = refs[6 + n_cast:6 + n_cast + n_out]
    cast_out = refs[6 + n_cast + n_out:6 + 2 * n_cast + n_out]
    h_ref, acc_ref = refs[6 + 2 * n_cast + n_out:]
    i, f = pl.program_id(0), pl.program_id(1)
    n_f = pl.num_programs(1)

    @pl.when(f == 0)
    def _():
        shift, scale = mod_ref[3:4, :], mod_ref[4:5, :]
        h_ref[...] = (x_ref[...] * (1.0 + scale) + shift).astype(BF16)
        acc_ref[...] = jnp.zeros_like(acc_ref)

    if n_cast:
        @pl.when(i * n_f + f < CAST_STEPS)
        def _():
            for src, dst in zip(cast_in, cast_out):
                dst[...] = src[...].astype(BF16)

    u = jnp.dot(h_ref[...], wu_ref[...], preferred_element_type=F32)
    u = jnp.square(jnp.maximum(u, 0.0)).astype(BF16)
    for n in range(0, D_MODEL, MLP_TN):
        acc_ref[:, n:n + MLP_TN] += jnp.dot(u, wd_ref[:, n:n + MLP_TN], preferred_element_type=F32)

    def finish(o_ref):
        y = DEEPNORM_ALPHA * x_ref[...] + mod_ref[5:6, :] * acc_ref[...]
        o_ref[...] = _layer_norm(y, g_ref[1:2, :], b_ref[1:2, :])

    last = f == n_f - 1
    if n_out == 1:
        pl.when(last)(lambda: finish(outs[0]))
    else:
        pl.when(jnp.logical_and(last, i < n_ctx_tiles))(lambda: finish(outs[0]))
        pl.when(jnp.logical_and(last, i >= n_ctx_tiles))(lambda: finish(outs[1]))


def _cast_specs(weights, layer):
    n_f = D_FF // MLP_TF
    in_specs, out_specs, out_shape = [], [], []
    for w in weights:
        _, rows, cols = w.shape
        slab = rows // CAST_STEPS
        in_specs.append(pl.BlockSpec(
            (None, slab, cols), lambda i, f: (layer, jnp.minimum(i * n_f + f, CAST_STEPS - 1), 0)))
        out_specs.append(pl.BlockSpec(
            (slab, cols), lambda i, f: (jnp.minimum(i * n_f + f, CAST_STEPS - 1), 0)))
        out_shape.append(jax.ShapeDtypeStruct((rows, cols), BF16))
    return in_specs, out_specs, out_shape


def _mlp(x, mods, w_up, w_down, ln_g, ln_b, layer, mod_idx, next_weights=(), split_rows=None):
    n_tok = x.shape[0]
    n_f = D_FF // MLP_TF
    assert (n_tok // MLP_TM) * n_f >= CAST_STEPS
    cast_in_specs, cast_out_specs, cast_shapes = _cast_specs(next_weights, layer + 1)
    if split_rows is None:
        n_ctx_tiles = n_tok // MLP_TM
        y_specs = [pl.BlockSpec((MLP_TM, D_MODEL), lambda i, f: (i, 0))]
        y_shapes = [jax.ShapeDtypeStruct((n_tok, D_MODEL), F32)]
    else:
        n_ctx_tiles = split_rows // MLP_TM
        y_specs = [pl.BlockSpec((MLP_TM, D_MODEL), lambda i, f: (jnp.minimum(i, n_ctx_tiles - 1), 0)),
                   pl.BlockSpec((MLP_TM, D_MODEL), lambda i, f: (jnp.maximum(i - n_ctx_tiles, 0), 0))]
        y_shapes = [jax.ShapeDtypeStruct((split_rows, D_MODEL), F32),
                    jax.ShapeDtypeStruct((n_tok - split_rows, D_MODEL), F32)]
    return pl.pallas_call(
        functools.partial(_mlp_kernel, n_cast=len(next_weights), n_out=len(y_specs),
                          n_ctx_tiles=n_ctx_tiles),
        grid=(n_tok // MLP_TM, n_f),
        in_specs=[pl.BlockSpec((MLP_TM, D_MODEL), lambda i, f: (i, 0)),
                  pl.BlockSpec((None, None, 6, D_MODEL), lambda i, f: (layer, mod_idx(i), 0, 0)),
                  pl.BlockSpec((D_MODEL, MLP_TF), lambda i, f: (0, f)),
                  pl.BlockSpec((MLP_TF, D_MODEL), lambda i, f: (f, 0)),
                  pl.BlockSpec((None, 2, D_MODEL), lambda i, f: (layer, 0, 0)),
                  pl.BlockSpec((None, 2, D_MODEL), lambda i, f: (layer, 0, 0))] + cast_in_specs,
        out_specs=y_specs + cast_out_specs,
        out_shape=y_shapes + cast_shapes,
        scratch_shapes=[pltpu.VMEM((MLP_TM, D_MODEL), BF16), pltpu.VMEM((MLP_TM, D_MODEL), F32)],
        compiler_params=_params(2),
        name="mlp_ln",
    )(x, mods, w_up, w_down, ln_g, ln_b, *next_weights)


def _cache_pack_kernel(*refs, depth):
    k_refs, v_refs = refs[:depth], refs[depth:2 * depth]
    ok_ref, ov_ref = refs[2 * depth:]
    for l in range(depth):
        for h in range(KV_HEADS):
            cols = slice(h * HEAD_DIM, (h + 1) * HEAD_DIM)
            ok_ref[0, l, :, h, :] = k_refs[l][:, cols]
            ov_ref[0, l, :, h, :] = v_refs[l][:, cols]


def _cache_pack(ks, vs, n_batch, seq):
    depth = len(ks)
    in_spec = pl.BlockSpec((seq, KV_DIM), lambda b: (b, 0))
    out_spec = pl.BlockSpec((1, depth, seq, KV_HEADS, HEAD_DIM), lambda b: (b, 0, 0, 0, 0))
    out_shape = jax.ShapeDtypeStruct((n_batch, depth, seq, KV_HEADS, HEAD_DIM), F32)
    return pl.pallas_call(
        functools.partial(_cache_pack_kernel, depth=depth),
        grid=(n_batch,),
        in_specs=[in_spec] * (2 * depth),
        out_specs=[out_spec, out_spec],
        out_shape=[out_shape, out_shape],
        compiler_params=_params(1),
        name="cache_pack",
    )(*ks, *vs)


def _lower_bounds(lb_logits):
    p = jax.nn.softmax(lb_logits.astype(F32), axis=0)
    cs = jnp.cumsum(p, axis=0)
    return cs - cs[0:1]


def kernel(x_prompt, x_sample, cache_k, cache_v, state_hgrn_fwd, state_hgrn_bwd, c, c_ctx, w_mod, b_mod,
           w_in, attn_sink, attn_norm_g, hg_lb_logits, hg_norm_g, w_o, ln_g, ln_b, w_up, w_down):
    bp, seq_p, d = x_prompt.shape
    bs, seq_s, _ = x_sample.shape
    depth = w_in.shape[0]
    past = cache_k.shape[2]
    n_ctx, n_lat = bp * seq_p, bs * seq_s

    xs = (x_prompt.reshape(n_ctx, d), x_sample.reshape(n_lat, d))
    c_rows = jnp.concatenate([c_ctx[None, :], c, jnp.zeros((MOD_ROWS - 1 - bs, d), F32)], 0)
    mods = _modulation(c_rows, w_mod, b_mod)[:, :1 + bs].reshape(depth, 1 + bs, 6, d)

    weights = (w_in, w_o, w_up, w_down)
    w_in_b, w_o_b, w_up_b, w_down_b = (w[0].astype(BF16) for w in weights)
    lb_f, lb_b = _lower_bounds(hg_lb_logits[0]), _lower_bounds(hg_lb_logits[1])
    rope_tabs = _rope_tables(seq_s)
    ck = cache_k.reshape(bs, depth, past, KV_DIM)
    cv = cache_v.reshape(bs, depth, past, KV_DIM)

    new_k, new_v = [], []
    s_f = jnp.zeros((depth, bp, HG_HEADS, HG_DK, HG_DV), F32)
    s_b = jnp.zeros((depth, bp, HG_HEADS, HG_DK, HG_DV), F32)
    for l in range(depth):
        z, k_c, v_c = _inproj(xs, mods, w_in_b, l, _mod_index(_inproj_tile(xs), n_ctx, seq_s), n_ctx)
        new_k.append(k_c)
        new_v.append(v_c)

        att_c = _ctx_attention(z, attn_sink[l], attn_norm_g[l], bp, seq_p)
        att_l = _lat_attention(z, ck, cv, l, attn_sink[l], attn_norm_g[l], rope_tabs, n_ctx, bs, seq_s)

        hg_c, s_f, s_b = _hgrn(z, lb_f[l], lb_b[l], hg_norm_g[l], 0, bp, seq_p, layer=l,
                               state_bufs=(s_f, s_b))
        (hg_l,) = _hgrn(z, lb_f[l], lb_b[l], hg_norm_g[l], n_ctx, bs, seq_s,
                        init_states=(state_hgrn_fwd, state_hgrn_bwd), layer=l)

        x = _oproj(att_c, att_l, hg_c, hg_l, w_o_b, xs, mods, ln_g, ln_b, l,
                   _mod_index(OPROJ_TM, n_ctx, seq_s))
        mlp_mods = _mod_index(MLP_TM, n_ctx, seq_s)
        if l + 1 < depth:
            x, w_in_b, w_o_b, w_up_b, w_down_b = _mlp(x, mods, w_up_b, w_down_b, ln_g, ln_b, l, mlp_mods,
                                                      next_weights=weights)
            xs = (x,)
        else:
            y_p, y_s = _mlp(x, mods, w_up_b, w_down_b, ln_g, ln_b, l, mlp_mods, split_rows=n_ctx)

    new_cache_k, new_cache_v = _cache_pack(new_k, new_v, bp, seq_p)
    return (y_p.reshape(bp, seq_p, d), y_s.reshape(bs, seq_s, d), new_cache_k, new_cache_v,
            jnp.swapaxes(s_f, 0, 1), jnp.swapaxes(s_b, 0, 1))
```
